```python
import math
import jax
import jax.numpy as jnp
from jax import lax
import numpy as np

D_MODEL = 1024
BATCH = 1
SEQ = 16384
DEPTH = 2
DEC_BATCH = 16
DEC_SEQ = 16
PAST_LEN = 4096

CHUNK = 64
D_MIX = 1024
HEAD_DIM = 64
A_HEADS = 8
A_WIDTH = 512
A_PREV_CHUNKS = 8
REL_CLIP = 128
B_WIDTH = 256
CONV_WIDTH = 3
C_HEADS = 4
C_QK_DIM = 32
C_V_DIM = 64
C_WIDTH = 256
ROPE_DIMS = 8
ROPE_THETA = 500000.0
Q_BLOCK = 128
NORM_EPS = 1e-6
SUBLN_EPS = 1e-5
SPLITS = (512, 512, 512, 512, 256, 256, 256, 256, 256, 256, 256, 256)
D_IN_PROJ = 4096

kernel_name = 'chunk_hybrid_stream_encoder_step'


def rmsnorm(x, g, eps=NORM_EPS):
    xf = x.astype(jnp.float32)
    y = xf * lax.rsqrt(jnp.mean(xf * xf, axis=-1, keepdims=True) + eps)
    return (y * g.astype(jnp.float32)).astype(x.dtype)


def split_projection(h, w_in):
    z = jnp.einsum('bsd,de->bse', h, w_in)
    bounds = np.cumsum(np.array(SPLITS))[:-1].tolist()
    return jnp.split(z, bounds, axis=-1)


def partial_rope(x, pos):
    half = ROPE_DIMS // 2
    inv_freq = ROPE_THETA ** (-jnp.arange(half, dtype=jnp.float32) * (2.0 / ROPE_DIMS))
    ang = pos.astype(jnp.float32)[:, None] * inv_freq[None, :]
    cos = jnp.cos(ang)[:, None, None, :]
    sin = jnp.sin(ang)[:, None, None, :]
    xr = x[..., :ROPE_DIMS].astype(jnp.float32)
    x1, x2 = xr[..., :half], xr[..., half:]
    rot = jnp.concatenate([x1 * cos - x2 * sin, x2 * cos + x1 * sin], axis=-1).astype(x.dtype)
    return jnp.concatenate([rot, x[..., ROPE_DIMS:]], axis=-1)


def rel_bias_lookup(table, dist):
    idx = jnp.clip(dist, -REL_CLIP, REL_CLIP) + REL_CLIP
    return table.astype(jnp.float32)[:, idx]


def band_attention_prompt(q, k, v, rel_bias):
    B, S, H, Dh = q.shape
    nc = S // CHUNK
    band = (A_PREV_CHUNKS + 1) * CHUNK
    pad = ((0, 0), (A_PREV_CHUNKS * CHUNK, 0), (0, 0), (0, 0))
    kp = jnp.pad(k, pad).reshape(B, nc + A_PREV_CHUNKS, CHUNK, H, Dh)
    vp = jnp.pad(v, pad).reshape(B, nc + A_PREV_CHUNKS, CHUNK, H, Dh)
    kb = jnp.concatenate([kp[:, j:j + nc] for j in range(A_PREV_CHUNKS + 1)], axis=2)
    vb = jnp.concatenate([vp[:, j:j + nc] for j in range(A_PREV_CHUNKS + 1)], axis=2)
    qc = q.reshape(B, nc, CHUNK, H, Dh)
    s = jnp.einsum('bcqhd,bckhd->bchqk', qc, kb).astype(jnp.float32) * (Dh ** -0.5)
    r = jnp.arange(band)
    dist = jnp.arange(CHUNK)[:, None] + A_PREV_CHUNKS * CHUNK - r[None, :]
    s = s + rel_bias_lookup(rel_bias, dist)[None, None]
    key_pos = jnp.arange(nc)[:, None] * CHUNK - A_PREV_CHUNKS * CHUNK + r[None, :]
    s = jnp.where((key_pos >= 0)[None, :, None, None, :], s, -jnp.inf)
    p = jax.nn.softmax(s, axis=-1).astype(v.dtype)
    o = jnp.einsum('bchqk,bckhd->bcqhd', p, vb)
    return o.reshape(B, S, H * Dh)


def band_attention_sample(q, k, v, cache_k, cache_v, rel_bias):
    B, T, H, Dh = q.shape
    W = cache_k.shape[1]
    kk = jnp.concatenate([cache_k, k], axis=1)
    vv = jnp.concatenate([cache_v, v], axis=1)
    s = jnp.einsum('bqhd,bkhd->bhqk', q, kk).astype(jnp.float32) * (Dh ** -0.5)
    dist = jnp.arange(T)[:, None] + W - jnp.arange(W + T)[None, :]
    s = s + rel_bias_lookup(rel_bias, dist)[None]
    p = jax.nn.softmax(s, axis=-1).astype(v.dtype)
    o = jnp.einsum('bhqk,bkhd->bqhd', p, vv)
    return o.reshape(B, T, H * Dh)


def causal_conv(up, w, T):
    out = up[:, 0:T] * w[0]
    for j in range(1, CONV_WIDTH):
        out = out + up[:, j:j + T] * w[j]
    return out


def diff_lambda(lq1, lk1, lq2, lk2, lam_init):
    f = lambda a: a.astype(jnp.float32)
    return jnp.exp(jnp.sum(f(lq1) * f(lk1))) - jnp.exp(jnp.sum(f(lq2) * f(lk2))) + lam_init


def diff_combine(s, lam, v):
    p = jax.nn.softmax(s, axis=-1)
    a = p[:, :, 0] - lam * p[:, :, 1]
    return jnp.einsum('bhqk,bkhe->bqhe', a.astype(v.dtype), v)


def diff_attention_prompt(q, k, v, lam):
    B, S, H, _, d = q.shape
    nb = S // Q_BLOCK
    q_blocks = jnp.moveaxis(q.reshape(B, nb, Q_BLOCK, H, 2, d), 1, 0)
    key_chunk = jnp.arange(S) // CHUNK

    def one_block(args):
        qb, bi = args
        q_chunk = (bi * Q_BLOCK + jnp.arange(Q_BLOCK)) // CHUNK
        s = jnp.einsum('bqhmd,bkhmd->bhmqk', qb, k).astype(jnp.float32) * (d ** -0.5)
        s = jnp.where((key_chunk[None, :] <= q_chunk[:, None])[None, None, None], s, -jnp.inf)
        return diff_combine(s, lam, v)

    o = lax.map(one_block, (q_blocks, jnp.arange(nb)))
    return jnp.moveaxis(o, 0, 1).reshape(B, S, H, -1)


def diff_attention_sample(q, k, v, cache_k, cache_v, lam):
    d = q.shape[-1]
    kk = jnp.concatenate([cache_k, k], axis=1)
    vv = jnp.concatenate([cache_v, v], axis=1)
    s = jnp.einsum('bqhmd,bkhmd->bhmqk', q, kk).astype(jnp.float32) * (d ** -0.5)
    return diff_combine(s, lam, vv)


def merge_branches(o_a, a_g, o_b, b_g, o_c, c_g, w_out):
    y = jnp.concatenate([o_a * jax.nn.silu(a_g), o_b * jax.nn.silu(b_g), o_c * jax.nn.silu(c_g)], axis=-1)
    return jnp.einsum('bse,ed->bsd', y, w_out)


def prompt_layer(x, g, w_in, w_out, rel_bias, conv_w, subln_g, lam, lam_init):
    B, S, _ = x.shape
    h = rmsnorm(x, g)
    a_q, a_k, a_v, a_g, b_b, b_c, b_h, b_g, c_q, c_k, c_v, c_g = split_projection(h, w_in)
    k = a_k.reshape(B, S, A_HEADS, HEAD_DIM)
    v = a_v.reshape(B, S, A_HEADS, HEAD_DIM)
    o_a = band_attention_prompt(a_q.reshape(B, S, A_HEADS, HEAD_DIM), k, v, rel_bias)
    keep = min(A_PREV_CHUNKS * CHUNK, S)
    up = jnp.pad(b_c * b_h, ((0, 0), (CONV_WIDTH - 1, 0), (0, 0)))
    o_b = b_b * causal_conv(up, conv_w, S)
    pos = jnp.arange(S)
    cq = partial_rope(c_q.reshape(B, S, C_HEADS, 2, C_QK_DIM), pos)
    ck = partial_rope(c_k.reshape(B, S, C_HEADS, 2, C_QK_DIM), pos)
    cv = c_v.reshape(B, S, C_HEADS, C_V_DIM)
    o = diff_attention_prompt(cq, ck, cv, lam)
    o_c = (rmsnorm(o, subln_g, SUBLN_EPS) * (1.0 - lam_init)).reshape(B, S, C_WIDTH)
    y = x + merge_branches(o_a, a_g, o_b, b_g, o_c, c_g, w_out)
    return y, (k[:, S - keep:], v[:, S - keep:], up[:, -(CONV_WIDTH - 1):], ck, cv)


def sample_layer(x, ca_k, ca_v, c_conv, cc_k, cc_v, g, w_in, w_out, rel_bias, conv_w, subln_g, lam, lam_init):
    B, T, _ = x.shape
    past = cc_k.shape[1]
    h = rmsnorm(x, g)
    a_q, a_k, a_v, a_g, b_b, b_c, b_h, b_g, c_q, c_k, c_v, c_g = split_projection(h, w_in)
    k = a_k.reshape(B, T, A_HEADS, HEAD_DIM)
    v = a_v.reshape(B, T, A_HEADS, HEAD_DIM)
    o_a = band_attention_sample(a_q.reshape(B, T, A_HEADS, HEAD_DIM), k, v, ca_k, ca_v, rel_bias)
    up = jnp.concatenate([c_conv, b_c * b_h], axis=1)
    o_b = b_b * causal_conv(up, conv_w, T)
    pos = past + jnp.arange(T)
    cq = partial_rope(c_q.reshape(B, T, C_HEADS, 2, C_QK_DIM), pos)
    ck = partial_rope(c_k.reshape(B, T, C_HEADS, 2, C_QK_DIM), pos)
    cv = c_v.reshape(B, T, C_HEADS, C_V_DIM)
    o = diff_attention_sample(cq, ck, cv, cc_k, cc_v, lam)
    o_c = (rmsnorm(o, subln_g, SUBLN_EPS) * (1.0 - lam_init)).reshape(B, T, C_WIDTH)
    y = x + merge_branches(o_a, a_g, o_b, b_g, o_c, c_g, w_out)
    return y, (k, v, up[:, -(CONV_WIDTH - 1):], ck, cv)


def setup_inputs(seed: int = 0) -> dict:
    key = jax.random.key(seed)
    ks = jax.random.split(key, 20)
    a_win = min(A_PREV_CHUNKS * CHUNK, PAST_LEN)

    def nrm(k, shape, scale=1.0):
        return scale * jax.random.normal(k, shape, dtype=jnp.float32)

    return {
        'x_prompt': nrm(ks[0], (BATCH, SEQ, D_MODEL)),
        'x_sample': nrm(ks[1], (DEC_BATCH, DEC_SEQ, D_MODEL)),
        'cache_a_k': nrm(ks[2], (DEPTH, DEC_BATCH, a_win, A_HEADS, HEAD_DIM)),
        'cache_a_v': nrm(ks[3], (DEPTH, DEC_BATCH, a_win, A_HEADS, HEAD_DIM)),
        'state_conv': nrm(ks[4], (DEPTH, DEC_BATCH, CONV_WIDTH - 1, B_WIDTH)),
        'cache_c_k': nrm(ks[5], (DEPTH, DEC_BATCH, PAST_LEN, C_HEADS, 2, C_QK_DIM)),
        'cache_c_v': nrm(ks[6], (DEPTH, DEC_BATCH, PAST_LEN, C_HEADS, C_V_DIM)),
        'norm_g': 1.0 + nrm(ks[7], (DEPTH, D_MODEL), 0.05),
        'w_in': nrm(ks[8], (DEPTH, D_MODEL, D_IN_PROJ), D_MODEL ** -0.5),
        'w_out': nrm(ks[9], (DEPTH, D_MIX, D_MODEL), D_MIX ** -0.5),
        'rel_bias': nrm(ks[10], (DEPTH, A_HEADS, 2 * REL_CLIP + 1), 0.2),
        'conv_w': nrm(ks[11], (DEPTH, CONV_WIDTH, B_WIDTH), CONV_WIDTH ** -0.5),
        'lam_q1': nrm(ks[12], (DEPTH, C_QK_DIM), 0.1),
        'lam_k1': nrm(ks[13], (DEPTH, C_QK_DIM), 0.1),
        'lam_q2': nrm(ks[14], (DEPTH, C_QK_DIM), 0.1),
        'lam_k2': nrm(ks[15], (DEPTH, C_QK_DIM), 0.1),
        'subln_g': 1.0 + nrm(ks[16], (DEPTH, C_V_DIM), 0.05),
        'final_g': 1.0 + nrm(ks[17], (D_MODEL,), 0.05),
    }


def reference(x_prompt, x_sample, cache_a_k, cache_a_v, state_conv, cache_c_k, cache_c_v,
              norm_g, w_in, w_out, rel_bias, conv_w, lam_q1, lam_k1, lam_q2, lam_k2, subln_g, final_g):
    xp, xs = x_prompt, x_sample
    p_ak, p_av, p_conv, p_ck, p_cv = [], [], [], [], []
    s_ak, s_av, s_conv, s_ck, s_cv = [], [], [], [], []
    for l in range(DEPTH):
        lam_init = 0.8 - 0.6 * math.exp(-0.3 * l)
        lam = diff_lambda(lam_q1[l], lam_k1[l], lam_q2[l], lam_k2[l], lam_init)
        xp, (ak, av, cs, ck, cv) = prompt_layer(xp, norm_g[l], w_in[l], w_out[l], rel_bias[l],
                                                conv_w[l], subln_g[l], lam, lam_init)
        p_ak.append(ak); p_av.append(av); p_conv.append(cs); p_ck.append(ck); p_cv.append(cv)
        xs, (ak, av, cs, ck, cv) = sample_layer(xs, cache_a_k[l], cache_a_v[l], state_conv[l],
                                                cache_c_k[l], cache_c_v[l], norm_g[l], w_in[l], w_out[l],
                                                rel_bias[l], conv_w[l], subln_g[l], lam, lam_init)
        s_ak.append(ak); s_av.append(av); s_conv.append(cs); s_ck.append(ck); s_cv.append(cv)
    y_prompt = rmsnorm(xp, final_g)
    y_sample = rmsnorm(xs, final_g)
    return (y_prompt, y_sample,
            jnp.stack(p_ak), jnp.stack(p_av), jnp.stack(p_conv), jnp.stack(p_ck), jnp.stack(p_cv),
            jnp.stack(s_ak), jnp.stack(s_av), jnp.stack(s_conv), jnp.stack(s_ck), jnp.stack(s_cv))
```

```python
import functools
import math

import numpy as np
import jax
import jax.numpy as jnp
from jax import lax
from jax.experimental import pallas as pl
from jax.experimental.pallas import tpu as pltpu

F32 = jnp.float32
BF16 = jnp.bfloat16

CHUNK = 64
HEAD_DIM = 64
A_HEADS = 8
A_WIDTH = A_HEADS * HEAD_DIM
A_PREV_CHUNKS = 8
A_WIN = A_PREV_CHUNKS * CHUNK
REL_CLIP = 128
B_WIDTH = 256
CONV_WIDTH = 3
C_HEADS = 4
C_QK_DIM = 32
C_V_DIM = 64
C_WIDTH = C_HEADS * C_V_DIM
C_MAPS = 2 * C_HEADS
ROPE_DIMS = 8
ROPE_THETA = 500000.0
NORM_EPS = 1e-6
SUBLN_EPS = 1e-5
D_IN_PROJ = 4096
COL_AQ, COL_AK, COL_AV, COL_AG = 0, 512, 1024, 1536
COL_B = 2048
COL_CQ, COL_CK, COL_CV, COL_CG = 3072, 3328, 3584, 3840

LANES = 128
HALF = LANES // 2
NEG = -1e30
VMEM_LIMIT = 48 * 1024 * 1024

PROJ_COLS = 512
TQ_A = 256
WIN_A = A_WIN + TQ_A
ROLL_A = 1024
N_BIAS = 4
T_C = 512


def _nt_dot(a, b):
    return lax.dot_general(a, b, (((1,), (1,)), ((), ())), preferred_element_type=F32)


def _cparams(n_axes=1):
    return pltpu.CompilerParams(dimension_semantics=("parallel",) * n_axes,
                                vmem_limit_bytes=VMEM_LIMIT)


def _resident(shape):
    nd = len(shape)
    return pl.BlockSpec(shape, lambda *_: (0,) * nd, pipeline_mode=pl.Buffered(1))


def _rope(x, cos, sa, sb):
    return (x * cos + pltpu.roll(x, LANES - ROPE_DIMS // 2, 1) * sa
            + pltpu.roll(x, ROPE_DIMS // 2, 1) * sb)


def _inproj_kernel(x_ref, g_ref, w_ref, cos_ref, sa_ref, sb_ref,
                   z_ref, qa_ref, ka_ref, va_ref, qc_ref, kc_ref, vc_ref, ckr_ref):
    x = x_ref[...]
    ms = jnp.mean(x * x, axis=-1, keepdims=True)
    h = (x * lax.rsqrt(ms + NORM_EPS) * g_ref[...]).astype(BF16)
    cos, sa, sb = cos_ref[...], sa_ref[...], sb_ref[...]
    for c in range(D_IN_PROJ // PROJ_COLS):
        lo = c * PROJ_COLS
        zc = jnp.dot(h, w_ref[:, lo:lo + PROJ_COLS], preferred_element_type=F32)
        z_ref[:, lo:lo + PROJ_COLS] = zc
        if lo == COL_AQ:
            qa_ref[...] = (zc * (HEAD_DIM ** -0.5)).astype(BF16)
        elif lo == COL_AK:
            ka_ref[...] = zc.astype(BF16)
        elif lo == COL_AV:
            va_ref[...] = zc.astype(BF16)
        elif lo == COL_CQ:
            for t in range(C_WIDTH // LANES):
                cq = _rope(zc[:, t * LANES:(t + 1) * LANES], cos, sa, sb)
                qc_ref[:, t * LANES:(t + 1) * LANES] = (cq * (C_QK_DIM ** -0.5)).astype(BF16)
                ck = _rope(zc[:, C_WIDTH + t * LANES:C_WIDTH + (t + 1) * LANES], cos, sa, sb)
                kc_ref[:, t * LANES:(t + 1) * LANES] = ck.astype(BF16)
                ckr_ref[:, t * LANES:(t + 1) * LANES] = ck
        elif lo == COL_CV:
            vc_ref[...] = zc[:, :C_WIDTH].astype(BF16)


def _inproj(x, g, w_b, cos, sa, sb, ts):
    n = x.shape[0]
    d = x.shape[1]
    row = lambda width: pl.BlockSpec((ts, width), lambda i: (i, 0))
    out_shape = (
        jax.ShapeDtypeStruct((n, D_IN_PROJ), F32),
        jax.ShapeDtypeStruct((n, A_WIDTH), BF16),
        jax.ShapeDtypeStruct((n, A_WIDTH), BF16),
        jax.ShapeDtypeStruct((n, A_WIDTH), BF16),
        jax.ShapeDtypeStruct((n, C_WIDTH), BF16),
        jax.ShapeDtypeStruct((n, C_WIDTH), BF16),
        jax.ShapeDtypeStruct((n, C_WIDTH), BF16),
        jax.ShapeDtypeStruct((n, C_WIDTH), F32),
    )
    return pl.pallas_call(
        _inproj_kernel,
        grid=(n // ts,),
        in_specs=[row(d), _resident((1, d)), _resident((d, D_IN_PROJ)),
                  row(LANES), row(LANES), row(LANES)],
        out_specs=(row(D_IN_PROJ), row(A_WIDTH), row(A_WIDTH), row(A_WIDTH),
                   row(C_WIDTH), row(C_WIDTH), row(C_WIDTH), row(C_WIDTH)),
        out_shape=out_shape,
        compiler_params=_cparams(),
        name="inproj",
    )(x, g, w_b, cos, sa, sb)


def _rope_tables(pos):
    half = ROPE_DIMS // 2
    inv_freq = ROPE_THETA ** (-jnp.arange(half, dtype=F32) * (2.0 / ROPE_DIMS))
    ang = pos.astype(F32)[:, None] * inv_freq[None, :]
    cos, sin = jnp.cos(ang), jnp.sin(ang)
    d = np.arange(LANES) % C_QK_DIM
    j = d % half
    first, second = d < half, (d >= half) & (d < ROPE_DIMS)
    cos_t = jnp.where(d < ROPE_DIMS, cos[:, j], 1.0)
    sa_t = jnp.where(first, -sin[:, j], 0.0)
    sb_t = jnp.where(second, sin[:, j], 0.0)
    return cos_t, sa_t, sb_t


def _bias_kernel(u_ref, b_ref):
    v = pl.program_id(0)
    u = jnp.broadcast_to(u_ref[0, 0], (TQ_A, ROLL_A))
    t = pltpu.roll(u, 0, 1, stride=1, stride_axis=0)[:, ROLL_A - WIN_A:]
    qc = lax.broadcasted_iota(jnp.int32, (TQ_A, WIN_A), 0) // CHUNK
    kc = lax.broadcasted_iota(jnp.int32, (TQ_A, WIN_A), 1) // CHUNK
    top = qc + jnp.minimum(v, 2) * (TQ_A // CHUNK)
    valid = ((kc <= top) & (kc >= top - A_PREV_CHUNKS)) | (v == N_BIAS - 1)
    b_ref[0, 0] = jnp.where(valid, t, NEG)


def _bias_tables(rel_bias_l):
    offs = np.array([0, TQ_A, A_WIN, A_WIN])[:, None]
    m = np.arange(ROLL_A)[None, :]
    idx = np.clip(offs + (ROLL_A - WIN_A) - m, -REL_CLIP, REL_CLIP) + REL_CLIP
    u = jnp.transpose(rel_bias_l.astype(F32)[:, idx], (1, 0, 2))[:, :, None, :]
    return pl.pallas_call(
        _bias_kernel,
        grid=(N_BIAS, A_HEADS),
        in_specs=[pl.BlockSpec((1, 1, 1, ROLL_A), lambda v, h: (v, h, 0, 0))],
        out_specs=pl.BlockSpec((1, 1, TQ_A, WIN_A), lambda v, h: (v, h, 0, 0)),
        out_shape=jax.ShapeDtypeStruct((N_BIAS, A_HEADS, TQ_A, WIN_A), F32),
        compiler_params=_cparams(2),
        name="bias",
    )(u)


def _band_kernel(q_ref, k0_ref, k1_ref, k2_ref, v0_ref, v1_ref, v2_ref, b_ref, o_ref):
    lo = lax.broadcasted_iota(jnp.int32, (TQ_A, LANES), 1) < HALF
    k_refs = (k0_ref, k1_ref, k2_ref)
    v_refs = (v0_ref, v1_ref, v2_ref)
    for p in range(A_WIDTH // LANES):
        cols = slice(p * LANES, (p + 1) * LANES)
        q2 = q_ref[:, cols]
        vw = jnp.concatenate([r[:, cols] for r in v_refs], axis=0)
        outs = []
        for half in range(2):
            sel = lo if half == 0 else jnp.logical_not(lo)
            qm = jnp.where(sel, q2, jnp.zeros_like(q2))
            s = jnp.concatenate([_nt_dot(qm, r[:, cols]) for r in k_refs], axis=1)
            s = s + b_ref[0, 2 * p + half]
            m = jnp.max(s, axis=-1, keepdims=True)
            e = jnp.exp(s - m)
            l = jnp.sum(e, axis=-1, keepdims=True)
            o = jnp.dot(e.astype(BF16), vw, preferred_element_type=F32)
            outs.append(o / l)
        o_ref[:, cols] = jnp.where(lo, outs[0], outs[1])


def _band_attention(qa, ka, va, bias):
    s = qa.shape[0]
    nq = s // TQ_A
    first = lambda i: jnp.maximum(i - A_WIN // TQ_A, 0)
    kv = lambda j: pl.BlockSpec((TQ_A, A_WIDTH), lambda i: (first(i) + j, 0))
    return pl.pallas_call(
        _band_kernel,
        grid=(nq,),
        in_specs=[pl.BlockSpec((TQ_A, A_WIDTH), lambda i: (i, 0)),
                  kv(0), kv(1), kv(2), kv(0), kv(1), kv(2),
                  pl.BlockSpec((1, A_HEADS, TQ_A, WIN_A),
                               lambda i: (jnp.minimum(i, 2), 0, 0, 0))],
        out_specs=pl.BlockSpec((TQ_A, A_WIDTH), lambda i: (i, 0)),
        out_shape=jax.ShapeDtypeStruct((s, A_WIDTH), F32),
        compiler_params=_cparams(),
        name="band_attn",
    )(qa, ka, ka, ka, va, va, va, bias)


def _diff_lambda(lam_ref, lam_init):
    a = jnp.sum(lam_ref[0:1, :] * lam_ref[1:2, :], axis=-1, keepdims=True)
    b = jnp.sum(lam_ref[2:3, :] * lam_ref[3:4, :], axis=-1, keepdims=True)
    return jnp.exp(a) - jnp.exp(b) + lam_init


def _subln(d, g2, lam_init):
    lo = lax.broadcasted_iota(jnp.int32, d.shape, 1) < HALF
    sq = d * d
    s_lo = jnp.sum(jnp.where(lo, sq, 0.0), axis=-1, keepdims=True)
    s_hi = jnp.sum(jnp.where(lo, 0.0, sq), axis=-1, keepdims=True)
    ms = jnp.where(lo, s_lo, s_hi) * (1.0 / C_V_DIM)
    return d * lax.rsqrt(ms + SUBLN_EPS) * g2 * (1.0 - lam_init)


def _diff_kernel(lam_ref, g_ref, q_ref, k_ref, v_ref, o_ref, qm_ref, m_ref, acc_ref, *, lam_init):
    i = pl.program_id(0)
    t = T_C
    q = q_ref[...]
    grp = lax.broadcasted_iota(jnp.int32, (t, C_WIDTH), 1) // C_QK_DIM
    for hm in range(C_MAPS):
        qm_ref[hm] = jnp.where(grp == hm, q, jnp.zeros_like(q))
    m_ref[...] = jnp.full(m_ref.shape, -jnp.inf, F32)
    acc_ref[...] = jnp.zeros(acc_ref.shape, F32)
    lo = lax.broadcasted_iota(jnp.int32, (t, LANES), 1) < HALF

    def step(k_t, v_t, mask):
        for h in range(C_HEADS):
            p_, half = divmod(h, 2)
            v2 = v_t[:, p_ * LANES:(p_ + 1) * LANES]
            sel = lo if half == 0 else jnp.logical_not(lo)
            v_aug = jnp.where(sel, v2, jnp.ones_like(v2))
            for mp in range(2):
                hm = 2 * h + mp
                s = _nt_dot(qm_ref[hm], k_t)
                if mask is not None:
                    s = jnp.where(mask, s, -jnp.inf)
                m_old = m_ref[hm]
                m_new = jnp.maximum(m_old, jnp.max(s, axis=-1, keepdims=True))
                alpha = jnp.exp(m_old - m_new)
                p = jnp.exp(s - m_new[:, 0:1])
                acc_ref[hm] = alpha * acc_ref[hm] + jnp.dot(p.astype(BF16), v_aug,
                                                            preferred_element_type=F32)
                m_ref[hm] = m_new

    def body(j, carry):
        start = pl.multiple_of(j * t, t)
        step(k_ref[pl.ds(start, t), :], v_ref[pl.ds(start, t), :], None)
        return carry

    lax.fori_loop(0, i, body, 0)
    start = pl.multiple_of(i * t, t)
    qchunk = lax.broadcasted_iota(jnp.int32, (t, t), 0) // CHUNK
    kchunk = lax.broadcasted_iota(jnp.int32, (t, t), 1) // CHUNK
    step(k_ref[pl.ds(start, t), :], v_ref[pl.ds(start, t), :], kchunk <= qchunk)

    lam = _diff_lambda(lam_ref, lam_init)
    for p_ in range(C_WIDTH // LANES):
        d = []
        for half in range(2):
            h = 2 * p_ + half
            r = []
            for mp in range(2):
                acc = acc_ref[2 * h + mp]
                r.append(acc / pltpu.roll(acc, HALF, 1))
            d.append(r[0] - lam * r[1])
        pair = jnp.where(lo, d[0], d[1])
        o_ref[:, p_ * LANES:(p_ + 1) * LANES] = _subln(pair, g_ref[...], lam_init)


def _diff_attention(lam4, g2, qc, kc, vc, lam_init):
    s = qc.shape[0]
    t = T_C
    return pl.pallas_call(
        functools.partial(_diff_kernel, lam_init=lam_init),
        grid=(s // t,),
        in_specs=[_resident((4, C_QK_DIM)), _resident((1, LANES)),
                  pl.BlockSpec((t, C_WIDTH), lambda i: (i, 0)),
                  _resident((s, C_WIDTH)), _resident((s, C_WIDTH))],
        out_specs=pl.BlockSpec((t, C_WIDTH), lambda i: (i, 0)),
        out_shape=jax.ShapeDtypeStruct((s, C_WIDTH), F32),
        scratch_shapes=[pltpu.VMEM((C_MAPS, t, C_WIDTH), BF16),
                        pltpu.VMEM((C_MAPS, t, LANES), F32),
                        pltpu.VMEM((C_MAPS, t, LANES), F32)],
        compiler_params=_cparams(),
        name="diff_attn",
    )(lam4, g2, qc, kc, vc)


def _silu(t):
    return t * jax.nn.sigmoid(t)


def _conv_taps(u, prev0, prev1):
    row = lax.broadcasted_iota(jnp.int32, u.shape, 0)
    u1 = jnp.where(row == 0, prev1, pltpu.roll(u, 1, 0))
    u2 = jnp.where(row == 0, prev0, jnp.where(row == 1, prev1, pltpu.roll(u, 2, 0)))
    return u2, u1


def _mix_and_project(x, ag, bb, cg, oa, oc, u2, u1, u, cw, w_ref):
    bw = B_WIDTH
    conv = u2 * cw[0:1] + u1 * cw[1:2] + u * cw[2:3]
    ob = bb[:, 0:bw] * conv
    y = jnp.concatenate([oa * _silu(ag), ob * _silu(bb[:, 3 * bw:4 * bw]), oc * _silu(cg)], axis=-1)
    return x + jnp.dot(y.astype(BF16), w_ref[...], preferred_element_type=F32)


def _final_norm(x, fg):
    ms = jnp.mean(x * x, axis=-1, keepdims=True)
    return x * lax.rsqrt(ms + NORM_EPS) * fg


def _merge_kernel(x_ref, ag_ref, bb_ref, cg_ref, prev_ref, oa_ref, oc_ref, w_ref, cw_ref, fg_ref,
                  xo_ref, ul_ref, *, final):
    i = pl.program_id(0)
    bw = B_WIDTH
    bb = bb_ref[...]
    u = bb[:, bw:2 * bw] * bb[:, 2 * bw:3 * bw]
    prev = prev_ref[...]
    pu = prev[:, bw:2 * bw] * prev[:, 2 * bw:3 * bw]
    pu = jnp.where(i == 0, 0.0, pu)
    u2, u1 = _conv_taps(u, pu[6:7], pu[7:8])
    out = _mix_and_project(x_ref[...], ag_ref[...], bb, cg_ref[...], oa_ref[...], oc_ref[...],
                           u2, u1, u, cw_ref[...], w_ref)
    ul_ref[0] = u[u.shape[0] - 8:]
    xo_ref[...] = _final_norm(out, fg_ref[...]) if final else out


def _merge(x, z, oa, oc, w_b, cw, fg, ts, final):
    n, d = x.shape
    nt = n // ts
    sub = ts // 8
    return pl.pallas_call(
        functools.partial(_merge_kernel, final=final),
        grid=(nt,),
        in_specs=[pl.BlockSpec((ts, d), lambda i: (i, 0)),
                  pl.BlockSpec((ts, A_WIDTH), lambda i: (i, COL_AG // A_WIDTH)),
                  pl.BlockSpec((ts, 4 * B_WIDTH), lambda i: (i, COL_B // (4 * B_WIDTH))),
                  pl.BlockSpec((ts, C_WIDTH), lambda i: (i, COL_CG // C_WIDTH)),
                  pl.BlockSpec((8, 4 * B_WIDTH),
                               lambda i: (jnp.maximum(i * sub - 1, 0), COL_B // (4 * B_WIDTH))),
                  pl.BlockSpec((ts, A_WIDTH), lambda i: (i, 0)),
                  pl.BlockSpec((ts, C_WIDTH), lambda i: (i, 0)),
                  _resident(w_b.shape), _resident(cw.shape), _resident((1, d))],
        out_specs=(pl.BlockSpec((ts, d), lambda i: (i, 0)),
                   pl.BlockSpec((1, 8, B_WIDTH), lambda i: (i, 0, 0))),
        out_shape=(jax.ShapeDtypeStruct((n, d), F32),
                   jax.ShapeDtypeStruct((nt, 8, B_WIDTH), F32)),
        compiler_params=_cparams(),
        name="merge",
    )(x, z, z, z, z, oa, oc, w_b, cw, fg)


def _stack_heads(q, n_groups, width):
    grp = lax.broadcasted_iota(jnp.int32, q.shape, 1) // width
    return jnp.concatenate([jnp.where(grp == g, q, jnp.zeros_like(q)) for g in range(n_groups)], axis=0)


def _two_part_softmax_pv(s_c, s_n, v_c, v_n):
    m = jnp.maximum(jnp.max(s_c, axis=-1, keepdims=True), jnp.max(s_n, axis=-1, keepdims=True))
    e_c = jnp.exp(s_c - m)
    e_n = jnp.exp(s_n - m)
    l = jnp.sum(e_c, axis=-1, keepdims=True) + jnp.sum(e_n, axis=-1, keepdims=True)
    o = (jnp.dot(e_c.astype(BF16), v_c, preferred_element_type=F32)
         + jnp.dot(e_n.astype(BF16), v_n, preferred_element_type=F32))
    return o / l


def _sample_kernel(lam_ref, g_ref, bc_ref, bn_ref,
                   x_ref, qa_ref, ka_ref, va_ref, qc_ref, kc_ref, vc_ref, ag_ref, bb_ref, cg_ref,
                   cak_ref, cav_ref, cck_ref, ccv_ref, conv_ref, w_ref, cw_ref, fg_ref,
                   xo_ref, ul_ref, *, lam_init, final):
    t = qa_ref.shape[0]
    qs = _stack_heads(qa_ref[...], A_HEADS, HEAD_DIM)
    s_c = _nt_dot(qs, cak_ref[0].astype(BF16)) + bc_ref[...]
    s_n = _nt_dot(qs, ka_ref[...]) + bn_ref[...]
    of = _two_part_softmax_pv(s_c, s_n, cav_ref[0].astype(BF16), va_ref[...])
    grp = lax.broadcasted_iota(jnp.int32, (t, A_WIDTH), 1) // HEAD_DIM
    oa = jnp.zeros((t, A_WIDTH), F32)
    for h in range(A_HEADS):
        oa = jnp.where(grp == h, of[h * t:(h + 1) * t], oa)
    qs = _stack_heads(qc_ref[...], C_MAPS, C_QK_DIM)
    s_c = _nt_dot(qs, cck_ref[0].astype(BF16))
    s_n = _nt_dot(qs, kc_ref[...])
    of = _two_part_softmax_pv(s_c, s_n, ccv_ref[0].astype(BF16), vc_ref[...])
    lam = _diff_lambda(lam_ref, lam_init)
    grp = lax.broadcasted_iota(jnp.int32, (t, C_WIDTH), 1) // C_V_DIM
    d = jnp.zeros((t, C_WIDTH), F32)
    for h in range(C_HEADS):
        dh = of[2 * h * t:(2 * h + 1) * t] - lam * of[(2 * h + 1) * t:(2 * h + 2) * t]
        d = jnp.where(grp == h, dh, d)
    oc = jnp.concatenate([_subln(d[:, p * LANES:(p + 1) * LANES], g_ref[...], lam_init)
                          for p in range(C_WIDTH // LANES)], axis=-1)
    bw = B_WIDTH
    bb = bb_ref[...]
    u = bb[:, bw:2 * bw] * bb[:, 2 * bw:3 * bw]
    prev = conv_ref[0]
    u2, u1 = _conv_taps(u, prev[0:1], prev[1:2])
    out = _mix_and_project(x_ref[...], ag_ref[...], bb, cg_ref[...], oa, oc,
                           u2, u1, u, cw_ref[...], w_ref)
    ul_ref[0] = u[t - 8:]
    xo_ref[...] = _final_norm(out, fg_ref[...]) if final else out


def _sample_step(lam4, g2, bias_c, bias_n, x, z, qa, ka, va, qc, kc, vc,
                 cak, cav, cck, ccv, conv, w_b, cw, fg, t, lam_init, final):
    n, d = x.shape
    nb = n // t
    win = cak.shape[1]
    past = cck.shape[1]
    row = lambda width, col=0: pl.BlockSpec((t, width), lambda b: (b, col))
    per_b = lambda a: pl.BlockSpec((1,) + a.shape[1:], lambda b: (b,) + (0,) * (a.ndim - 1))
    return pl.pallas_call(
        functools.partial(_sample_kernel, lam_init=lam_init, final=final),
        grid=(nb,),
        in_specs=[_resident((4, C_QK_DIM)), _resident((1, LANES)),
                  _resident(bias_c.shape), _resident(bias_n.shape),
                  row(d), row(A_WIDTH), row(A_WIDTH), row(A_WIDTH),
                  row(C_WIDTH), row(C_WIDTH), row(C_WIDTH),
                  row(A_WIDTH, COL_AG // A_WIDTH), row(4 * B_WIDTH, COL_B // (4 * B_WIDTH)),
                  row(C_WIDTH, COL_CG // C_WIDTH),
                  per_b(cak), per_b(cav), per_b(cck), per_b(ccv), per_b(conv),
                  _resident(w_b.shape), _resident(cw.shape), _resident((1, d))],
        out_specs=(row(d), pl.BlockSpec((1, 8, B_WIDTH), lambda b: (b, 0, 0))),
        out_shape=(jax.ShapeDtypeStruct((n, d), F32),
                   jax.ShapeDtypeStruct((nb, 8, B_WIDTH), F32)),
        compiler_params=_cparams(),
        name="sample_step",
    )(lam4, g2, bias_c, bias_n, x, qa, ka, va, qc, kc, vc, z, z, z,
      cak, cav, cck, ccv, conv, w_b, cw, fg)


def kernel(x_prompt, x_sample, cache_a_k, cache_a_v, state_conv, cache_c_k, cache_c_v, norm_g, w_in, w_out, rel_bias, conv_w, lam_q1, lam_k1, lam_q2, lam_k2, subln_g, final_g):
    depth = norm_g.shape[0]
    batch, seq, d_model = x_prompt.shape
    nb, t, _ = x_sample.shape
    past = cache_c_k.shape[2]
    win = cache_a_k.shape[2]
    assert batch == 1 and win == A_WIN and seq % T_C == 0 and seq >= WIN_A
    assert t % 16 == 0 and t >= 8 and past % LANES == 0
    keep = min(A_WIN, seq)

    xp = x_prompt.reshape(seq, d_model)
    xs = x_sample.reshape(nb * t, d_model)
    rope_p = _rope_tables(jnp.arange(seq))
    rope_s = _rope_tables(jnp.tile(past + jnp.arange(t), nb))
    fg = final_g.reshape(1, d_model).astype(F32)
    ts_p = 512

    outs = {k: [] for k in ("pak", "pav", "pcv", "pck", "pcc", "sak", "sav", "scv", "sck", "scc")}
    for l in range(depth):
        final = l == depth - 1
        lam_init = 0.8 - 0.6 * math.exp(-0.3 * l)
        g = norm_g[l].reshape(1, d_model).astype(F32)
        w_in_b = w_in[l].astype(BF16)
        w_out_b = w_out[l].astype(BF16)
        cw = conv_w[l].astype(F32)
        lam4 = jnp.stack([lam_q1[l], lam_k1[l], lam_q2[l], lam_k2[l]]).astype(F32)
        g2 = jnp.tile(subln_g[l].astype(F32), LANES // C_V_DIM).reshape(1, LANES)
        bias = _bias_tables(rel_bias[l])

        z, qa, ka, va, qc, kc, vc, ckr = _inproj(xp, g, w_in_b, *rope_p, ts_p)
        oa = _band_attention(qa, ka, va, bias)
        oc = _diff_attention(lam4, g2, qc, kc, vc, lam_init)
        xp, ul = _merge(xp, z, oa, oc, w_out_b, cw, fg, ts_p, final)
        outs["pak"].append(z[seq - keep:, COL_AK:COL_AK + A_WIDTH].reshape(1, keep, A_HEADS, HEAD_DIM))
        outs["pav"].append(z[seq - keep:, COL_AV:COL_AV + A_WIDTH].reshape(1, keep, A_HEADS, HEAD_DIM))
        outs["pcv"].append(ul[-1, 8 - (CONV_WIDTH - 1):][None])
        outs["pck"].append(ckr.reshape(1, seq, C_HEADS, 2, C_QK_DIM))
        outs["pcc"].append(z[:, COL_CV:COL_CV + C_WIDTH].reshape(1, seq, C_HEADS, C_V_DIM))

        z, qa, ka, va, qc, kc, vc, ckr = _inproj(xs, g, w_in_b, *rope_s, nb * t)
        unmasked = bias[N_BIAS - 1, :, :t, :A_WIN + t].reshape(A_HEADS * t, A_WIN + t)
        xs, ul = _sample_step(
            lam4, g2, unmasked[:, :A_WIN], unmasked[:, A_WIN:], xs, z, qa, ka, va, qc, kc, vc,
            cache_a_k[l].reshape(nb, win, A_WIDTH), cache_a_v[l].reshape(nb, win, A_WIDTH),
            cache_c_k[l].reshape(nb, past, C_WIDTH), cache_c_v[l].reshape(nb, past, C_WIDTH),
            state_conv[l], w_out_b, cw, fg, t, lam_init, final)
        outs["sak"].append(z[:, COL_AK:COL_AK + A_WIDTH].reshape(nb, t, A_HEADS, HEAD_DIM))
        outs["sav"].append(z[:, COL_AV:COL_AV + A_WIDTH].reshape(nb, t, A_HEADS, HEAD_DIM))
        outs["scv"].append(ul[:, 8 - (CONV_WIDTH - 1):])
        outs["sck"].append(ckr.reshape(nb, t, C_HEADS, 2, C_QK_DIM))
        outs["scc"].append(z[:, COL_CV:COL_CV + C_WIDTH].reshape(nb, t, C_HEADS, C_V_DIM))

    st = lambda k: jnp.stack(outs[k])
    return (xp.reshape(batch, seq, d_model), xs.reshape(nb, t, d_model),
            st("pak"), st("pav"), st("pcv"), st("pck"), st("pcc"),
            st("sak"), st("sav"), st("scv"), st("sck"), st("scc"))
```

```python
import functools
import math

import numpy as np
import jax
import jax.numpy as jnp
from jax import lax
from jax.experimental import pallas as pl
from jax.experimental.pallas import tpu as pltpu

F32 = jnp.float32
BF16 = jnp.bfloat16

CHUNK = 64
HEAD_DIM = 64
A_HEADS = 8
A_WIDTH = A_HEADS * HEAD_DIM
A_PREV_CHUNKS = 8
A_WIN = A_PREV_CHUNKS * CHUNK
REL_CLIP = 128
B_WIDTH = 256
CONV_WIDTH = 3
C_HEADS = 4
C_QK_DIM = 32
C_V_DIM = 64
C_WIDTH = C_HEADS * C_V_DIM
C_MAPS = 2 * C_HEADS
ROPE_DIMS = 8
ROPE_THETA = 500000.0
NORM_EPS = 1e-6
SUBLN_EPS = 1e-5
D_IN_PROJ = 4096
COL_AQ, COL_AK, COL_AV, COL_AG = 0, 512, 1024, 1536
COL_B = 2048
COL_CQ, COL_CK, COL_CV, COL_CG = 3072, 3328, 3584, 3840

LANES = 128
HALF = LANES // 2
NEG = -1e30
LOG2E = math.log2(math.e)
C_QSCALE = C_QK_DIM ** -0.5 * LOG2E
VMEM_LIMIT = 48 * 1024 * 1024

PROJ_COLS = 512
TQ_A = 256
WIN_A = A_WIN + TQ_A
ROLL_A = 1024
N_BIAS = 4
T_C = 512


def _nt_dot(a, b):
    return lax.dot_general(a, b, (((1,), (1,)), ((), ())), preferred_element_type=F32)


def _cparams(n_axes=1):
    return pltpu.CompilerParams(dimension_semantics=("parallel",) * n_axes,
                                vmem_limit_bytes=VMEM_LIMIT)


def _resident(shape):
    nd = len(shape)
    return pl.BlockSpec(shape, lambda *_: (0,) * nd, pipeline_mode=pl.Buffered(1))


def _rope(x, cos, sa, sb):
    return (x * cos + pltpu.roll(x, LANES - ROPE_DIMS // 2, 1) * sa
            + pltpu.roll(x, ROPE_DIMS // 2, 1) * sb)


def _inproj_kernel(x_ref, g_ref, w_ref, cos_ref, sa_ref, sb_ref,
                   z_ref, qa_ref, ka_ref, va_ref, qc_ref, kc_ref, vc_ref, ckr_ref):
    x = x_ref[...]
    ms = jnp.mean(x * x, axis=-1, keepdims=True)
    h = (x * lax.rsqrt(ms + NORM_EPS) * g_ref[...]).astype(BF16)
    cos, sa, sb = cos_ref[...], sa_ref[...], sb_ref[...]
    for c in range(D_IN_PROJ // PROJ_COLS):
        lo = c * PROJ_COLS
        zc = jnp.dot(h, w_ref[:, lo:lo + PROJ_COLS], preferred_element_type=F32)
        z_ref[:, lo:lo + PROJ_COLS] = zc
        if lo == COL_AQ:
            qa_ref[...] = (zc * (HEAD_DIM ** -0.5)).astype(BF16)
        elif lo == COL_AK:
            ka_ref[...] = zc.astype(BF16)
        elif lo == COL_AV:
            va_ref[...] = zc.astype(BF16)
        elif lo == COL_CQ:
            for t in range(C_WIDTH // LANES):
                cq = _rope(zc[:, t * LANES:(t + 1) * LANES], cos, sa, sb)
                qc_ref[:, t * LANES:(t + 1) * LANES] = (cq * C_QSCALE).astype(BF16)
                ck = _rope(zc[:, C_WIDTH + t * LANES:C_WIDTH + (t + 1) * LANES], cos, sa, sb)
                kc_ref[:, t * LANES:(t + 1) * LANES] = ck.astype(BF16)
                ckr_ref[:, t * LANES:(t + 1) * LANES] = ck
        elif lo == COL_CV:
            vc_ref[...] = zc[:, :C_WIDTH].astype(BF16)


def _inproj(x, g, w_b, cos, sa, sb, ts):
    n = x.shape[0]
    d = x.shape[1]
    row = lambda width: pl.BlockSpec((ts, width), lambda i: (i, 0))
    out_shape = (
        jax.ShapeDtypeStruct((n, D_IN_PROJ), F32),
        jax.ShapeDtypeStruct((n, A_WIDTH), BF16),
        jax.ShapeDtypeStruct((n, A_WIDTH), BF16),
        jax.ShapeDtypeStruct((n, A_WIDTH), BF16),
        jax.ShapeDtypeStruct((n, C_WIDTH), BF16),
        jax.ShapeDtypeStruct((n, C_WIDTH), BF16),
        jax.ShapeDtypeStruct((n, C_WIDTH), BF16),
        jax.ShapeDtypeStruct((n, C_WIDTH), F32),
    )
    return pl.pallas_call(
        _inproj_kernel,
        grid=(n // ts,),
        in_specs=[row(d), _resident((1, d)), _resident((d, D_IN_PROJ)),
                  row(LANES), row(LANES), row(LANES)],
        out_specs=(row(D_IN_PROJ), row(A_WIDTH), row(A_WIDTH), row(A_WIDTH),
                   row(C_WIDTH), row(C_WIDTH), row(C_WIDTH), row(C_WIDTH)),
        out_shape=out_shape,
        compiler_params=_cparams(),
        name="inproj",
    )(x, g, w_b, cos, sa, sb)


def _rope_tables(pos):
    half = ROPE_DIMS // 2
    inv_freq = ROPE_THETA ** (-jnp.arange(half, dtype=F32) * (2.0 / ROPE_DIMS))
    ang = pos.astype(F32)[:, None] * inv_freq[None, :]
    cos, sin = jnp.cos(ang), jnp.sin(ang)
    d = np.arange(LANES) % C_QK_DIM
    j = d % half
    first, second = d < half, (d >= half) & (d < ROPE_DIMS)
    cos_t = jnp.where(d < ROPE_DIMS, cos[:, j], 1.0)
    sa_t = jnp.where(first, -sin[:, j], 0.0)
    sb_t = jnp.where(second, sin[:, j], 0.0)
    return cos_t, sa_t, sb_t


def _bias_kernel(u_ref, b_ref):
    v = pl.program_id(0)
    u = jnp.broadcast_to(u_ref[0, 0], (TQ_A, ROLL_A))
    t = pltpu.roll(u, 0, 1, stride=1, stride_axis=0)[:, ROLL_A - WIN_A:]
    qc = lax.broadcasted_iota(jnp.int32, (TQ_A, WIN_A), 0) // CHUNK
    kc = lax.broadcasted_iota(jnp.int32, (TQ_A, WIN_A), 1) // CHUNK
    top = qc + jnp.minimum(v, 2) * (TQ_A // CHUNK)
    valid = ((kc <= top) & (kc >= top - A_PREV_CHUNKS)) | (v == N_BIAS - 1)
    b_ref[0, 0] = jnp.where(valid, t, NEG)


def _bias_tables(rel_bias_l):
    offs = np.array([0, TQ_A, A_WIN, A_WIN])[:, None]
    m = np.arange(ROLL_A)[None, :]
    idx = np.clip(offs + (ROLL_A - WIN_A) - m, -REL_CLIP, REL_CLIP) + REL_CLIP
    u = jnp.transpose(rel_bias_l.astype(F32)[:, idx], (1, 0, 2))[:, :, None, :]
    return pl.pallas_call(
        _bias_kernel,
        grid=(N_BIAS, A_HEADS),
        in_specs=[pl.BlockSpec((1, 1, 1, ROLL_A), lambda v, h: (v, h, 0, 0))],
        out_specs=pl.BlockSpec((1, 1, TQ_A, WIN_A), lambda v, h: (v, h, 0, 0)),
        out_shape=jax.ShapeDtypeStruct((N_BIAS, A_HEADS, TQ_A, WIN_A), F32),
        compiler_params=_cparams(2),
        name="bias",
    )(u)


def _band_kernel(q_ref, k0_ref, k1_ref, k2_ref, v0_ref, v1_ref, v2_ref, b_ref, o_ref):
    lo = lax.broadcasted_iota(jnp.int32, (TQ_A, LANES), 1) < HALF
    k_refs = (k0_ref, k1_ref, k2_ref)
    v_refs = (v0_ref, v1_ref, v2_ref)
    for p in range(A_WIDTH // LANES):
        cols = slice(p * LANES, (p + 1) * LANES)
        q2 = q_ref[:, cols]
        vw = jnp.concatenate([r[:, cols] for r in v_refs], axis=0)
        outs = []
        for half in range(2):
            sel = lo if half == 0 else jnp.logical_not(lo)
            qm = jnp.where(sel, q2, jnp.zeros_like(q2))
            s = jnp.concatenate([_nt_dot(qm, r[:, cols]) for r in k_refs], axis=1)
            s = s + b_ref[0, 2 * p + half]
            m = jnp.max(s, axis=-1, keepdims=True)
            e = jnp.exp(s - m)
            l = jnp.sum(e, axis=-1, keepdims=True)
            o = jnp.dot(e.astype(BF16), vw, preferred_element_type=F32)
            outs.append(o / l)
        o_ref[:, cols] = jnp.where(lo, outs[0], outs[1])


def _band_attention(qa, ka, va, bias):
    s = qa.shape[0]
    nq = s // TQ_A
    first = lambda i: jnp.maximum(i - A_WIN // TQ_A, 0)
    kv = lambda j: pl.BlockSpec((TQ_A, A_WIDTH), lambda i: (first(i) + j, 0))
    return pl.pallas_call(
        _band_kernel,
        grid=(nq,),
        in_specs=[pl.BlockSpec((TQ_A, A_WIDTH), lambda i: (i, 0)),
                  kv(0), kv(1), kv(2), kv(0), kv(1), kv(2),
                  pl.BlockSpec((1, A_HEADS, TQ_A, WIN_A),
                               lambda i: (jnp.minimum(i, 2), 0, 0, 0))],
        out_specs=pl.BlockSpec((TQ_A, A_WIDTH), lambda i: (i, 0)),
        out_shape=jax.ShapeDtypeStruct((s, A_WIDTH), F32),
        compiler_params=_cparams(),
        name="band_attn",
    )(qa, ka, ka, ka, va, va, va, bias)


def _diff_lambda(lam_ref, lam_init):
    a = jnp.sum(lam_ref[0:1, :] * lam_ref[1:2, :], axis=-1, keepdims=True)
    b = jnp.sum(lam_ref[2:3, :] * lam_ref[3:4, :], axis=-1, keepdims=True)
    return jnp.exp(a) - jnp.exp(b) + lam_init


def _subln(d, g2, lam_init):
    lo = lax.broadcasted_iota(jnp.int32, d.shape, 1) < HALF
    sq = d * d
    s_lo = jnp.sum(jnp.where(lo, sq, 0.0), axis=-1, keepdims=True)
    s_hi = jnp.sum(jnp.where(lo, 0.0, sq), axis=-1, keepdims=True)
    ms = jnp.where(lo, s_lo, s_hi) * (1.0 / C_V_DIM)
    return d * lax.rsqrt(ms + SUBLN_EPS) * g2 * (1.0 - lam_init)


def _diff_kernel(lam_ref, g_ref, q_ref, k_ref, v_ref, o_ref, qm_ref, s_ref, m_ref, acc_ref, *, lam_init):
    i = pl.program_id(0)
    t = T_C
    n_grp = C_HEADS // 2
    per_grp = C_MAPS // n_grp
    q = q_ref[...]
    lane_grp = lax.broadcasted_iota(jnp.int32, (t, C_WIDTH), 1) // C_QK_DIM
    for hm in range(C_MAPS):
        g, loc = divmod(hm, per_grp)
        qm_ref[g, loc * t:(loc + 1) * t, :] = jnp.where(lane_grp == hm, q, jnp.zeros_like(q))
    m_ref[...] = jnp.full(m_ref.shape, -jnp.inf, F32)
    acc_ref[...] = jnp.zeros(acc_ref.shape, F32)
    lo = lax.broadcasted_iota(jnp.int32, (t, LANES), 1) < HALF

    def k_tile(j):
        return k_ref[pl.ds(pl.multiple_of(j * t, t), t), :]

    def v_tile(j):
        return v_ref[pl.ds(pl.multiple_of(j * t, t), t), :]

    def scores(g, k_t):
        s_ref[g] = _nt_dot(qm_ref[g], k_t)

    def softmax_pv(g, v_t, mask):
        v2 = v_t[:, g * LANES:(g + 1) * LANES]
        for half in range(2):
            h = 2 * g + half
            sel = lo if half == 0 else jnp.logical_not(lo)
            v_aug = jnp.where(sel, v2, jnp.ones_like(v2))
            ps, alphas = [], []
            for mp in range(2):
                hm = 2 * h + mp
                loc = hm - g * per_grp
                s = s_ref[g, loc * t:(loc + 1) * t, :]
                if mask is not None:
                    s = jnp.where(mask, s, -jnp.inf)
                m_old = m_ref[hm]
                m_new = jnp.maximum(m_old, jnp.max(s, axis=-1, keepdims=True))
                alphas.append(jnp.exp2(m_old - m_new))
                ps.append(jnp.exp2(s - pltpu.repeat(m_new, t // LANES, 1)).astype(BF16))
                m_ref[hm] = m_new
            pv = jnp.dot(jnp.concatenate(ps, axis=0), v_aug, preferred_element_type=F32)
            for mp in range(2):
                hm = 2 * h + mp
                acc_ref[hm] = alphas[mp] * acc_ref[hm] + pv[mp * t:(mp + 1) * t]

    scores(0, k_tile(0))

    def body(j, carry):
        v_t = v_tile(j)
        scores(1, k_tile(j))
        softmax_pv(0, v_t, None)
        scores(0, k_tile(j + 1))
        softmax_pv(1, v_t, None)
        return carry

    lax.fori_loop(0, i, body, 0)
    qchunk = lax.broadcasted_iota(jnp.int32, (t, t), 0) // CHUNK
    kchunk = lax.broadcasted_iota(jnp.int32, (t, t), 1) // CHUNK
    mask = kchunk <= qchunk
    v_t = v_tile(i)
    scores(1, k_tile(i))
    softmax_pv(0, v_t, mask)
    softmax_pv(1, v_t, mask)

    lam = _diff_lambda(lam_ref, lam_init)
    for p_ in range(C_WIDTH // LANES):
        d = []
        for half in range(2):
            h = 2 * p_ + half
            r = []
            for mp in range(2):
                acc = acc_ref[2 * h + mp]
                r.append(acc / pltpu.roll(acc, HALF, 1))
            d.append(r[0] - lam * r[1])
        pair = jnp.where(lo, d[0], d[1])
        o_ref[:, p_ * LANES:(p_ + 1) * LANES] = _subln(pair, g_ref[...], lam_init)


def _diff_attention(lam4, g2, qc, kc, vc, lam_init):
    s = qc.shape[0]
    t = T_C
    n_grp = C_HEADS // 2
    return pl.pallas_call(
        functools.partial(_diff_kernel, lam_init=lam_init),
        grid=(s // t,),
        in_specs=[_resident((4, C_QK_DIM)), _resident((1, LANES)),
                  pl.BlockSpec((t, C_WIDTH), lambda i: (i, 0)),
                  _resident((s, C_WIDTH)), _resident((s, C_WIDTH))],
        out_specs=pl.BlockSpec((t, C_WIDTH), lambda i: (i, 0)),
        out_shape=jax.ShapeDtypeStruct((s, C_WIDTH), F32),
        scratch_shapes=[pltpu.VMEM((n_grp, C_MAPS // n_grp * t, C_WIDTH), BF16),
                        pltpu.VMEM((n_grp, C_MAPS // n_grp * t, t), F32),
                        pltpu.VMEM((C_MAPS, t, LANES), F32),
                        pltpu.VMEM((C_MAPS, t, LANES), F32)],
        compiler_params=_cparams(),
        name="diff_attn",
    )(lam4, g2, qc, kc, vc)


def _silu(t):
    return t * jax.nn.sigmoid(t)


def _conv_taps(u, prev0, prev1):
    row = lax.broadcasted_iota(jnp.int32, u.shape, 0)
    u1 = jnp.where(row == 0, prev1, pltpu.roll(u, 1, 0))
    u2 = jnp.where(row == 0, prev0, jnp.where(row == 1, prev1, pltpu.roll(u, 2, 0)))
    return u2, u1


def _mix_and_project(x, ag, bb, cg, oa, oc, u2, u1, u, cw, w_ref):
    bw = B_WIDTH
    conv = u2 * cw[0:1] + u1 * cw[1:2] + u * cw[2:3]
    ob = bb[:, 0:bw] * conv
    y = jnp.concatenate([oa * _silu(ag), ob * _silu(bb[:, 3 * bw:4 * bw]), oc * _silu(cg)], axis=-1)
    return x + jnp.dot(y.astype(BF16), w_ref[...], preferred_element_type=F32)


def _final_norm(x, fg):
    ms = jnp.mean(x * x, axis=-1, keepdims=True)
    return x * lax.rsqrt(ms + NORM_EPS) * fg


def _merge_kernel(x_ref, ag_ref, bb_ref, cg_ref, prev_ref, oa_ref, oc_ref, w_ref, cw_ref, fg_ref,
                  xo_ref, ul_ref, *, final):
    i = pl.program_id(0)
    bw = B_WIDTH
    bb = bb_ref[...]
    u = bb[:, bw:2 * bw] * bb[:, 2 * bw:3 * bw]
    prev = prev_ref[...]
    pu = prev[:, bw:2 * bw] * prev[:, 2 * bw:3 * bw]
    pu = jnp.where(i == 0, 0.0, pu)
    u2, u1 = _conv_taps(u, pu[6:7], pu[7:8])
    out = _mix_and_project(x_ref[...], ag_ref[...], bb, cg_ref[...], oa_ref[...], oc_ref[...],
                           u2, u1, u, cw_ref[...], w_ref)
    ul_ref[0] = u[u.shape[0] - 8:]
    xo_ref[...] = _final_norm(out, fg_ref[...]) if final else out


def _merge(x, z, oa, oc, w_b, cw, fg, ts, final):
    n, d = x.shape
    nt = n // ts
    sub = ts // 8
    return pl.pallas_call(
        functools.partial(_merge_kernel, final=final),
        grid=(nt,),
        in_specs=[pl.BlockSpec((ts, d), lambda i: (i, 0)),
                  pl.BlockSpec((ts, A_WIDTH), lambda i: (i, COL_AG // A_WIDTH)),
                  pl.BlockSpec((ts, 4 * B_WIDTH), lambda i: (i, COL_B // (4 * B_WIDTH))),
                  pl.BlockSpec((ts, C_WIDTH), lambda i: (i, COL_CG // C_WIDTH)),
                  pl.BlockSpec((8, 4 * B_WIDTH),
                               lambda i: (jnp.maximum(i * sub - 1, 0), COL_B // (4 * B_WIDTH))),
                  pl.BlockSpec((ts, A_WIDTH), lambda i: (i, 0)),
                  pl.BlockSpec((ts, C_WIDTH), lambda i: (i, 0)),
                  _resident(w_b.shape), _resident(cw.shape), _resident((1, d))],
        out_specs=(pl.BlockSpec((ts, d), lambda i: (i, 0)),
                   pl.BlockSpec((1, 8, B_WIDTH), lambda i: (i, 0, 0))),
        out_shape=(jax.ShapeDtypeStruct((n, d), F32),
                   jax.ShapeDtypeStruct((nt, 8, B_WIDTH), F32)),
        compiler_params=_cparams(),
        name="merge",
    )(x, z, z, z, z, oa, oc, w_b, cw, fg)


def _stack_heads(q, n_groups, width):
    grp = lax.broadcasted_iota(jnp.int32, q.shape, 1) // width
    return jnp.concatenate([jnp.where(grp == g, q, jnp.zeros_like(q)) for g in range(n_groups)], axis=0)


def _two_part_softmax_pv(s_c, s_n, v_c, v_n, exp):
    m = jnp.maximum(jnp.max(s_c, axis=-1, keepdims=True), jnp.max(s_n, axis=-1, keepdims=True))
    e_c = exp(s_c - m)
    e_n = exp(s_n - m)
    l = jnp.sum(e_c, axis=-1, keepdims=True) + jnp.sum(e_n, axis=-1, keepdims=True)
    o = (jnp.dot(e_c.astype(BF16), v_c, preferred_element_type=F32)
         + jnp.dot(e_n.astype(BF16), v_n, preferred_element_type=F32))
    return o / l


def _sample_kernel(lam_ref, g_ref, bc_ref, bn_ref,
                   x_ref, qa_ref, ka_ref, va_ref, qc_ref, kc_ref, vc_ref, ag_ref, bb_ref, cg_ref,
                   cak_ref, cav_ref, cck_ref, ccv_ref, conv_ref, w_ref, cw_ref, fg_ref,
                   xo_ref, ul_ref, *, lam_init, final):
    t = qa_ref.shape[0]
    qs = _stack_heads(qa_ref[...], A_HEADS, HEAD_DIM)
    s_c = _nt_dot(qs, cak_ref[0].astype(BF16)) + bc_ref[...]
    s_n = _nt_dot(qs, ka_ref[...]) + bn_ref[...]
    of = _two_part_softmax_pv(s_c, s_n, cav_ref[0].astype(BF16), va_ref[...], jnp.exp)
    grp = lax.broadcasted_iota(jnp.int32, (t, A_WIDTH), 1) // HEAD_DIM
    oa = jnp.zeros((t, A_WIDTH), F32)
    for h in range(A_HEADS):
        oa = jnp.where(grp == h, of[h * t:(h + 1) * t], oa)
    qs = _stack_heads(qc_ref[...], C_MAPS, C_QK_DIM)
    s_c = _nt_dot(qs, cck_ref[0].astype(BF16))
    s_n = _nt_dot(qs, kc_ref[...])
    of = _two_part_softmax_pv(s_c, s_n, ccv_ref[0].astype(BF16), vc_ref[...], jnp.exp2)
    lam = _diff_lambda(lam_ref, lam_init)
    grp = lax.broadcasted_iota(jnp.int32, (t, C_WIDTH), 1) // C_V_DIM
    d = jnp.zeros((t, C_WIDTH), F32)
    for h in range(C_HEADS):
        dh = of[2 * h * t:(2 * h + 1) * t] - lam * of[(2 * h + 1) * t:(2 * h + 2) * t]
        d = jnp.where(grp == h, dh, d)
    oc = jnp.concatenate([_subln(d[:, p * LANES:(p + 1) * LANES], g_ref[...], lam_init)
                          for p in range(C_WIDTH // LANES)], axis=-1)
    bw = B_WIDTH
    bb = bb_ref[...]
    u = bb[:, bw:2 * bw] * bb[:, 2 * bw:3 * bw]
    prev = conv_ref[0]
    u2, u1 = _conv_taps(u, prev[0:1], prev[1:2])
    out = _mix_and_project(x_ref[...], ag_ref[...], bb, cg_ref[...], oa, oc,
                           u2, u1, u, cw_ref[...], w_ref)
    ul_ref[0] = u[t - 8:]
    xo_ref[...] = _final_norm(out, fg_ref[...]) if final else out


def _sample_step(lam4, g2, bias_c, bias_n, x, z, qa, ka, va, qc, kc, vc,
                 cak, cav, cck, ccv, conv, w_b, cw, fg, t, lam_init, final):
    n, d = x.shape
    nb = n // t
    win = cak.shape[1]
    past = cck.shape[1]
    row = lambda width, col=0: pl.BlockSpec((t, width), lambda b: (b, col))
    per_b = lambda a: pl.BlockSpec((1,) + a.shape[1:], lambda b: (b,) + (0,) * (a.ndim - 1))
    return pl.pallas_call(
        functools.partial(_sample_kernel, lam_init=lam_init, final=final),
        grid=(nb,),
        in_specs=[_resident((4, C_QK_DIM)), _resident((1, LANES)),
                  _resident(bias_c.shape), _resident(bias_n.shape),
                  row(d), row(A_WIDTH), row(A_WIDTH), row(A_WIDTH),
                  row(C_WIDTH), row(C_WIDTH), row(C_WIDTH),
                  row(A_WIDTH, COL_AG // A_WIDTH), row(4 * B_WIDTH, COL_B // (4 * B_WIDTH)),
                  row(C_WIDTH, COL_CG // C_WIDTH),
                  per_b(cak), per_b(cav), per_b(cck), per_b(ccv), per_b(conv),
                  _resident(w_b.shape), _resident(cw.shape), _resident((1, d))],
        out_specs=(row(d), pl.BlockSpec((1, 8, B_WIDTH), lambda b: (b, 0, 0))),
        out_shape=(jax.ShapeDtypeStruct((n, d), F32),
                   jax.ShapeDtypeStruct((nb, 8, B_WIDTH), F32)),
        compiler_params=_cparams(),
        name="sample_step",
    )(lam4, g2, bias_c, bias_n, x, qa, ka, va, qc, kc, vc, z, z, z,
      cak, cav, cck, ccv, conv, w_b, cw, fg)


def kernel(x_prompt, x_sample, cache_a_k, cache_a_v, state_conv, cache_c_k, cache_c_v, norm_g, w_in, w_out, rel_bias, conv_w, lam_q1, lam_k1, lam_q2, lam_k2, subln_g, final_g):
    depth = norm_g.shape[0]
    batch, seq, d_model = x_prompt.shape
    nb, t, _ = x_sample.shape
    past = cache_c_k.shape[2]
    win = cache_a_k.shape[2]
    assert batch == 1 and win == A_WIN and seq % T_C == 0 and seq >= WIN_A
    assert t % 16 == 0 and t >= 8 and past % LANES == 0
    keep = min(A_WIN, seq)

    xp = x_prompt.reshape(seq, d_model)
    xs = x_sample.reshape(nb * t, d_model)
    rope_p = _rope_tables(jnp.arange(seq))
    rope_s = _rope_tables(jnp.tile(past + jnp.arange(t), nb))
    fg = final_g.reshape(1, d_model).astype(F32)
    ts_p = 512

    outs = {k: [] for k in ("pak", "pav", "pcv", "pck", "pcc", "sak", "sav", "scv", "sck", "scc")}
    for l in range(depth):
        final = l == depth - 1
        lam_init = 0.8 - 0.6 * math.exp(-0.3 * l)
        g = norm_g[l].reshape(1, d_model).astype(F32)
        w_in_b = w_in[l].astype(BF16)
        w_out_b = w_out[l].astype(BF16)
        cw = conv_w[l].astype(F32)
        lam4 = jnp.stack([lam_q1[l], lam_k1[l], lam_q2[l], lam_k2[l]]).astype(F32)
        g2 = jnp.tile(subln_g[l].astype(F32), LANES // C_V_DIM).reshape(1, LANES)
        bias = _bias_tables(rel_bias[l])

        z, qa, ka, va, qc, kc, vc, ckr = _inproj(xp, g, w_in_b, *rope_p, ts_p)
        oa = _band_attention(qa, ka, va, bias)
        oc = _diff_attention(lam4, g2, qc, kc, vc, lam_init)
        xp, ul = _merge(xp, z, oa, oc, w_out_b, cw, fg, ts_p, final)
        outs["pak"].append(z[seq - keep:, COL_AK:COL_AK + A_WIDTH].reshape(1, keep, A_HEADS, HEAD_DIM))
        outs["pav"].append(z[seq - keep:, COL_AV:COL_AV + A_WIDTH].reshape(1, keep, A_HEADS, HEAD_DIM))
        outs["pcv"].append(ul[-1, 8 - (CONV_WIDTH - 1):][None])
        outs["pck"].append(ckr.reshape(1, seq, C_HEADS, 2, C_QK_DIM))
        outs["pcc"].append(z[:, COL_CV:COL_CV + C_WIDTH].reshape(1, seq, C_HEADS, C_V_DIM))

        z, qa, ka, va, qc, kc, vc, ckr = _inproj(xs, g, w_in_b, *rope_s, nb * t)
        unmasked = bias[N_BIAS - 1, :, :t, :A_WIN + t].reshape(A_HEADS * t, A_WIN + t)
        xs, ul = _sample_step(
            lam4, g2, unmasked[:, :A_WIN], unmasked[:, A_WIN:], xs, z, qa, ka, va, qc, kc, vc,
            cache_a_k[l].reshape(nb, win, A_WIDTH), cache_a_v[l].reshape(nb, win, A_WIDTH),
            cache_c_k[l].reshape(nb, past, C_WIDTH), cache_c_v[l].reshape(nb, past, C_WIDTH),
            state_conv[l], w_out_b, cw, fg, t, lam_init, final)
        outs["sak"].append(z[:, COL_AK:COL_AK + A_WIDTH].reshape(nb, t, A_HEADS, HEAD_DIM))
        outs["sav"].append(z[:, COL_AV:COL_AV + A_WIDTH].reshape(nb, t, A_HEADS, HEAD_DIM))
        outs["scv"].append(ul[:, 8 - (CONV_WIDTH - 1):])
        outs["sck"].append(ckr.reshape(nb, t, C_HEADS, 2, C_QK_DIM))
        outs["scc"].append(z[:, COL_CV:COL_CV + C_WIDTH].reshape(nb, t, C_HEADS, C_V_DIM))

    st = lambda k: jnp.stack(outs[k])
    return (xp.reshape(batch, seq, d_model), xs.reshape(nb, t, d_model),
            st("pak"), st("pav"), st("pcv"), st("pck"), st("pcc"),
            st("sak"), st("sav"), st("scv"), st("sck"), st("scc"))
```

```python
import functools
import math

import numpy as np
import jax
import jax.numpy as jnp
from jax import lax
from jax.experimental import pallas as pl
from jax.experimental.pallas import tpu as pltpu

F32 = jnp.float32
BF16 = jnp.bfloat16

CHUNK = 64
HEAD_DIM = 64
A_HEADS = 8
A_WIDTH = A_HEADS * HEAD_DIM
A_PREV_CHUNKS = 8
A_WIN = A_PREV_CHUNKS * CHUNK
REL_CLIP = 128
B_WIDTH = 256
CONV_WIDTH = 3
C_HEADS = 4
C_QK_DIM = 32
C_V_DIM = 64
C_WIDTH = C_HEADS * C_V_DIM
C_MAPS = 2 * C_HEADS
ROPE_DIMS = 8
ROPE_THETA = 500000.0
NORM_EPS = 1e-6
SUBLN_EPS = 1e-5
D_IN_PROJ = 4096
COL_AQ, COL_AK, COL_AV, COL_AG = 0, 512, 1024, 1536
COL_B = 2048
COL_CQ, COL_CK, COL_CV, COL_CG = 3072, 3328, 3584, 3840
GATE_W = A_WIDTH + 4 * B_WIDTH + C_WIDTH
G_AG, G_B, G_CG = 0, A_WIDTH, A_WIDTH + 4 * B_WIDTH

LANES = 128
HALF = LANES // 2
NEG = -1e30
LOG2E = math.log2(math.e)
A_QSCALE = HEAD_DIM ** -0.5 * LOG2E
C_QSCALE = C_QK_DIM ** -0.5 * LOG2E
VMEM_LIMIT = 48 * 1024 * 1024

PROJ_COLS = 512
TS_PROMPT = 512
TQ_A = 256
WIN_A = A_WIN + TQ_A
ROLL_A = 1024
N_BIAS = 4
T_C = 512


def _nt_dot(a, b):
    return lax.dot_general(a, b, (((1,), (1,)), ((), ())), preferred_element_type=F32)


def _cparams(n_axes=1):
    return pltpu.CompilerParams(dimension_semantics=("parallel",) * n_axes,
                                vmem_limit_bytes=VMEM_LIMIT)


def _resident(shape):
    nd = len(shape)
    return pl.BlockSpec(shape, lambda *_: (0,) * nd, pipeline_mode=pl.Buffered(1))


def _rope(x, cos, sa, sb):
    return (x * cos + pltpu.roll(x, LANES - ROPE_DIMS // 2, 1) * sa
            + pltpu.roll(x, ROPE_DIMS // 2, 1) * sb)


def _inproj_kernel(x_ref, g_ref, w_ref, cos_ref, sa_ref, sb_ref,
                   gate_ref, akv_ref, qa_ref, ka_ref, va_ref, qc_ref, kc_ref, vc_ref, ckr_ref, cvr_ref,
                   *, first_keep, kv_t):
    keep = pl.program_id(0) >= first_keep
    x = x_ref[...]
    ms = jnp.mean(x * x, axis=-1, keepdims=True)
    h = (x * lax.rsqrt(ms + NORM_EPS) * g_ref[...]).astype(BF16)
    cos, sa, sb = cos_ref[...], sa_ref[...], sb_ref[...]

    def put_f32(ref, t, val):
        if kv_t:
            ref[t * LANES:(t + 1) * LANES, :] = val.T
        else:
            ref[:, t * LANES:(t + 1) * LANES] = val

    for c in range(D_IN_PROJ // PROJ_COLS):
        lo = c * PROJ_COLS
        zc = jnp.dot(h, w_ref[:, lo:lo + PROJ_COLS], preferred_element_type=F32)
        if lo == COL_AQ:
            qa_ref[...] = (zc * A_QSCALE).astype(BF16)
        elif lo in (COL_AK, COL_AV):
            (ka_ref if lo == COL_AK else va_ref)[...] = zc.astype(BF16)

            @pl.when(keep)
            def _(zc=zc, lo=lo):
                akv_ref[:, lo - COL_AK:lo - COL_AK + A_WIDTH] = zc
        elif lo == COL_AG:
            gate_ref[:, G_AG:G_AG + A_WIDTH] = zc
        elif COL_B <= lo < COL_CQ:
            gate_ref[:, G_B + lo - COL_B:G_B + lo - COL_B + PROJ_COLS] = zc
        elif lo == COL_CQ:
            for t in range(C_WIDTH // LANES):
                cq = _rope(zc[:, t * LANES:(t + 1) * LANES], cos, sa, sb)
                qc_ref[:, t * LANES:(t + 1) * LANES] = (cq * C_QSCALE).astype(BF16)
                ck = _rope(zc[:, C_WIDTH + t * LANES:C_WIDTH + (t + 1) * LANES], cos, sa, sb)
                kc_ref[:, t * LANES:(t + 1) * LANES] = ck.astype(BF16)
                put_f32(ckr_ref, t, ck)
        elif lo == COL_CV:
            vc_ref[...] = zc[:, :C_WIDTH].astype(BF16)
            for t in range(C_WIDTH // LANES):
                put_f32(cvr_ref, t, zc[:, t * LANES:(t + 1) * LANES])
            gate_ref[:, G_CG:G_CG + C_WIDTH] = zc[:, C_WIDTH:]


def _inproj(x, g, w_b, cos, sa, sb, ts, keep, kv_t):
    n, d = x.shape
    nt = n // ts
    first_keep = nt - keep // ts
    row = lambda width: pl.BlockSpec((ts, width), lambda i: (i, 0))
    if kv_t:
        kv_shape, kv_spec = (C_WIDTH, n), pl.BlockSpec((C_WIDTH, ts), lambda i: (0, i))
    else:
        kv_shape, kv_spec = (n, C_WIDTH), row(C_WIDTH)
    out_shape = (
        jax.ShapeDtypeStruct((n, GATE_W), F32),
        jax.ShapeDtypeStruct((keep, 2 * A_WIDTH), F32),
        jax.ShapeDtypeStruct((n, A_WIDTH), BF16),
        jax.ShapeDtypeStruct((n, A_WIDTH), BF16),
        jax.ShapeDtypeStruct((n, A_WIDTH), BF16),
        jax.ShapeDtypeStruct((n, C_WIDTH), BF16),
        jax.ShapeDtypeStruct((n, C_WIDTH), BF16),
        jax.ShapeDtypeStruct((n, C_WIDTH), BF16),
        jax.ShapeDtypeStruct(kv_shape, F32),
        jax.ShapeDtypeStruct(kv_shape, F32),
    )
    return pl.pallas_call(
        functools.partial(_inproj_kernel, first_keep=first_keep, kv_t=kv_t),
        grid=(nt,),
        in_specs=[row(d), _resident((1, d)), _resident((d, D_IN_PROJ)),
                  row(LANES), row(LANES), row(LANES)],
        out_specs=(row(GATE_W),
                   pl.BlockSpec((ts, 2 * A_WIDTH), lambda i: (jnp.maximum(i - first_keep, 0), 0)),
                   row(A_WIDTH), row(A_WIDTH), row(A_WIDTH),
                   row(C_WIDTH), row(C_WIDTH), row(C_WIDTH), kv_spec, kv_spec),
        out_shape=out_shape,
        compiler_params=pltpu.CompilerParams(dimension_semantics=("arbitrary",),
                                             vmem_limit_bytes=VMEM_LIMIT),
        name="inproj",
    )(x, g, w_b, cos, sa, sb)


def _rope_tables(pos):
    half = ROPE_DIMS // 2
    inv_freq = ROPE_THETA ** (-jnp.arange(half, dtype=F32) * (2.0 / ROPE_DIMS))
    ang = pos.astype(F32)[:, None] * inv_freq[None, :]
    cos, sin = jnp.cos(ang), jnp.sin(ang)
    d = np.arange(LANES) % C_QK_DIM
    j = d % half
    first, second = d < half, (d >= half) & (d < ROPE_DIMS)
    cos_t = jnp.where(d < ROPE_DIMS, cos[:, j], 1.0)
    sa_t = jnp.where(first, -sin[:, j], 0.0)
    sb_t = jnp.where(second, sin[:, j], 0.0)
    return cos_t, sa_t, sb_t


def _bias_kernel(u_ref, b_ref):
    v = pl.program_id(0)
    u = jnp.broadcast_to(u_ref[0, 0], (TQ_A, ROLL_A))
    t = pltpu.roll(u, 0, 1, stride=1, stride_axis=0)[:, ROLL_A - WIN_A:]
    qc = lax.broadcasted_iota(jnp.int32, (TQ_A, WIN_A), 0) // CHUNK
    kc = lax.broadcasted_iota(jnp.int32, (TQ_A, WIN_A), 1) // CHUNK
    top = qc + jnp.minimum(v, 2) * (TQ_A // CHUNK)
    valid = ((kc <= top) & (kc >= top - A_PREV_CHUNKS)) | (v == N_BIAS - 1)
    b_ref[0, 0] = jnp.where(valid, t * LOG2E, NEG)


def _bias_tables(rel_bias_l):
    offs = np.array([0, TQ_A, A_WIN, A_WIN])[:, None]
    m = np.arange(ROLL_A)[None, :]
    idx = np.clip(offs + (ROLL_A - WIN_A) - m, -REL_CLIP, REL_CLIP) + REL_CLIP
    u = jnp.transpose(rel_bias_l.astype(F32)[:, idx], (1, 0, 2))[:, :, None, :]
    return pl.pallas_call(
        _bias_kernel,
        grid=(N_BIAS, A_HEADS),
        in_specs=[pl.BlockSpec((1, 1, 1, ROLL_A), lambda v, h: (v, h, 0, 0))],
        out_specs=pl.BlockSpec((1, 1, TQ_A, WIN_A), lambda v, h: (v, h, 0, 0)),
        out_shape=jax.ShapeDtypeStruct((N_BIAS, A_HEADS, TQ_A, WIN_A), F32),
        compiler_params=_cparams(2),
        name="bias",
    )(u)


def _band_kernel(q_ref, k0_ref, k1_ref, k2_ref, v0_ref, v1_ref, v2_ref, b_ref, o_ref, s_ref):
    i = pl.program_id(0)
    n_pair = A_WIDTH // LANES
    lo = lax.broadcasted_iota(jnp.int32, (TQ_A, LANES), 1) < HALF
    lo_w = lax.broadcasted_iota(jnp.int32, (WIN_A, LANES), 1) < HALF

    @pl.when(i == 0)
    def _():
        s_ref[1] = jnp.zeros(s_ref.shape[1:], F32)

    def step(cur):
        for p in range(n_pair):
            cols = slice(p * LANES, (p + 1) * LANES)
            q2 = q_ref[:, cols]
            qs = jnp.concatenate([jnp.where(lo, q2, jnp.zeros_like(q2)),
                                  jnp.where(lo, jnp.zeros_like(q2), q2)], axis=0)
            kw = jnp.concatenate([r[:, cols] for r in (k0_ref, k1_ref, k2_ref)], axis=0)
            s_ref[cur, 2 * p * TQ_A:(2 * p + 2) * TQ_A, :] = _nt_dot(qs, kw)
        for p in range(n_pair):
            cols = slice(p * LANES, (p + 1) * LANES)
            vw = jnp.concatenate([r[:, cols] for r in (v0_ref, v1_ref, v2_ref)], axis=0)
            outs = []
            for half in range(2):
                h = 2 * p + half
                sel_w = lo_w if half == 0 else jnp.logical_not(lo_w)
                v_aug = jnp.where(sel_w, vw, jnp.ones_like(vw))
                s = s_ref[1 - cur, h * TQ_A:(h + 1) * TQ_A, :] + b_ref[0, h]
                m = jnp.max(s, axis=-1, keepdims=True)
                e = jnp.exp2(s - m).astype(BF16)
                o = jnp.dot(e, v_aug, preferred_element_type=F32)
                outs.append(o / pltpu.roll(o, HALF, 1))
            o_ref[:, cols] = jnp.where(lo, outs[0], outs[1])

    parity = jnp.bitwise_and(i, 1)
    pl.when(parity == 0)(functools.partial(step, 0))
    pl.when(parity == 1)(functools.partial(step, 1))


def _band_attention(qa, ka, va, bias):
    s = qa.shape[0]
    nq = s // TQ_A
    q_tile = lambda i: jnp.minimum(i, nq - 1)
    o_tile = lambda i: jnp.maximum(i - 1, 0)
    first = lambda tile: jnp.maximum(tile - A_WIN // TQ_A, 0)
    k_spec = lambda j: pl.BlockSpec((TQ_A, A_WIDTH), lambda i: (first(q_tile(i)) + j, 0))
    v_spec = lambda j: pl.BlockSpec((TQ_A, A_WIDTH), lambda i: (first(o_tile(i)) + j, 0))
    return pl.pallas_call(
        _band_kernel,
        grid=(nq + 1,),
        in_specs=[pl.BlockSpec((TQ_A, A_WIDTH), lambda i: (q_tile(i), 0)),
                  k_spec(0), k_spec(1), k_spec(2), v_spec(0), v_spec(1), v_spec(2),
                  pl.BlockSpec((1, A_HEADS, TQ_A, WIN_A),
                               lambda i: (jnp.minimum(o_tile(i), 2), 0, 0, 0))],
        out_specs=pl.BlockSpec((TQ_A, A_WIDTH), lambda i: (o_tile(i), 0)),
        out_shape=jax.ShapeDtypeStruct((s, A_WIDTH), F32),
        scratch_shapes=[pltpu.VMEM((2, A_HEADS * TQ_A, WIN_A), F32)],
        compiler_params=pltpu.CompilerParams(dimension_semantics=("arbitrary",),
                                             vmem_limit_bytes=VMEM_LIMIT),
        name="band_attn",
    )(qa, ka, ka, ka, va, va, va, bias)


def _diff_lambda(lam_ref, lam_init):
    a = jnp.sum(lam_ref[0:1, :] * lam_ref[1:2, :], axis=-1, keepdims=True)
    b = jnp.sum(lam_ref[2:3, :] * lam_ref[3:4, :], axis=-1, keepdims=True)
    return jnp.exp(a) - jnp.exp(b) + lam_init


def _subln(d, g2, lam_init):
    lo = lax.broadcasted_iota(jnp.int32, d.shape, 1) < HALF
    sq = d * d
    s_lo = jnp.sum(jnp.where(lo, sq, 0.0), axis=-1, keepdims=True)
    s_hi = jnp.sum(jnp.where(lo, 0.0, sq), axis=-1, keepdims=True)
    ms = jnp.where(lo, s_lo, s_hi) * (1.0 / C_V_DIM)
    return d * lax.rsqrt(ms + SUBLN_EPS) * g2 * (1.0 - lam_init)


def _diff_kernel(lam_ref, g_ref, q_ref, k_ref, v_ref, o_ref, qm_ref, s_ref, m_ref, acc_ref, *, lam_init):
    i = pl.program_id(0)
    t = T_C
    n_grp = C_HEADS // 2
    per_grp = C_MAPS // n_grp
    q = q_ref[...]
    lane_grp = lax.broadcasted_iota(jnp.int32, (t, C_WIDTH), 1) // C_QK_DIM
    for hm in range(C_MAPS):
        g, loc = divmod(hm, per_grp)
        qm_ref[g, loc * t:(loc + 1) * t, :] = jnp.where(lane_grp == hm, q, jnp.zeros_like(q))
    m_ref[...] = jnp.full(m_ref.shape, -jnp.inf, F32)
    acc_ref[...] = jnp.zeros(acc_ref.shape, F32)
    lo = lax.broadcasted_iota(jnp.int32, (t, LANES), 1) < HALF

    def k_tile(j):
        return k_ref[pl.ds(pl.multiple_of(j * t, t), t), :]

    def v_tile(j):
        return v_ref[pl.ds(pl.multiple_of(j * t, t), t), :]

    def scores(g, k_t):
        s_ref[g] = _nt_dot(qm_ref[g], k_t)

    def softmax_pv(g, v_t, mask):
        v2 = v_t[:, g * LANES:(g + 1) * LANES]
        for half in range(2):
            h = 2 * g + half
            sel = lo if half == 0 else jnp.logical_not(lo)
            v_aug = jnp.where(sel, v2, jnp.ones_like(v2))
            ps, alphas = [], []
            for mp in range(2):
                hm = 2 * h + mp
                loc = hm - g * per_grp
                s = s_ref[g, loc * t:(loc + 1) * t, :]
                if mask is not None:
                    s = jnp.where(mask, s, -jnp.inf)
                m_old = m_ref[hm]
                m_new = jnp.maximum(m_old, jnp.max(s, axis=-1, keepdims=True))
                alphas.append(jnp.exp2(m_old - m_new))
                ps.append(jnp.exp2(s - pltpu.repeat(m_new, t // LANES, 1)).astype(BF16))
                m_ref[hm] = m_new
            pv = jnp.dot(jnp.concatenate(ps, axis=0), v_aug, preferred_element_type=F32)
            for mp in range(2):
                hm = 2 * h + mp
                acc_ref[hm] = alphas[mp] * acc_ref[hm] + pv[mp * t:(mp + 1) * t]

    scores(0, k_tile(0))

    def full_tile(j):
        v_t = v_tile(j)
        scores(1, k_tile(j))
        softmax_pv(0, v_t, None)
        scores(0, k_tile(j + 1))
        softmax_pv(1, v_t, None)

    odd = jnp.bitwise_and(i, 1)

    @pl.when(odd == 1)
    def _():
        full_tile(0)

    def body(jj, carry):
        j = odd + 2 * jj
        full_tile(j)
        full_tile(j + 1)
        return carry

    lax.fori_loop(0, lax.shift_right_logical(i, 1), body, 0)
    qchunk = lax.broadcasted_iota(jnp.int32, (t, t), 0) // CHUNK
    kchunk = lax.broadcasted_iota(jnp.int32, (t, t), 1) // CHUNK
    mask = kchunk <= qchunk
    v_t = v_tile(i)
    scores(1, k_tile(i))
    softmax_pv(0, v_t, mask)
    softmax_pv(1, v_t, mask)

    lam = _diff_lambda(lam_ref, lam_init)
    for p_ in range(C_WIDTH // LANES):
        d = []
        for half in range(2):
            h = 2 * p_ + half
            r = []
            for mp in range(2):
                acc = acc_ref[2 * h + mp]
                r.append(acc / pltpu.roll(acc, HALF, 1))
            d.append(r[0] - lam * r[1])
        pair = jnp.where(lo, d[0], d[1])
        o_ref[:, p_ * LANES:(p_ + 1) * LANES] = _subln(pair, g_ref[...], lam_init)


def _diff_attention(lam4, g2, qc, kc, vc, lam_init):
    s = qc.shape[0]
    t = T_C
    n_grp = C_HEADS // 2
    return pl.pallas_call(
        functools.partial(_diff_kernel, lam_init=lam_init),
        grid=(s // t,),
        in_specs=[_resident((4, C_QK_DIM)), _resident((1, LANES)),
                  pl.BlockSpec((t, C_WIDTH), lambda i: (i, 0)),
                  _resident((s, C_WIDTH)), _resident((s, C_WIDTH))],
        out_specs=pl.BlockSpec((t, C_WIDTH), lambda i: (i, 0)),
        out_shape=jax.ShapeDtypeStruct((s, C_WIDTH), F32),
        scratch_shapes=[pltpu.VMEM((n_grp, C_MAPS // n_grp * t, C_WIDTH), BF16),
                        pltpu.VMEM((n_grp, C_MAPS // n_grp * t, t), F32),
                        pltpu.VMEM((C_MAPS, t, LANES), F32),
                        pltpu.VMEM((C_MAPS, t, LANES), F32)],
        compiler_params=_cparams(),
        name="diff_attn",
    )(lam4, g2, qc, kc, vc)


def _silu(t):
    return t * jax.nn.sigmoid(t)


def _conv_input(gate):
    b = gate[:, G_B:G_B + 4 * B_WIDTH]
    return b[:, B_WIDTH:2 * B_WIDTH] * b[:, 2 * B_WIDTH:3 * B_WIDTH]


def _conv_taps(u, prev0, prev1):
    row = lax.broadcasted_iota(jnp.int32, u.shape, 0)
    u1 = jnp.where(row == 0, prev1, pltpu.roll(u, 1, 0))
    u2 = jnp.where(row == 0, prev0, jnp.where(row == 1, prev1, pltpu.roll(u, 2, 0)))
    return u2, u1


def _mix_and_project(x, gate, oa, oc, u2, u1, u, cw, w_ref):
    bw = B_WIDTH
    conv = u2 * cw[0:1] + u1 * cw[1:2] + u * cw[2:3]
    ob = gate[:, G_B:G_B + bw] * conv
    y = jnp.concatenate([oa * _silu(gate[:, G_AG:G_AG + A_WIDTH]),
                         ob * _silu(gate[:, G_B + 3 * bw:G_B + 4 * bw]),
                         oc * _silu(gate[:, G_CG:G_CG + C_WIDTH])], axis=-1)
    return x + jnp.dot(y.astype(BF16), w_ref[...], preferred_element_type=F32)


def _final_norm(x, fg):
    ms = jnp.mean(x * x, axis=-1, keepdims=True)
    return x * lax.rsqrt(ms + NORM_EPS) * fg


def _merge_kernel(x_ref, gate_ref, prev_ref, oa_ref, oc_ref, w_ref, cw_ref, fg_ref,
                  xo_ref, ul_ref, *, final):
    i = pl.program_id(0)
    gate = gate_ref[...]
    u = _conv_input(gate)
    pu = jnp.where(i == 0, 0.0, _conv_input(prev_ref[...]))
    u2, u1 = _conv_taps(u, pu[6:7], pu[7:8])
    out = _mix_and_project(x_ref[...], gate, oa_ref[...], oc_ref[...], u2, u1, u, cw_ref[...], w_ref)
    ul_ref[0] = u[u.shape[0] - 8:]
    xo_ref[...] = _final_norm(out, fg_ref[...]) if final else out


def _merge(x, gate, oa, oc, w_b, cw, fg, ts, final):
    n, d = x.shape
    nt = n // ts
    sub = ts // 8
    return pl.pallas_call(
        functools.partial(_merge_kernel, final=final),
        grid=(nt,),
        in_specs=[pl.BlockSpec((ts, d), lambda i: (i, 0)),
                  pl.BlockSpec((ts, GATE_W), lambda i: (i, 0)),
                  pl.BlockSpec((8, GATE_W), lambda i: (jnp.maximum(i * sub - 1, 0), 0)),
                  pl.BlockSpec((ts, A_WIDTH), lambda i: (i, 0)),
                  pl.BlockSpec((ts, C_WIDTH), lambda i: (i, 0)),
                  _resident(w_b.shape), _resident(cw.shape), _resident((1, d))],
        out_specs=(pl.BlockSpec((ts, d), lambda i: (i, 0)),
                   pl.BlockSpec((1, 8, B_WIDTH), lambda i: (i, 0, 0))),
        out_shape=(jax.ShapeDtypeStruct((n, d), F32),
                   jax.ShapeDtypeStruct((nt, 8, B_WIDTH), F32)),
        compiler_params=_cparams(),
        name="merge",
    )(x, gate, gate, oa, oc, w_b, cw, fg)


def _stack_heads(q, n_groups, width):
    grp = lax.broadcasted_iota(jnp.int32, q.shape, 1) // width
    return jnp.concatenate([jnp.where(grp == g, q, jnp.zeros_like(q)) for g in range(n_groups)], axis=0)


def _cached_attention(qs, kt_c, vt_c, k_n, v_n, bias_c, bias_n):
    s_c = jnp.dot(qs, kt_c, preferred_element_type=F32)
    s_n = _nt_dot(qs, k_n)
    if bias_c is not None:
        s_c, s_n = s_c + bias_c, s_n + bias_n
    m = jnp.maximum(jnp.max(s_c, axis=-1, keepdims=True), jnp.max(s_n, axis=-1, keepdims=True))
    e_c = jnp.exp2(s_c - m)
    e_n = jnp.exp2(s_n - m)
    l = jnp.sum(e_c, axis=-1, keepdims=True) + jnp.sum(e_n, axis=-1, keepdims=True)
    o = _nt_dot(e_c.astype(BF16), vt_c) + jnp.dot(e_n.astype(BF16), v_n, preferred_element_type=F32)
    return o / l


def _sample_kernel(lam_ref, g_ref, bc_ref, bn_ref,
                   x_ref, qa_ref, ka_ref, va_ref, qc_ref, kc_ref, vc_ref, gate_ref,
                   cak_ref, cav_ref, cck_ref, ccv_ref, conv_ref, w_ref, cw_ref, fg_ref,
                   xo_ref, ul_ref, *, lam_init, final):
    t = qa_ref.shape[0]
    qs = _stack_heads(qa_ref[...], A_HEADS, HEAD_DIM)
    of = _cached_attention(qs, cak_ref[0, 0].astype(BF16), cav_ref[0, 0].astype(BF16),
                           ka_ref[...], va_ref[...], bc_ref[...], bn_ref[...])
    grp = lax.broadcasted_iota(jnp.int32, (t, A_WIDTH), 1) // HEAD_DIM
    oa = jnp.zeros((t, A_WIDTH), F32)
    for h in range(A_HEADS):
        oa = jnp.where(grp == h, of[h * t:(h + 1) * t], oa)
    qs = _stack_heads(qc_ref[...], C_MAPS, C_QK_DIM)
    of = _cached_attention(qs, cck_ref[0, 0].astype(BF16), ccv_ref[0, 0].astype(BF16),
                           kc_ref[...], vc_ref[...], None, None)
    lam = _diff_lambda(lam_ref, lam_init)
    grp = lax.broadcasted_iota(jnp.int32, (t, C_WIDTH), 1) // C_V_DIM
    d = jnp.zeros((t, C_WIDTH), F32)
    for h in range(C_HEADS):
        dh = of[2 * h * t:(2 * h + 1) * t] - lam * of[(2 * h + 1) * t:(2 * h + 2) * t]
        d = jnp.where(grp == h, dh, d)
    oc = jnp.concatenate([_subln(d[:, p * LANES:(p + 1) * LANES], g_ref[...], lam_init)
                          for p in range(C_WIDTH // LANES)], axis=-1)
    gate = gate_ref[...]
    u = _conv_input(gate)
    prev = conv_ref[0]
    u2, u1 = _conv_taps(u, prev[0:1], prev[1:2])
    out = _mix_and_project(x_ref[...], gate, oa, oc, u2, u1, u, cw_ref[...], w_ref)
    ul_ref[0] = u[t - 8:]
    xo_ref[...] = _final_norm(out, fg_ref[...]) if final else out


def _sample_step(layer, lam4, g2, bias_c, bias_n, x, gate, qa, ka, va, qc, kc, vc,
                 cak_t, cav_t, cck_t, ccv_t, conv, w_b, cw, fg, t, lam_init, final):
    n, d = x.shape
    nb = n // t
    row = lambda width: pl.BlockSpec((t, width), lambda b: (b, 0))
    cache = lambda a: pl.BlockSpec((1, 1) + a.shape[2:], lambda b: (layer, b, 0, 0))
    return pl.pallas_call(
        functools.partial(_sample_kernel, lam_init=lam_init, final=final),
        grid=(nb,),
        in_specs=[_resident((4, C_QK_DIM)), _resident((1, LANES)),
                  _resident(bias_c.shape), _resident(bias_n.shape),
                  row(d), row(A_WIDTH), row(A_WIDTH), row(A_WIDTH),
                  row(C_WIDTH), row(C_WIDTH), row(C_WIDTH), row(GATE_W),
                  cache(cak_t), cache(cav_t), cache(cck_t), cache(ccv_t),
                  pl.BlockSpec((1,) + conv.shape[1:], lambda b: (b, 0, 0)),
                  _resident(w_b.shape), _resident(cw.shape), _resident((1, d))],
        out_specs=(row(d), pl.BlockSpec((1, 8, B_WIDTH), lambda b: (b, 0, 0))),
        out_shape=(jax.ShapeDtypeStruct((n, d), F32),
                   jax.ShapeDtypeStruct((nb, 8, B_WIDTH), F32)),
        compiler_params=_cparams(),
        name="sample_step",
    )(lam4, g2, bias_c, bias_n, x, qa, ka, va, qc, kc, vc, gate,
      cak_t, cav_t, cck_t, ccv_t, conv, w_b, cw, fg)


def _feature_major(cache):
    nd = cache.ndim
    c = jnp.transpose(cache, (0, 1) + tuple(range(3, nd)) + (2,))
    return c.reshape(c.shape[0], c.shape[1], -1, c.shape[-1])


def kernel(x_prompt, x_sample, cache_a_k, cache_a_v, state_conv, cache_c_k, cache_c_v, norm_g, w_in, w_out, rel_bias, conv_w, lam_q1, lam_k1, lam_q2, lam_k2, subln_g, final_g):
    depth = norm_g.shape[0]
    batch, seq, d_model = x_prompt.shape
    nb, t, _ = x_sample.shape
    past = cache_c_k.shape[2]
    win = cache_a_k.shape[2]
    ts = TS_PROMPT
    assert batch == 1 and win == A_WIN and seq % T_C == 0 and seq % ts == 0 and seq >= WIN_A
    assert t % 16 == 0 and past % LANES == 0
    keep = min(A_WIN, seq)
    assert keep % ts == 0

    xp = x_prompt.reshape(seq, d_model)
    xs = x_sample.reshape(nb * t, d_model)
    rope_p = _rope_tables(jnp.arange(seq))
    rope_s = _rope_tables(jnp.tile(past + jnp.arange(t), nb))
    fg = final_g.reshape(1, d_model).astype(F32)
    cak_t, cav_t = _feature_major(cache_a_k), _feature_major(cache_a_v)
    cck_t, ccv_t = _feature_major(cache_c_k), _feature_major(cache_c_v)
    w_in_b = w_in.astype(BF16)
    w_out_b = w_out.astype(BF16)

    outs = {k: [] for k in ("pak", "pav", "pcv", "pck", "pcc", "sak", "sav", "scv", "sck", "scc")}
    for l in range(depth):
        final = l == depth - 1
        lam_init = 0.8 - 0.6 * math.exp(-0.3 * l)
        g = norm_g[l].reshape(1, d_model).astype(F32)
        cw = conv_w[l].astype(F32)
        lam4 = jnp.stack([lam_q1[l], lam_k1[l], lam_q2[l], lam_k2[l]]).astype(F32)
        g2 = jnp.tile(subln_g[l].astype(F32), LANES // C_V_DIM).reshape(1, LANES)
        bias = _bias_tables(rel_bias[l])

        gate, akv, qa, ka, va, qc, kc, vc, ckr_t, cvr_t = _inproj(
            xp, g, w_in_b[l], *rope_p, ts, keep, True)
        oa = _band_attention(qa, ka, va, bias)
        oc = _diff_attention(lam4, g2, qc, kc, vc, lam_init)
        xp, ul = _merge(xp, gate, oa, oc, w_out_b[l], cw, fg, ts, final)
        outs["pak"].append(akv[:, :A_WIDTH].reshape(1, keep, A_HEADS, HEAD_DIM))
        outs["pav"].append(akv[:, A_WIDTH:].reshape(1, keep, A_HEADS, HEAD_DIM))
        outs["pcv"].append(ul[-1, 8 - (CONV_WIDTH - 1):][None])
        outs["pck"].append(jnp.transpose(ckr_t.reshape(C_HEADS, 2, C_QK_DIM, seq), (3, 0, 1, 2))[None])
        outs["pcc"].append(jnp.transpose(cvr_t.reshape(C_HEADS, C_V_DIM, seq), (2, 0, 1))[None])

        gate, akv, qa, ka, va, qc, kc, vc, ckr, cvr = _inproj(
            xs, g, w_in_b[l], *rope_s, nb * t, nb * t, False)
        unmasked = bias[N_BIAS - 1, :, :t, :A_WIN + t].reshape(A_HEADS * t, A_WIN + t)
        xs, ul = _sample_step(
            l, lam4, g2, unmasked[:, :A_WIN], unmasked[:, A_WIN:], xs, gate, qa, ka, va, qc, kc, vc,
            cak_t, cav_t, cck_t, ccv_t, state_conv[l], w_out_b[l], cw, fg, t, lam_init, final)
        outs["sak"].append(akv[:, :A_WIDTH].reshape(nb, t, A_HEADS, HEAD_DIM))
        outs["sav"].append(akv[:, A_WIDTH:].reshape(nb, t, A_HEADS, HEAD_DIM))
        outs["scv"].append(ul[:, 8 - (CONV_WIDTH - 1):])
        outs["sck"].append(ckr.reshape(nb, t, C_HEADS, 2, C_QK_DIM))
        outs["scc"].append(cvr.reshape(nb, t, C_HEADS, C_V_DIM))

    st = lambda k: jnp.stack(outs[k])
    return (xp.reshape(batch, seq, d_model), xs.reshape(nb, t, d_model),
            st("pak"), st("pav"), st("pcv"), st("pck"), st("pcc"),
            st("sak"), st("sav"), st("scv"), st("sck"), st("scc"))
```

```python
import functools
import math

import numpy as np
import jax
import jax.numpy as jnp
from jax import lax
from jax.experimental import pallas as pl
from jax.experimental.pallas import tpu as pltpu

F32 = jnp.float32
BF16 = jnp.bfloat16

CHUNK = 64
HEAD_DIM = 64
A_HEADS = 8
A_WIDTH = A_HEADS * HEAD_DIM
A_PREV_CHUNKS = 8
A_WIN = A_PREV_CHUNKS * CHUNK
REL_CLIP = 128
B_WIDTH = 256
CONV_WIDTH = 3
C_HEADS = 4
C_QK_DIM = 32
C_V_DIM = 64
C_WIDTH = C_HEADS * C_V_DIM
C_MAPS = 2 * C_HEADS
ROPE_DIMS = 8
ROPE_THETA = 500000.0
NORM_EPS = 1e-6
SUBLN_EPS = 1e-5
D_IN_PROJ = 4096
COL_AQ, COL_AK, COL_AV, COL_AG = 0, 512, 1024, 1536
COL_B = 2048
COL_CQ, COL_CK, COL_CV, COL_CG = 3072, 3328, 3584, 3840
GATE_W = A_WIDTH + 4 * B_WIDTH + C_WIDTH
G_AG, G_B, G_CG = 0, A_WIDTH, A_WIDTH + 4 * B_WIDTH

LANES = 128
HALF = LANES // 2
NEG = -1e30
LOG2E = math.log2(math.e)
A_QSCALE = HEAD_DIM ** -0.5 * LOG2E
C_QSCALE = C_QK_DIM ** -0.5 * LOG2E
VMEM_LIMIT = 48 * 1024 * 1024

PROJ_COLS = 512
TS_PROMPT = 512
TQ_A = 256
WIN_A = A_WIN + TQ_A
ROLL_A = 1024
N_BIAS = 4
T_C = 512


def _nt_dot(a, b):
    return lax.dot_general(a, b, (((1,), (1,)), ((), ())), preferred_element_type=F32)


def _cparams(n_axes=1):
    return pltpu.CompilerParams(dimension_semantics=("parallel",) * n_axes,
                                vmem_limit_bytes=VMEM_LIMIT)


def _resident(shape):
    nd = len(shape)
    return pl.BlockSpec(shape, lambda *_: (0,) * nd, pipeline_mode=pl.Buffered(1))


def _rope(x, cos, sa, sb):
    return (x * cos + pltpu.roll(x, LANES - ROPE_DIMS // 2, 1) * sa
            + pltpu.roll(x, ROPE_DIMS // 2, 1) * sb)


def _inproj_kernel(*refs, prompt):
    x_ref, g_ref, w_ref, cos_ref, sa_ref, sb_ref = refs[:6]
    if prompt:
        (cw_ref, acg_ref, akv_ref, qa_ref, ka_ref, va_ref, qc_ref, kc_ref, vc_ref, ckr_ref, cvr_ref,
         yb_ref, tail_ref, carry_ref) = refs[6:]
    else:
        gate_ref, akv_ref, qa_ref, ka_ref, va_ref, qc_ref, kc_ref, vc_ref, ckr_ref, cvr_ref = refs[6:]

    if prompt:
        @pl.when(pl.program_id(0) == 0)
        def _():
            carry_ref[...] = jnp.zeros(carry_ref.shape, F32)

    x = x_ref[...]
    ms = jnp.mean(x * x, axis=-1, keepdims=True)
    h = (x * lax.rsqrt(ms + NORM_EPS) * g_ref[...]).astype(BF16)
    cos, sa, sb = cos_ref[...], sa_ref[...], sb_ref[...]

    def proj(lo, width):
        return jnp.dot(h, w_ref[:, lo:lo + width], preferred_element_type=F32)

    def put_f32(ref, t, val):
        if prompt:
            ref[t * LANES:(t + 1) * LANES, :] = val.T
        else:
            ref[:, t * LANES:(t + 1) * LANES] = val

    zc = proj(COL_CQ, 2 * C_WIDTH)
    for t in range(C_WIDTH // LANES):
        cq = _rope(zc[:, t * LANES:(t + 1) * LANES], cos, sa, sb)
        qc_ref[:, t * LANES:(t + 1) * LANES] = (cq * C_QSCALE).astype(BF16)
        ck = _rope(zc[:, C_WIDTH + t * LANES:C_WIDTH + (t + 1) * LANES], cos, sa, sb)
        kc_ref[:, t * LANES:(t + 1) * LANES] = ck.astype(BF16)
        put_f32(ckr_ref, t, ck)
    zc = proj(COL_CV, 2 * C_WIDTH)
    vc_ref[...] = zc[:, :C_WIDTH].astype(BF16)
    for t in range(C_WIDTH // LANES):
        put_f32(cvr_ref, t, zc[:, t * LANES:(t + 1) * LANES])
    if prompt:
        acg_ref[:, A_WIDTH:] = zc[:, C_WIDTH:]
    else:
        gate_ref[:, G_CG:G_CG + C_WIDTH] = zc[:, C_WIDTH:]

    zb = proj(COL_B, 4 * B_WIDTH)
    if prompt:
        bw = B_WIDTH
        u = zb[:, bw:2 * bw] * zb[:, 2 * bw:3 * bw]
        prev = carry_ref[...]
        u2, u1 = _conv_taps(u, prev[6:7], prev[7:8])
        cw = cw_ref[...]
        conv = u2 * cw[0:1] + u1 * cw[1:2] + u * cw[2:3]
        yb_ref[...] = (zb[:, :bw] * conv * _silu(zb[:, 3 * bw:])).astype(BF16)
        carry_ref[...] = u[u.shape[0] - 8:]
        tail_ref[...] = u[u.shape[0] - 8:]
    else:
        gate_ref[:, G_B:G_B + 4 * B_WIDTH] = zb

    zc = proj(COL_AK, A_WIDTH)
    ka_ref[...] = zc.astype(BF16)
    akv_ref[:, :A_WIDTH] = zc
    zc = proj(COL_AV, A_WIDTH)
    va_ref[...] = zc.astype(BF16)
    akv_ref[:, A_WIDTH:] = zc
    zc = proj(COL_AG, A_WIDTH)
    if prompt:
        acg_ref[:, :A_WIDTH] = zc
    else:
        gate_ref[:, G_AG:G_AG + A_WIDTH] = zc
    qa_ref[...] = (proj(COL_AQ, A_WIDTH) * A_QSCALE).astype(BF16)


def _inproj(x, g, w_b, cos, sa, sb, ts, keep, cw=None):
    prompt = cw is not None
    n, d = x.shape
    nt = n // ts
    first_keep = nt - keep // ts
    row = lambda width: pl.BlockSpec((ts, width), lambda i: (i, 0))
    sds = jax.ShapeDtypeStruct
    if prompt:
        kv_shape, kv_spec = (C_WIDTH, n), pl.BlockSpec((C_WIDTH, ts), lambda i: (0, i))
        gate_w = A_WIDTH + C_WIDTH
    else:
        kv_shape, kv_spec = (n, C_WIDTH), row(C_WIDTH)
        gate_w = GATE_W
    akv_spec = pl.BlockSpec((ts, 2 * A_WIDTH), lambda i: (jnp.maximum(i - first_keep, 0), 0))
    out_shape = [sds((n, gate_w), F32), sds((keep, 2 * A_WIDTH), F32),
                 sds((n, A_WIDTH), BF16), sds((n, A_WIDTH), BF16), sds((n, A_WIDTH), BF16),
                 sds((n, C_WIDTH), BF16), sds((n, C_WIDTH), BF16), sds((n, C_WIDTH), BF16),
                 sds(kv_shape, F32), sds(kv_shape, F32)]
    out_specs = [row(gate_w), akv_spec, row(A_WIDTH), row(A_WIDTH), row(A_WIDTH),
                 row(C_WIDTH), row(C_WIDTH), row(C_WIDTH), kv_spec, kv_spec]
    in_specs = [row(d), _resident((1, d)), _resident((d, D_IN_PROJ)), row(LANES), row(LANES), row(LANES)]
    args = [x, g, w_b, cos, sa, sb]
    scratch = []
    if prompt:
        in_specs.append(_resident(cw.shape))
        args.append(cw)
        out_shape += [sds((n, B_WIDTH), BF16), sds((8, B_WIDTH), F32)]
        out_specs += [row(B_WIDTH), pl.BlockSpec((8, B_WIDTH), lambda i: (0, 0))]
        scratch = [pltpu.VMEM((8, B_WIDTH), F32)]
    return pl.pallas_call(
        functools.partial(_inproj_kernel, prompt=prompt),
        grid=(nt,),
        in_specs=in_specs,
        out_specs=tuple(out_specs),
        out_shape=tuple(out_shape),
        scratch_shapes=scratch,
        compiler_params=pltpu.CompilerParams(dimension_semantics=("arbitrary",),
                                             vmem_limit_bytes=VMEM_LIMIT),
        name="inproj",
    )(*args)


def _rope_tables(pos):
    half = ROPE_DIMS // 2
    inv_freq = ROPE_THETA ** (-jnp.arange(half, dtype=F32) * (2.0 / ROPE_DIMS))
    ang = pos.astype(F32)[:, None] * inv_freq[None, :]
    cos, sin = jnp.cos(ang), jnp.sin(ang)
    d = np.arange(LANES) % C_QK_DIM
    j = d % half
    first, second = d < half, (d >= half) & (d < ROPE_DIMS)
    cos_t = jnp.where(d < ROPE_DIMS, cos[:, j], 1.0)
    sa_t = jnp.where(first, -sin[:, j], 0.0)
    sb_t = jnp.where(second, sin[:, j], 0.0)
    return cos_t, sa_t, sb_t


def _bias_kernel(u_ref, b_ref):
    v = pl.program_id(0)
    u = jnp.broadcast_to(u_ref[0, 0], (TQ_A, ROLL_A))
    t = pltpu.roll(u, 0, 1, stride=1, stride_axis=0)[:, ROLL_A - WIN_A:]
    qc = lax.broadcasted_iota(jnp.int32, (TQ_A, WIN_A), 0) // CHUNK
    kc = lax.broadcasted_iota(jnp.int32, (TQ_A, WIN_A), 1) // CHUNK
    top = qc + jnp.minimum(v, 2) * (TQ_A // CHUNK)
    valid = ((kc <= top) & (kc >= top - A_PREV_CHUNKS)) | (v == N_BIAS - 1)
    b_ref[0, 0] = jnp.where(valid, t * LOG2E, NEG)


def _bias_tables(rel_bias_l):
    offs = np.array([0, TQ_A, A_WIN, A_WIN])[:, None]
    m = np.arange(ROLL_A)[None, :]
    idx = np.clip(offs + (ROLL_A - WIN_A) - m, -REL_CLIP, REL_CLIP) + REL_CLIP
    u = jnp.transpose(rel_bias_l.astype(F32)[:, idx], (1, 0, 2))[:, :, None, :]
    return pl.pallas_call(
        _bias_kernel,
        grid=(N_BIAS, A_HEADS),
        in_specs=[pl.BlockSpec((1, 1, 1, ROLL_A), lambda v, h: (v, h, 0, 0))],
        out_specs=pl.BlockSpec((1, 1, TQ_A, WIN_A), lambda v, h: (v, h, 0, 0)),
        out_shape=jax.ShapeDtypeStruct((N_BIAS, A_HEADS, TQ_A, WIN_A), F32),
        compiler_params=_cparams(2),
        name="bias",
    )(u)


def _band_kernel(q_ref, k0_ref, k1_ref, k2_ref, v0_ref, v1_ref, v2_ref, b_ref, ag_ref, o_ref, s_ref):
    i = pl.program_id(0)
    n_pair = A_WIDTH // LANES
    lo = lax.broadcasted_iota(jnp.int32, (TQ_A, LANES), 1) < HALF
    lo_w = lax.broadcasted_iota(jnp.int32, (WIN_A, LANES), 1) < HALF

    @pl.when(i == 0)
    def _():
        s_ref[1] = jnp.zeros(s_ref.shape[1:], F32)

    def step(cur):
        for p in range(n_pair):
            cols = slice(p * LANES, (p + 1) * LANES)
            q2 = q_ref[:, cols]
            qs = jnp.concatenate([jnp.where(lo, q2, jnp.zeros_like(q2)),
                                  jnp.where(lo, jnp.zeros_like(q2), q2)], axis=0)
            kw = jnp.concatenate([r[:, cols] for r in (k0_ref, k1_ref, k2_ref)], axis=0)
            s_ref[cur, 2 * p * TQ_A:(2 * p + 2) * TQ_A, :] = _nt_dot(qs, kw)
        for p in range(n_pair):
            cols = slice(p * LANES, (p + 1) * LANES)
            vw = jnp.concatenate([r[:, cols] for r in (v0_ref, v1_ref, v2_ref)], axis=0)
            outs = []
            for half in range(2):
                h = 2 * p + half
                sel_w = lo_w if half == 0 else jnp.logical_not(lo_w)
                v_aug = jnp.where(sel_w, vw, jnp.ones_like(vw))
                s = s_ref[1 - cur, h * TQ_A:(h + 1) * TQ_A, :] + b_ref[0, h]
                m = jnp.max(s, axis=-1, keepdims=True)
                e = jnp.exp2(s - m).astype(BF16)
                o = jnp.dot(e, v_aug, preferred_element_type=F32)
                outs.append(o / pltpu.roll(o, HALF, 1))
            gated = jnp.where(lo, outs[0], outs[1]) * _silu(ag_ref[:, cols])
            o_ref[:, cols] = gated.astype(BF16)

    parity = jnp.bitwise_and(i, 1)
    pl.when(parity == 0)(functools.partial(step, 0))
    pl.when(parity == 1)(functools.partial(step, 1))


def _band_attention(qa, ka, va, bias, acg):
    s = qa.shape[0]
    nq = s // TQ_A
    q_tile = lambda i: jnp.minimum(i, nq - 1)
    o_tile = lambda i: jnp.maximum(i - 1, 0)
    first = lambda tile: jnp.maximum(tile - A_WIN // TQ_A, 0)
    k_spec = lambda j: pl.BlockSpec((TQ_A, A_WIDTH), lambda i: (first(q_tile(i)) + j, 0))
    v_spec = lambda j: pl.BlockSpec((TQ_A, A_WIDTH), lambda i: (first(o_tile(i)) + j, 0))
    return pl.pallas_call(
        _band_kernel,
        grid=(nq + 1,),
        in_specs=[pl.BlockSpec((TQ_A, A_WIDTH), lambda i: (q_tile(i), 0)),
                  k_spec(0), k_spec(1), k_spec(2), v_spec(0), v_spec(1), v_spec(2),
                  pl.BlockSpec((1, A_HEADS, TQ_A, WIN_A),
                               lambda i: (jnp.minimum(o_tile(i), 2), 0, 0, 0)),
                  pl.BlockSpec((TQ_A, A_WIDTH), lambda i: (o_tile(i), 0))],
        out_specs=pl.BlockSpec((TQ_A, A_WIDTH), lambda i: (o_tile(i), 0)),
        out_shape=jax.ShapeDtypeStruct((s, A_WIDTH), BF16),
        scratch_shapes=[pltpu.VMEM((2, A_HEADS * TQ_A, WIN_A), F32)],
        compiler_params=pltpu.CompilerParams(dimension_semantics=("arbitrary",),
                                             vmem_limit_bytes=VMEM_LIMIT),
        name="band_attn",
    )(qa, ka, ka, ka, va, va, va, bias, acg)


def _diff_lambda(lam_ref, lam_init):
    a = jnp.sum(lam_ref[0:1, :] * lam_ref[1:2, :], axis=-1, keepdims=True)
    b = jnp.sum(lam_ref[2:3, :] * lam_ref[3:4, :], axis=-1, keepdims=True)
    return jnp.exp(a) - jnp.exp(b) + lam_init


def _subln(d, g2, lam_init):
    lo = lax.broadcasted_iota(jnp.int32, d.shape, 1) < HALF
    sq = d * d
    s_lo = jnp.sum(jnp.where(lo, sq, 0.0), axis=-1, keepdims=True)
    s_hi = jnp.sum(jnp.where(lo, 0.0, sq), axis=-1, keepdims=True)
    ms = jnp.where(lo, s_lo, s_hi) * (1.0 / C_V_DIM)
    return d * lax.rsqrt(ms + SUBLN_EPS) * g2 * (1.0 - lam_init)


def _diff_kernel(lam_ref, g_ref, q_ref, k_ref, v_ref, cg_ref, o_ref, qm_ref, s_ref, m_ref, acc_ref,
                 *, lam_init):
    i = pl.program_id(0)
    t = T_C
    n_grp = C_HEADS // 2
    per_grp = C_MAPS // n_grp
    q = q_ref[...]
    lane_grp = lax.broadcasted_iota(jnp.int32, (t, C_WIDTH), 1) // C_QK_DIM
    for hm in range(C_MAPS):
        g, loc = divmod(hm, per_grp)
        qm_ref[g, loc * t:(loc + 1) * t, :] = jnp.where(lane_grp == hm, q, jnp.zeros_like(q))
    m_ref[...] = jnp.full(m_ref.shape, -jnp.inf, F32)
    acc_ref[...] = jnp.zeros(acc_ref.shape, F32)
    lo = lax.broadcasted_iota(jnp.int32, (t, LANES), 1) < HALF

    def k_tile(j):
        return k_ref[pl.ds(pl.multiple_of(j * t, t), t), :]

    def v_tile(j):
        return v_ref[pl.ds(pl.multiple_of(j * t, t), t), :]

    def scores(g, k_t):
        s_ref[g] = _nt_dot(qm_ref[g], k_t)

    def softmax_pv(g, v_t, mask):
        v2 = v_t[:, g * LANES:(g + 1) * LANES]
        for half in range(2):
            h = 2 * g + half
            sel = lo if half == 0 else jnp.logical_not(lo)
            v_aug = jnp.where(sel, v2, jnp.ones_like(v2))
            ps, alphas = [], []
            for mp in range(2):
                hm = 2 * h + mp
                loc = hm - g * per_grp
                s = s_ref[g, loc * t:(loc + 1) * t, :]
                if mask is not None:
                    s = jnp.where(mask, s, -jnp.inf)
                m_old = m_ref[hm]
                m_new = jnp.maximum(m_old, jnp.max(s, axis=-1, keepdims=True))
                alphas.append(jnp.exp2(m_old - m_new))
                ps.append(jnp.exp2(s - jnp.tile(m_new, (1, t // LANES))).astype(BF16))
                m_ref[hm] = m_new
            pv = jnp.dot(jnp.concatenate(ps, axis=0), v_aug, preferred_element_type=F32)
            for mp in range(2):
                hm = 2 * h + mp
                acc_ref[hm] = alphas[mp] * acc_ref[hm] + pv[mp * t:(mp + 1) * t]

    scores(0, k_tile(0))

    def full_tile(j):
        v_t = v_tile(j)
        scores(1, k_tile(j))
        softmax_pv(0, v_t, None)
        scores(0, k_tile(j + 1))
        softmax_pv(1, v_t, None)

    odd = jnp.bitwise_and(i, 1)

    @pl.when(odd == 1)
    def _():
        full_tile(0)

    def body(jj, carry):
        j = odd + 2 * jj
        full_tile(j)
        full_tile(j + 1)
        return carry

    lax.fori_loop(0, lax.shift_right_logical(i, 1), body, 0)
    qchunk = lax.broadcasted_iota(jnp.int32, (t, t), 0) // CHUNK
    kchunk = lax.broadcasted_iota(jnp.int32, (t, t), 1) // CHUNK
    mask = kchunk <= qchunk
    v_t = v_tile(i)
    scores(1, k_tile(i))
    softmax_pv(0, v_t, mask)
    softmax_pv(1, v_t, mask)

    lam = _diff_lambda(lam_ref, lam_init)
    for p_ in range(C_WIDTH // LANES):
        d = []
        for half in range(2):
            h = 2 * p_ + half
            r = []
            for mp in range(2):
                acc = acc_ref[2 * h + mp]
                r.append(acc / pltpu.roll(acc, HALF, 1))
            d.append(r[0] - lam * r[1])
        pair = jnp.where(lo, d[0], d[1])
        tile = slice(p_ * LANES, (p_ + 1) * LANES)
        o_ref[:, tile] = (_subln(pair, g_ref[...], lam_init) * _silu(cg_ref[:, tile])).astype(BF16)


def _diff_attention(lam4, g2, qc, kc, vc, acg, lam_init):
    s = qc.shape[0]
    t = T_C
    n_grp = C_HEADS // 2
    return pl.pallas_call(
        functools.partial(_diff_kernel, lam_init=lam_init),
        grid=(s // t,),
        in_specs=[_resident((4, C_QK_DIM)), _resident((1, LANES)),
                  pl.BlockSpec((t, C_WIDTH), lambda i: (i, 0)),
                  _resident((s, C_WIDTH)), _resident((s, C_WIDTH)),
                  pl.BlockSpec((t, C_WIDTH), lambda i: (i, A_WIDTH // C_WIDTH))],
        out_specs=pl.BlockSpec((t, C_WIDTH), lambda i: (i, 0)),
        out_shape=jax.ShapeDtypeStruct((s, C_WIDTH), BF16),
        scratch_shapes=[pltpu.VMEM((n_grp, C_MAPS // n_grp * t, C_WIDTH), BF16),
                        pltpu.VMEM((n_grp, C_MAPS // n_grp * t, t), F32),
                        pltpu.VMEM((C_MAPS, t, LANES), F32),
                        pltpu.VMEM((C_MAPS, t, LANES), F32)],
        compiler_params=_cparams(),
        name="diff_attn",
    )(lam4, g2, qc, kc, vc, acg)


def _silu(t):
    return t * jax.nn.sigmoid(t)


def _conv_input(gate):
    b = gate[:, G_B:G_B + 4 * B_WIDTH]
    return b[:, B_WIDTH:2 * B_WIDTH] * b[:, 2 * B_WIDTH:3 * B_WIDTH]


def _conv_taps(u, prev0, prev1):
    row = lax.broadcasted_iota(jnp.int32, u.shape, 0)
    u1 = jnp.where(row == 0, prev1, pltpu.roll(u, 1, 0))
    u2 = jnp.where(row == 0, prev0, jnp.where(row == 1, prev1, pltpu.roll(u, 2, 0)))
    return u2, u1


def _mix_and_project(x, gate, oa, oc, u2, u1, u, cw, w_ref):
    bw = B_WIDTH
    conv = u2 * cw[0:1] + u1 * cw[1:2] + u * cw[2:3]
    ob = gate[:, G_B:G_B + bw] * conv
    y = jnp.concatenate([oa * _silu(gate[:, G_AG:G_AG + A_WIDTH]),
                         ob * _silu(gate[:, G_B + 3 * bw:G_B + 4 * bw]),
                         oc * _silu(gate[:, G_CG:G_CG + C_WIDTH])], axis=-1)
    return x + jnp.dot(y.astype(BF16), w_ref[...], preferred_element_type=F32)


def _final_norm(x, fg):
    ms = jnp.mean(x * x, axis=-1, keepdims=True)
    return x * lax.rsqrt(ms + NORM_EPS) * fg


def _merge_kernel(x_ref, ya_ref, yb_ref, yc_ref, w_ref, fg_ref, xo_ref, *, final):
    y = jnp.concatenate([ya_ref[...], yb_ref[...], yc_ref[...]], axis=-1)
    out = x_ref[...] + jnp.dot(y, w_ref[...], preferred_element_type=F32)
    xo_ref[...] = _final_norm(out, fg_ref[...]) if final else out


def _merge(x, ya, yb, yc, w_b, fg, ts, final):
    n, d = x.shape
    row = lambda width: pl.BlockSpec((ts, width), lambda i: (i, 0))
    return pl.pallas_call(
        functools.partial(_merge_kernel, final=final),
        grid=(n // ts,),
        in_specs=[row(d), row(A_WIDTH), row(B_WIDTH), row(C_WIDTH),
                  _resident(w_b.shape), _resident((1, d))],
        out_specs=row(d),
        out_shape=jax.ShapeDtypeStruct((n, d), F32),
        compiler_params=_cparams(),
        name="merge",
    )(x, ya, yb, yc, w_b, fg)


def _stack_heads(q, n_groups, width):
    grp = lax.broadcasted_iota(jnp.int32, q.shape, 1) // width
    return jnp.concatenate([jnp.where(grp == g, q, jnp.zeros_like(q)) for g in range(n_groups)], axis=0)


def _cached_attention(qs, kt_c, vt_c, k_n, v_n, bias_c, bias_n):
    s_c = jnp.dot(qs, kt_c, preferred_element_type=F32)
    s_n = _nt_dot(qs, k_n)
    if bias_c is not None:
        s_c, s_n = s_c + bias_c, s_n + bias_n
    m = jnp.maximum(jnp.max(s_c, axis=-1, keepdims=True), jnp.max(s_n, axis=-1, keepdims=True))
    e_c = jnp.exp2(s_c - m)
    e_n = jnp.exp2(s_n - m)
    l = jnp.sum(e_c, axis=-1, keepdims=True) + jnp.sum(e_n, axis=-1, keepdims=True)
    o = _nt_dot(e_c.astype(BF16), vt_c) + jnp.dot(e_n.astype(BF16), v_n, preferred_element_type=F32)
    return o / l


def _sample_kernel(lam_ref, g_ref, bc_ref, bn_ref,
                   x_ref, qa_ref, ka_ref, va_ref, qc_ref, kc_ref, vc_ref, gate_ref,
                   cak_ref, cav_ref, cck_ref, ccv_ref, conv_ref, w_ref, cw_ref, fg_ref,
                   xo_ref, ul_ref, *, lam_init, final):
    t = qa_ref.shape[0]
    qs = _stack_heads(qa_ref[...], A_HEADS, HEAD_DIM)
    of = _cached_attention(qs, cak_ref[0, 0].astype(BF16), cav_ref[0, 0].astype(BF16),
                           ka_ref[...], va_ref[...], bc_ref[...], bn_ref[...])
    grp = lax.broadcasted_iota(jnp.int32, (t, A_WIDTH), 1) // HEAD_DIM
    oa = jnp.zeros((t, A_WIDTH), F32)
    for h in range(A_HEADS):
        oa = jnp.where(grp == h, of[h * t:(h + 1) * t], oa)
    qs = _stack_heads(qc_ref[...], C_MAPS, C_QK_DIM)
    of = _cached_attention(qs, cck_ref[0, 0].astype(BF16), ccv_ref[0, 0].astype(BF16),
                           kc_ref[...], vc_ref[...], None, None)
    lam = _diff_lambda(lam_ref, lam_init)
    grp = lax.broadcasted_iota(jnp.int32, (t, C_WIDTH), 1) // C_V_DIM
    d = jnp.zeros((t, C_WIDTH), F32)
    for h in range(C_HEADS):
        dh = of[2 * h * t:(2 * h + 1) * t] - lam * of[(2 * h + 1) * t:(2 * h + 2) * t]
        d = jnp.where(grp == h, dh, d)
    oc = jnp.concatenate([_subln(d[:, p * LANES:(p + 1) * LANES], g_ref[...], lam_init)
                          for p in range(C_WIDTH // LANES)], axis=-1)
    gate = gate_ref[...]
    u = _conv_input(gate)
    prev = conv_ref[0]
    u2, u1 = _conv_taps(u, prev[0:1], prev[1:2])
    out = _mix_and_project(x_ref[...], gate, oa, oc, u2, u1, u, cw_ref[...], w_ref)
    ul_ref[0] = u[t - 8:]
    xo_ref[...] = _final_norm(out, fg_ref[...]) if final else out


def _sample_step(layer, lam4, g2, bias_c, bias_n, x, gate, qa, ka, va, qc, kc, vc,
                 cak_t, cav_t, cck_t, ccv_t, conv, w_b, cw, fg, t, lam_init, final):
    n, d = x.shape
    nb = n // t
    row = lambda width: pl.BlockSpec((t, width), lambda b: (b, 0))
    cache = lambda a: pl.BlockSpec((1, 1) + a.shape[2:], lambda b: (layer, b, 0, 0))
    return pl.pallas_call(
        functools.partial(_sample_kernel, lam_init=lam_init, final=final),
        grid=(nb,),
        in_specs=[_resident((4, C_QK_DIM)), _resident((1, LANES)),
                  _resident(bias_c.shape), _resident(bias_n.shape),
                  row(d), row(A_WIDTH), row(A_WIDTH), row(A_WIDTH),
                  row(C_WIDTH), row(C_WIDTH), row(C_WIDTH), row(GATE_W),
                  cache(cak_t), cache(cav_t), cache(cck_t), cache(ccv_t),
                  pl.BlockSpec((1,) + conv.shape[1:], lambda b: (b, 0, 0)),
                  _resident(w_b.shape), _resident(cw.shape), _resident((1, d))],
        out_specs=(row(d), pl.BlockSpec((1, 8, B_WIDTH), lambda b: (b, 0, 0))),
        out_shape=(jax.ShapeDtypeStruct((n, d), F32),
                   jax.ShapeDtypeStruct((nb, 8, B_WIDTH), F32)),
        compiler_params=_cparams(),
        name="sample_step",
    )(lam4, g2, bias_c, bias_n, x, qa, ka, va, qc, kc, vc, gate,
      cak_t, cav_t, cck_t, ccv_t, conv, w_b, cw, fg)


def _feature_major(cache):
    nd = cache.ndim
    c = jnp.transpose(cache, (0, 1) + tuple(range(3, nd)) + (2,))
    return c.reshape(c.shape[0], c.shape[1], -1, c.shape[-1])


def kernel(x_prompt, x_sample, cache_a_k, cache_a_v, state_conv, cache_c_k, cache_c_v, norm_g, w_in, w_out, rel_bias, conv_w, lam_q1, lam_k1, lam_q2, lam_k2, subln_g, final_g):
    depth = norm_g.shape[0]
    batch, seq, d_model = x_prompt.shape
    nb, t, _ = x_sample.shape
    past = cache_c_k.shape[2]
    win = cache_a_k.shape[2]
    ts = TS_PROMPT
    assert batch == 1 and win == A_WIN and seq % T_C == 0 and seq % ts == 0 and seq >= WIN_A
    assert t % 16 == 0 and past % LANES == 0
    keep = min(A_WIN, seq)
    assert keep % ts == 0

    xp = x_prompt.reshape(seq, d_model)
    xs = x_sample.reshape(nb * t, d_model)
    rope_p = _rope_tables(jnp.arange(seq))
    rope_s = _rope_tables(jnp.tile(past + jnp.arange(t), nb))
    fg = final_g.reshape(1, d_model).astype(F32)
    cak_t, cav_t = _feature_major(cache_a_k), _feature_major(cache_a_v)
    cck_t, ccv_t = _feature_major(cache_c_k), _feature_major(cache_c_v)

    outs = {k: [] for k in ("pak", "pav", "pcv", "pck", "pcc", "sak", "sav", "scv", "sck", "scc")}
    for l in range(depth):
        final = l == depth - 1
        lam_init = 0.8 - 0.6 * math.exp(-0.3 * l)
        g = norm_g[l].reshape(1, d_model).astype(F32)
        cw = conv_w[l].astype(F32)
        lam4 = jnp.stack([lam_q1[l], lam_k1[l], lam_q2[l], lam_k2[l]]).astype(F32)
        g2 = jnp.tile(subln_g[l].astype(F32), LANES // C_V_DIM).reshape(1, LANES)
        bias = _bias_tables(rel_bias[l])
        w_in_b = w_in[l].astype(BF16)
        w_out_b = w_out[l].astype(BF16)

        acg, akv, qa, ka, va, qc, kc, vc, ckr_t, cvr_t, yb, tail = _inproj(
            xp, g, w_in_b, *rope_p, ts, keep, cw)
        ya = _band_attention(qa, ka, va, bias, acg)
        yc = _diff_attention(lam4, g2, qc, kc, vc, acg, lam_init)
        xp = _merge(xp, ya, yb, yc, w_out_b, fg, ts, final)
        outs["pak"].append(akv[:, :A_WIDTH].reshape(1, keep, A_HEADS, HEAD_DIM))
        outs["pav"].append(akv[:, A_WIDTH:].reshape(1, keep, A_HEADS, HEAD_DIM))
        outs["pcv"].append(tail[8 - (CONV_WIDTH - 1):][None])
        outs["pck"].append(jnp.transpose(ckr_t.reshape(C_HEADS, 2, C_QK_DIM, seq), (3, 0, 1, 2))[None])
        outs["pcc"].append(jnp.transpose(cvr_t.reshape(C_HEADS, C_V_DIM, seq), (2, 0, 1))[None])

        gate, akv, qa, ka, va, qc, kc, vc, ckr, cvr = _inproj(
            xs, g, w_in_b, *rope_s, nb * t, nb * t)
        unmasked = bias[N_BIAS - 1, :, :t, :A_WIN + t].reshape(A_HEADS * t, A_WIN + t)
        xs, ul = _sample_step(
            l, lam4, g2, unmasked[:, :A_WIN], unmasked[:, A_WIN:], xs, gate, qa, ka, va, qc, kc, vc,
            cak_t, cav_t, cck_t, ccv_t, state_conv[l], w_out_b, cw, fg, t, lam_init, final)
        outs["sak"].append(akv[:, :A_WIDTH].reshape(nb, t, A_HEADS, HEAD_DIM))
        outs["sav"].append(akv[:, A_WIDTH:].reshape(nb, t, A_HEADS, HEAD_DIM))
        outs["scv"].append(ul[:, 8 - (CONV_WIDTH - 1):])
        outs["sck"].append(ckr.reshape(nb, t, C_HEADS, 2, C_QK_DIM))
        outs["scc"].append(cvr.reshape(nb, t, C_HEADS, C_V_DIM))

    st = lambda k: jnp.stack(outs[k])
    return (xp.reshape(batch, seq, d_model), xs.reshape(nb, t, d_model),
            st("pak"), st("pav"), st("pcv"), st("pck"), st("pcc"),
            st("sak"), st("sav"), st("scv"), st("sck"), st("scc"))
```

```python
import functools
import math

import numpy as np
import jax
import jax.numpy as jnp
from jax import lax
from jax.experimental import pallas as pl
from jax.experimental.pallas import tpu as pltpu

F32 = jnp.float32
BF16 = jnp.bfloat16

CHUNK = 64
HEAD_DIM = 64
A_HEADS = 8
A_WIDTH = A_HEADS * HEAD_DIM
A_PREV_CHUNKS = 8
A_WIN = A_PREV_CHUNKS * CHUNK
REL_CLIP = 128
B_WIDTH = 256
CONV_WIDTH = 3
C_HEADS = 4
C_QK_DIM = 32
C_V_DIM = 64
C_WIDTH = C_HEADS * C_V_DIM
C_MAPS = 2 * C_HEADS
ROPE_DIMS = 8
ROPE_THETA = 500000.0
NORM_EPS = 1e-6
SUBLN_EPS = 1e-5
D_IN_PROJ = 4096
COL_AQ, COL_AK, COL_AV, COL_AG = 0, 512, 1024, 1536
COL_B = 2048
COL_CQ, COL_CK, COL_CV, COL_CG = 3072, 3328, 3584, 3840
GATE_W = A_WIDTH + 4 * B_WIDTH + C_WIDTH
G_AG, G_B, G_CG = 0, A_WIDTH, A_WIDTH + 4 * B_WIDTH

LANES = 128
HALF = LANES // 2
NEG = -1e30
LOG2E = math.log2(math.e)
A_QSCALE = HEAD_DIM ** -0.5 * LOG2E
C_QSCALE = C_QK_DIM ** -0.5 * LOG2E
VMEM_LIMIT = 48 * 1024 * 1024

PROJ_COLS = 512
TS_PROMPT = 512
TS_MERGE = 1024
TQ_A = 256
WIN_A = A_WIN + TQ_A
ROLL_A = 1024
N_BIAS = 4
T_C = 512


def _nt_dot(a, b):
    return lax.dot_general(a, b, (((1,), (1,)), ((), ())), preferred_element_type=F32)


def _cparams(n_axes=1):
    return pltpu.CompilerParams(dimension_semantics=("parallel",) * n_axes,
                                vmem_limit_bytes=VMEM_LIMIT)


def _resident(shape):
    nd = len(shape)
    return pl.BlockSpec(shape, lambda *_: (0,) * nd, pipeline_mode=pl.Buffered(1))


def _layer_weights(w_all, layer):
    return pl.BlockSpec((1,) + w_all.shape[1:], lambda *_: (layer, 0, 0), pipeline_mode=pl.Buffered(1))


def _rope(x, cos, sa, sb):
    return (x * cos + pltpu.roll(x, LANES - ROPE_DIMS // 2, 1) * sa
            + pltpu.roll(x, ROPE_DIMS // 2, 1) * sb)


def _rope_coefficients(base, rt_ref):
    cb, sb_ = base[0:1], base[1:2]
    ct, st = rt_ref[0], rt_ref[1]
    cos_p = cb * ct - sb_ * st
    sin_p = sb_ * ct + cb * st
    d = lax.broadcasted_iota(jnp.int32, cos_p.shape, 1) % C_QK_DIM
    half = ROPE_DIMS // 2
    cos = jnp.where(d < ROPE_DIMS, cos_p, 1.0)
    sa = jnp.where(d < half, -sin_p, 0.0)
    sb = jnp.where((d >= half) & (d < ROPE_DIMS), sin_p, 0.0)
    return cos, sa, sb


def _inproj_kernel(*refs, prompt):
    x_ref, g_ref, w_ref, rb_ref, rt_ref = refs[:5]
    if prompt:
        (cw_ref, acg_ref, akv_ref, qa_ref, ka_ref, va_ref, qc_ref, kc_ref, vc_ref, ckr_ref, cvr_ref,
         yb_ref, tail_ref, carry_ref) = refs[5:]
    else:
        gate_ref, akv_ref, qa_ref, ka_ref, va_ref, qc_ref, kc_ref, vc_ref, ckr_ref, cvr_ref = refs[5:]

    if prompt:
        @pl.when(pl.program_id(0) == 0)
        def _():
            carry_ref[...] = jnp.zeros(carry_ref.shape, F32)

    x = x_ref[...]
    ms = jnp.mean(x * x, axis=-1, keepdims=True)
    h = (x * lax.rsqrt(ms + NORM_EPS) * g_ref[...]).astype(BF16)
    cos, sa, sb = _rope_coefficients(rb_ref[0], rt_ref)

    def proj(lo, width):
        return jnp.dot(h, w_ref[0, :, lo:lo + width], preferred_element_type=F32)

    def put_f32(ref, t, val):
        if prompt:
            ref[t * LANES:(t + 1) * LANES, :] = val.T
        else:
            ref[:, t * LANES:(t + 1) * LANES] = val

    zc = proj(COL_CQ, 2 * C_WIDTH)
    for t in range(C_WIDTH // LANES):
        cq = _rope(zc[:, t * LANES:(t + 1) * LANES], cos, sa, sb)
        qc_ref[:, t * LANES:(t + 1) * LANES] = (cq * C_QSCALE).astype(BF16)
        ck = _rope(zc[:, C_WIDTH + t * LANES:C_WIDTH + (t + 1) * LANES], cos, sa, sb)
        kc_ref[:, t * LANES:(t + 1) * LANES] = ck.astype(BF16)
        put_f32(ckr_ref, t, ck)
    zc = proj(COL_CV, 2 * C_WIDTH)
    vc_ref[...] = zc[:, :C_WIDTH].astype(BF16)
    for t in range(C_WIDTH // LANES):
        put_f32(cvr_ref, t, zc[:, t * LANES:(t + 1) * LANES])
    if prompt:
        acg_ref[:, A_WIDTH:] = zc[:, C_WIDTH:]
    else:
        gate_ref[:, G_CG:G_CG + C_WIDTH] = zc[:, C_WIDTH:]

    zb = proj(COL_B, 4 * B_WIDTH)
    if prompt:
        bw = B_WIDTH
        u = zb[:, bw:2 * bw] * zb[:, 2 * bw:3 * bw]
        prev = carry_ref[...]
        u2, u1 = _conv_taps(u, prev[6:7], prev[7:8])
        cw = cw_ref[...]
        conv = u2 * cw[0:1] + u1 * cw[1:2] + u * cw[2:3]
        yb_ref[...] = (zb[:, :bw] * conv * _silu(zb[:, 3 * bw:])).astype(BF16)
        carry_ref[...] = u[u.shape[0] - 8:]
        tail_ref[...] = u[u.shape[0] - 8:]
    else:
        gate_ref[:, G_B:G_B + 4 * B_WIDTH] = zb

    zc = proj(COL_AK, A_WIDTH)
    ka_ref[...] = zc.astype(BF16)
    akv_ref[:, :A_WIDTH] = zc
    zc = proj(COL_AV, A_WIDTH)
    va_ref[...] = zc.astype(BF16)
    akv_ref[:, A_WIDTH:] = zc
    zc = proj(COL_AG, A_WIDTH)
    if prompt:
        acg_ref[:, :A_WIDTH] = zc
    else:
        gate_ref[:, G_AG:G_AG + A_WIDTH] = zc
    qa_ref[...] = (proj(COL_AQ, A_WIDTH) * A_QSCALE).astype(BF16)


def _inproj(x, g, w_all, layer, rope_base, rope_off, ts, keep, cw=None):
    prompt = cw is not None
    n, d = x.shape
    nt = n // ts
    first_keep = nt - keep // ts
    row = lambda width: pl.BlockSpec((ts, width), lambda i: (i, 0))
    sds = jax.ShapeDtypeStruct
    if prompt:
        kv_shape, kv_spec = (C_WIDTH, n), pl.BlockSpec((C_WIDTH, ts), lambda i: (0, i))
        gate_w = A_WIDTH + C_WIDTH
    else:
        kv_shape, kv_spec = (n, C_WIDTH), row(C_WIDTH)
        gate_w = GATE_W
    akv_spec = pl.BlockSpec((ts, 2 * A_WIDTH), lambda i: (jnp.maximum(i - first_keep, 0), 0))
    out_shape = [sds((n, gate_w), F32), sds((keep, 2 * A_WIDTH), F32),
                 sds((n, A_WIDTH), BF16), sds((n, A_WIDTH), BF16), sds((n, A_WIDTH), BF16),
                 sds((n, C_WIDTH), BF16), sds((n, C_WIDTH), BF16), sds((n, C_WIDTH), BF16),
                 sds(kv_shape, F32), sds(kv_shape, F32)]
    out_specs = [row(gate_w), akv_spec, row(A_WIDTH), row(A_WIDTH), row(A_WIDTH),
                 row(C_WIDTH), row(C_WIDTH), row(C_WIDTH), kv_spec, kv_spec]
    in_specs = [row(d), _resident((1, d)),
                _layer_weights(w_all, layer),
                pl.BlockSpec((1, 2, LANES), lambda i: (i, 0, 0)), _resident(rope_off.shape)]
    args = [x, g, w_all, rope_base, rope_off]
    scratch = []
    if prompt:
        in_specs.append(_resident(cw.shape))
        args.append(cw)
        out_shape += [sds((n, B_WIDTH), BF16), sds((8, B_WIDTH), F32)]
        out_specs += [row(B_WIDTH), pl.BlockSpec((8, B_WIDTH), lambda i: (0, 0))]
        scratch = [pltpu.VMEM((8, B_WIDTH), F32)]
    return pl.pallas_call(
        functools.partial(_inproj_kernel, prompt=prompt),
        grid=(nt,),
        in_specs=in_specs,
        out_specs=tuple(out_specs),
        out_shape=tuple(out_shape),
        scratch_shapes=scratch,
        compiler_params=pltpu.CompilerParams(dimension_semantics=("arbitrary",),
                                             vmem_limit_bytes=VMEM_LIMIT),
        name="inproj",
    )(*args)


def _rope_angle_tables(tile_base, row_off):
    half = ROPE_DIMS // 2
    inv_freq = ROPE_THETA ** (-jnp.arange(half, dtype=F32) * (2.0 / ROPE_DIMS))
    lane_freq = inv_freq[(np.arange(LANES) % C_QK_DIM) % half][None, :]
    a = tile_base.astype(F32)[:, None] * lane_freq
    b = row_off.astype(F32)[:, None] * lane_freq
    return jnp.stack([jnp.cos(a), jnp.sin(a)], axis=1), jnp.stack([jnp.cos(b), jnp.sin(b)], axis=0)


def _bias_kernel(u_ref, b_ref):
    v = pl.program_id(0)
    qc = lax.broadcasted_iota(jnp.int32, (TQ_A, WIN_A), 0) // CHUNK
    kc = lax.broadcasted_iota(jnp.int32, (TQ_A, WIN_A), 1) // CHUNK
    top = qc + jnp.minimum(v, 2) * (TQ_A // CHUNK)
    valid = ((kc <= top) & (kc >= top - A_PREV_CHUNKS)) | (v == N_BIAS - 1)
    for h in range(A_HEADS):
        u = jnp.broadcast_to(u_ref[0, h], (TQ_A, ROLL_A))
        t = pltpu.roll(u, 0, 1, stride=1, stride_axis=0)[:, ROLL_A - WIN_A:]
        b_ref[0, h] = jnp.where(valid, t * LOG2E, NEG)


def _bias_tables(rel_bias_l):
    offs = np.array([0, TQ_A, A_WIN, A_WIN])[:, None]
    m = np.arange(ROLL_A)[None, :]
    idx = np.clip(offs + (ROLL_A - WIN_A) - m, -REL_CLIP, REL_CLIP) + REL_CLIP
    u = jnp.transpose(rel_bias_l.astype(F32)[:, idx], (1, 0, 2))[:, :, None, :]
    return pl.pallas_call(
        _bias_kernel,
        grid=(N_BIAS,),
        in_specs=[pl.BlockSpec((1, A_HEADS, 1, ROLL_A), lambda v: (v, 0, 0, 0))],
        out_specs=pl.BlockSpec((1, A_HEADS, TQ_A, WIN_A), lambda v: (v, 0, 0, 0)),
        out_shape=jax.ShapeDtypeStruct((N_BIAS, A_HEADS, TQ_A, WIN_A), F32),
        compiler_params=_cparams(),
        name="bias",
    )(u)


def _band_kernel(q_ref, k0_ref, k1_ref, k2_ref, v0_ref, v1_ref, v2_ref, b_ref, ag_ref, o_ref, s_ref):
    i = pl.program_id(0)
    n_pair = A_WIDTH // LANES
    lo = lax.broadcasted_iota(jnp.int32, (TQ_A, LANES), 1) < HALF
    lo_w = lax.broadcasted_iota(jnp.int32, (WIN_A, LANES), 1) < HALF

    @pl.when(i == 0)
    def _():
        s_ref[1] = jnp.zeros(s_ref.shape[1:], F32)

    def step(cur):
        for p in range(n_pair):
            cols = slice(p * LANES, (p + 1) * LANES)
            q2 = q_ref[:, cols]
            qs = jnp.concatenate([jnp.where(lo, q2, jnp.zeros_like(q2)),
                                  jnp.where(lo, jnp.zeros_like(q2), q2)], axis=0)
            kw = jnp.concatenate([r[:, cols] for r in (k0_ref, k1_ref, k2_ref)], axis=0)
            s_ref[cur, 2 * p * TQ_A:(2 * p + 2) * TQ_A, :] = _nt_dot(qs, kw)
        for p in range(n_pair):
            cols = slice(p * LANES, (p + 1) * LANES)
            vw = jnp.concatenate([r[:, cols] for r in (v0_ref, v1_ref, v2_ref)], axis=0)
            outs = []
            for half in range(2):
                h = 2 * p + half
                sel_w = lo_w if half == 0 else jnp.logical_not(lo_w)
                v_aug = jnp.where(sel_w, vw, jnp.ones_like(vw))
                s = s_ref[1 - cur, h * TQ_A:(h + 1) * TQ_A, :] + b_ref[0, h]
                m = jnp.max(s, axis=-1, keepdims=True)
                e = jnp.exp2(s - m).astype(BF16)
                o = jnp.dot(e, v_aug, preferred_element_type=F32)
                outs.append(o / pltpu.roll(o, HALF, 1))
            gated = jnp.where(lo, outs[0], outs[1]) * _silu(ag_ref[:, cols])
            o_ref[:, cols] = gated.astype(BF16)

    parity = jnp.bitwise_and(i, 1)
    pl.when(parity == 0)(functools.partial(step, 0))
    pl.when(parity == 1)(functools.partial(step, 1))


def _band_attention(qa, ka, va, bias, acg):
    s = qa.shape[0]
    nq = s // TQ_A
    q_tile = lambda i: jnp.minimum(i, nq - 1)
    o_tile = lambda i: jnp.maximum(i - 1, 0)
    first = lambda tile: jnp.maximum(tile - A_WIN // TQ_A, 0)
    k_spec = lambda j: pl.BlockSpec((TQ_A, A_WIDTH), lambda i: (first(q_tile(i)) + j, 0))
    v_spec = lambda j: pl.BlockSpec((TQ_A, A_WIDTH), lambda i: (first(o_tile(i)) + j, 0))
    return pl.pallas_call(
        _band_kernel,
        grid=(nq + 1,),
        in_specs=[pl.BlockSpec((TQ_A, A_WIDTH), lambda i: (q_tile(i), 0)),
                  k_spec(0), k_spec(1), k_spec(2), v_spec(0), v_spec(1), v_spec(2),
                  pl.BlockSpec((1, A_HEADS, TQ_A, WIN_A),
                               lambda i: (jnp.minimum(o_tile(i), 2), 0, 0, 0)),
                  pl.BlockSpec((TQ_A, A_WIDTH), lambda i: (o_tile(i), 0))],
        out_specs=pl.BlockSpec((TQ_A, A_WIDTH), lambda i: (o_tile(i), 0)),
        out_shape=jax.ShapeDtypeStruct((s, A_WIDTH), BF16),
        scratch_shapes=[pltpu.VMEM((2, A_HEADS * TQ_A, WIN_A), F32)],
        compiler_params=pltpu.CompilerParams(dimension_semantics=("arbitrary",),
                                             vmem_limit_bytes=VMEM_LIMIT),
        name="band_attn",
    )(qa, ka, ka, ka, va, va, va, bias, acg)


def _diff_lambda(lam_ref, lam_init):
    a = jnp.sum(lam_ref[0:1, :] * lam_ref[1:2, :], axis=-1, keepdims=True)
    b = jnp.sum(lam_ref[2:3, :] * lam_ref[3:4, :], axis=-1, keepdims=True)
    return jnp.exp(a) - jnp.exp(b) + lam_init


def _subln(d, g2, lam_init):
    lo = lax.broadcasted_iota(jnp.int32, d.shape, 1) < HALF
    sq = d * d
    s_lo = jnp.sum(jnp.where(lo, sq, 0.0), axis=-1, keepdims=True)
    s_hi = jnp.sum(jnp.where(lo, 0.0, sq), axis=-1, keepdims=True)
    ms = jnp.where(lo, s_lo, s_hi) * (1.0 / C_V_DIM)
    return d * lax.rsqrt(ms + SUBLN_EPS) * g2 * (1.0 - lam_init)


def _diff_kernel(lam_ref, g_ref, q_ref, k_ref, v_ref, cg_ref, o_ref, qm_ref, s_ref, m_ref, acc_ref,
                 *, lam_init):
    i = pl.program_id(0)
    t = T_C
    n_grp = C_HEADS // 2
    per_grp = C_MAPS // n_grp
    q = q_ref[...]
    lane_grp = lax.broadcasted_iota(jnp.int32, (t, C_WIDTH), 1) // C_QK_DIM
    for hm in range(C_MAPS):
        g, loc = divmod(hm, per_grp)
        qm_ref[g, loc * t:(loc + 1) * t, :] = jnp.where(lane_grp == hm, q, jnp.zeros_like(q))
    m_ref[...] = jnp.full(m_ref.shape, -jnp.inf, F32)
    acc_ref[...] = jnp.zeros(acc_ref.shape, F32)
    lo = lax.broadcasted_iota(jnp.int32, (t, LANES), 1) < HALF

    def k_tile(j):
        return k_ref[pl.ds(pl.multiple_of(j * t, t), t), :]

    def v_tile(j):
        return v_ref[pl.ds(pl.multiple_of(j * t, t), t), :]

    def scores(g, k_t):
        s_ref[g] = _nt_dot(qm_ref[g], k_t)

    def softmax_pv(g, v_t, mask):
        v2 = v_t[:, g * LANES:(g + 1) * LANES]
        for half in range(2):
            h = 2 * g + half
            sel = lo if half == 0 else jnp.logical_not(lo)
            v_aug = jnp.where(sel, v2, jnp.ones_like(v2))
            ps, alphas = [], []
            for mp in range(2):
                hm = 2 * h + mp
                loc = hm - g * per_grp
                s = s_ref[g, loc * t:(loc + 1) * t, :]
                if mask is not None:
                    s = jnp.where(mask, s, -jnp.inf)
                m_old = m_ref[hm]
                m_new = jnp.maximum(m_old, jnp.max(s, axis=-1, keepdims=True))
                alphas.append(jnp.exp2(m_old - m_new))
                ps.append(jnp.exp2(s - jnp.tile(m_new, (1, t // LANES))).astype(BF16))
                m_ref[hm] = m_new
            pv = jnp.dot(jnp.concatenate(ps, axis=0), v_aug, preferred_element_type=F32)
            for mp in range(2):
                hm = 2 * h + mp
                acc_ref[hm] = alphas[mp] * acc_ref[hm] + pv[mp * t:(mp + 1) * t]

    scores(0, k_tile(0))

    def full_tile(j):
        v_t = v_tile(j)
        scores(1, k_tile(j))
        softmax_pv(0, v_t, None)
        scores(0, k_tile(j + 1))
        softmax_pv(1, v_t, None)

    odd = jnp.bitwise_and(i, 1)

    @pl.when(odd == 1)
    def _():
        full_tile(0)

    def body(jj, carry):
        j = odd + 2 * jj
        full_tile(j)
        full_tile(j + 1)
        return carry

    lax.fori_loop(0, lax.shift_right_logical(i, 1), body, 0)
    qchunk = lax.broadcasted_iota(jnp.int32, (t, t), 0) // CHUNK
    kchunk = lax.broadcasted_iota(jnp.int32, (t, t), 1) // CHUNK
    mask = kchunk <= qchunk
    v_t = v_tile(i)
    scores(1, k_tile(i))
    softmax_pv(0, v_t, mask)
    softmax_pv(1, v_t, mask)

    lam = _diff_lambda(lam_ref, lam_init)
    for p_ in range(C_WIDTH // LANES):
        d = []
        for half in range(2):
            h = 2 * p_ + half
            r = []
            for mp in range(2):
                acc = acc_ref[2 * h + mp]
                r.append(acc / pltpu.roll(acc, HALF, 1))
            d.append(r[0] - lam * r[1])
        pair = jnp.where(lo, d[0], d[1])
        tile = slice(p_ * LANES, (p_ + 1) * LANES)
        o_ref[:, tile] = (_subln(pair, g_ref[...], lam_init) * _silu(cg_ref[:, tile])).astype(BF16)


def _diff_attention(lam4, g2, qc, kc, vc, acg, lam_init):
    s = qc.shape[0]
    t = T_C
    n_grp = C_HEADS // 2
    return pl.pallas_call(
        functools.partial(_diff_kernel, lam_init=lam_init),
        grid=(s // t,),
        in_specs=[_resident((4, C_QK_DIM)), _resident((1, LANES)),
                  pl.BlockSpec((t, C_WIDTH), lambda i: (i, 0)),
                  _resident((s, C_WIDTH)), _resident((s, C_WIDTH)),
                  pl.BlockSpec((t, C_WIDTH), lambda i: (i, A_WIDTH // C_WIDTH))],
        out_specs=pl.BlockSpec((t, C_WIDTH), lambda i: (i, 0)),
        out_shape=jax.ShapeDtypeStruct((s, C_WIDTH), BF16),
        scratch_shapes=[pltpu.VMEM((n_grp, C_MAPS // n_grp * t, C_WIDTH), BF16),
                        pltpu.VMEM((n_grp, C_MAPS // n_grp * t, t), F32),
                        pltpu.VMEM((C_MAPS, t, LANES), F32),
                        pltpu.VMEM((C_MAPS, t, LANES), F32)],
        compiler_params=_cparams(),
        name="diff_attn",
    )(lam4, g2, qc, kc, vc, acg)


def _silu(t):
    return t * jax.nn.sigmoid(t)


def _conv_input(gate):
    b = gate[:, G_B:G_B + 4 * B_WIDTH]
    return b[:, B_WIDTH:2 * B_WIDTH] * b[:, 2 * B_WIDTH:3 * B_WIDTH]


def _conv_taps(u, prev0, prev1):
    row = lax.broadcasted_iota(jnp.int32, u.shape, 0)
    u1 = jnp.where(row == 0, prev1, pltpu.roll(u, 1, 0))
    u2 = jnp.where(row == 0, prev0, jnp.where(row == 1, prev1, pltpu.roll(u, 2, 0)))
    return u2, u1


def _mix_and_project(x, gate, oa, oc, u2, u1, u, cw, w_ref):
    bw = B_WIDTH
    conv = u2 * cw[0:1] + u1 * cw[1:2] + u * cw[2:3]
    ob = gate[:, G_B:G_B + bw] * conv
    y = jnp.concatenate([oa * _silu(gate[:, G_AG:G_AG + A_WIDTH]),
                         ob * _silu(gate[:, G_B + 3 * bw:G_B + 4 * bw]),
                         oc * _silu(gate[:, G_CG:G_CG + C_WIDTH])], axis=-1)
    return x + jnp.dot(y.astype(BF16), w_ref[0], preferred_element_type=F32)


def _final_norm(x, fg):
    ms = jnp.mean(x * x, axis=-1, keepdims=True)
    return x * lax.rsqrt(ms + NORM_EPS) * fg


def _merge_kernel(x_ref, ya_ref, yb_ref, yc_ref, w_ref, fg_ref, xo_ref, *, final):
    y = jnp.concatenate([ya_ref[...], yb_ref[...], yc_ref[...]], axis=-1)
    out = x_ref[...] + jnp.dot(y, w_ref[0], preferred_element_type=F32)
    xo_ref[...] = _final_norm(out, fg_ref[...]) if final else out


def _merge(x, ya, yb, yc, w_all, layer, fg, ts, final):
    n, d = x.shape
    row = lambda width: pl.BlockSpec((ts, width), lambda i: (i, 0))
    return pl.pallas_call(
        functools.partial(_merge_kernel, final=final),
        grid=(n // ts,),
        in_specs=[row(d), row(A_WIDTH), row(B_WIDTH), row(C_WIDTH),
                  _layer_weights(w_all, layer), _resident((1, d))],
        out_specs=row(d),
        out_shape=jax.ShapeDtypeStruct((n, d), F32),
        compiler_params=_cparams(),
        name="merge",
    )(x, ya, yb, yc, w_all, fg)


def _stack_heads(q, n_groups, width):
    grp = lax.broadcasted_iota(jnp.int32, q.shape, 1) // width
    return jnp.concatenate([jnp.where(grp == g, q, jnp.zeros_like(q)) for g in range(n_groups)], axis=0)


def _cached_attention(qs, kt_c, vt_c, k_n, v_n, bias_c, bias_n):
    s_c = jnp.dot(qs, kt_c, preferred_element_type=F32)
    s_n = _nt_dot(qs, k_n)
    if bias_c is not None:
        s_c, s_n = s_c + bias_c, s_n + bias_n
    m = jnp.maximum(jnp.max(s_c, axis=-1, keepdims=True), jnp.max(s_n, axis=-1, keepdims=True))
    e_c = jnp.exp2(s_c - m)
    e_n = jnp.exp2(s_n - m)
    l = jnp.sum(e_c, axis=-1, keepdims=True) + jnp.sum(e_n, axis=-1, keepdims=True)
    o = _nt_dot(e_c.astype(BF16), vt_c) + jnp.dot(e_n.astype(BF16), v_n, preferred_element_type=F32)
    return o / l


def _sample_kernel(lam_ref, g_ref, bc_ref, bn_ref,
                   x_ref, qa_ref, ka_ref, va_ref, qc_ref, kc_ref, vc_ref, gate_ref,
                   cak_ref, cav_ref, cck_ref, ccv_ref, conv_ref, w_ref, cw_ref, fg_ref,
                   xo_ref, ul_ref, *, lam_init, final):
    t = qa_ref.shape[0]
    qs = _stack_heads(qa_ref[...], A_HEADS, HEAD_DIM)
    of = _cached_attention(qs, cak_ref[0, 0].astype(BF16), cav_ref[0, 0].astype(BF16),
                           ka_ref[...], va_ref[...], bc_ref[...], bn_ref[...])
    grp = lax.broadcasted_iota(jnp.int32, (t, A_WIDTH), 1) // HEAD_DIM
    oa = jnp.zeros((t, A_WIDTH), F32)
    for h in range(A_HEADS):
        oa = jnp.where(grp == h, of[h * t:(h + 1) * t], oa)
    qs = _stack_heads(qc_ref[...], C_MAPS, C_QK_DIM)
    of = _cached_attention(qs, cck_ref[0, 0].astype(BF16), ccv_ref[0, 0].astype(BF16),
                           kc_ref[...], vc_ref[...], None, None)
    lam = _diff_lambda(lam_ref, lam_init)
    grp = lax.broadcasted_iota(jnp.int32, (t, C_WIDTH), 1) // C_V_DIM
    d = jnp.zeros((t, C_WIDTH), F32)
    for h in range(C_HEADS):
        dh = of[2 * h * t:(2 * h + 1) * t] - lam * of[(2 * h + 1) * t:(2 * h + 2) * t]
        d = jnp.where(grp == h, dh, d)
    oc = jnp.concatenate([_subln(d[:, p * LANES:(p + 1) * LANES], g_ref[...], lam_init)
                          for p in range(C_WIDTH // LANES)], axis=-1)
    gate = gate_ref[...]
    u = _conv_input(gate)
    prev = conv_ref[0]
    u2, u1 = _conv_taps(u, prev[0:1], prev[1:2])
    out = _mix_and_project(x_ref[...], gate, oa, oc, u2, u1, u, cw_ref[...], w_ref)
    ul_ref[0] = u[t - 8:]
    xo_ref[...] = _final_norm(out, fg_ref[...]) if final else out


def _sample_step(layer, lam4, g2, bias_c, bias_n, x, gate, qa, ka, va, qc, kc, vc,
                 cak_t, cav_t, cck_t, ccv_t, conv, w_all, cw, fg, t, lam_init, final):
    n, d = x.shape
    nb = n // t
    row = lambda width: pl.BlockSpec((t, width), lambda b: (b, 0))
    cache = lambda a: pl.BlockSpec((1, 1) + a.shape[2:], lambda b: (layer, b, 0, 0))
    return pl.pallas_call(
        functools.partial(_sample_kernel, lam_init=lam_init, final=final),
        grid=(nb,),
        in_specs=[_resident((4, C_QK_DIM)), _resident((1, LANES)),
                  _resident(bias_c.shape), _resident(bias_n.shape),
                  row(d), row(A_WIDTH), row(A_WIDTH), row(A_WIDTH),
                  row(C_WIDTH), row(C_WIDTH), row(C_WIDTH), row(GATE_W),
                  cache(cak_t), cache(cav_t), cache(cck_t), cache(ccv_t),
                  pl.BlockSpec((1,) + conv.shape[1:], lambda b: (b, 0, 0)),
                  _layer_weights(w_all, layer), _resident(cw.shape), _resident((1, d))],
        out_specs=(row(d), pl.BlockSpec((1, 8, B_WIDTH), lambda b: (b, 0, 0))),
        out_shape=(jax.ShapeDtypeStruct((n, d), F32),
                   jax.ShapeDtypeStruct((nb, 8, B_WIDTH), F32)),
        compiler_params=_cparams(),
        name="sample_step",
    )(lam4, g2, bias_c, bias_n, x, qa, ka, va, qc, kc, vc, gate,
      cak_t, cav_t, cck_t, ccv_t, conv, w_all, cw, fg)


def _feature_major(cache):
    nd = cache.ndim
    c = jnp.transpose(cache, (0, 1) + tuple(range(3, nd)) + (2,))
    return c.reshape(c.shape[0], c.shape[1], -1, c.shape[-1])


def kernel(x_prompt, x_sample, cache_a_k, cache_a_v, state_conv, cache_c_k, cache_c_v, norm_g, w_in, w_out, rel_bias, conv_w, lam_q1, lam_k1, lam_q2, lam_k2, subln_g, final_g):
    depth = norm_g.shape[0]
    batch, seq, d_model = x_prompt.shape
    nb, t, _ = x_sample.shape
    past = cache_c_k.shape[2]
    win = cache_a_k.shape[2]
    ts = TS_PROMPT
    assert batch == 1 and win == A_WIN and seq % T_C == 0 and seq % ts == 0 and seq >= WIN_A
    assert t % 16 == 0 and past % LANES == 0
    keep = min(A_WIN, seq)
    assert keep % ts == 0

    xp = x_prompt.reshape(seq, d_model)
    xs = x_sample.reshape(nb * t, d_model)
    rope_p = _rope_angle_tables(jnp.arange(0, seq, ts), jnp.arange(ts))
    rope_s = _rope_angle_tables(jnp.full((1,), past), jnp.tile(jnp.arange(t), nb))
    w_in_b = w_in.astype(BF16)
    w_out_b = w_out.astype(BF16)
    fg = final_g.reshape(1, d_model).astype(F32)
    cak_t, cav_t = _feature_major(cache_a_k), _feature_major(cache_a_v)
    cck_t, ccv_t = _feature_major(cache_c_k), _feature_major(cache_c_v)

    outs = {k: [] for k in ("pak", "pav", "pcv", "pck", "pcc", "sak", "sav", "scv", "sck", "scc")}
    for l in range(depth):
        final = l == depth - 1
        lam_init = 0.8 - 0.6 * math.exp(-0.3 * l)
        g = norm_g[l].reshape(1, d_model).astype(F32)
        cw = conv_w[l].astype(F32)
        lam4 = jnp.stack([lam_q1[l], lam_k1[l], lam_q2[l], lam_k2[l]]).astype(F32)
        g2 = jnp.tile(subln_g[l].astype(F32), LANES // C_V_DIM).reshape(1, LANES)
        bias = _bias_tables(rel_bias[l])

        acg, akv, qa, ka, va, qc, kc, vc, ckr_t, cvr_t, yb, tail = _inproj(
            xp, g, w_in_b, l, *rope_p, ts, keep, cw)
        ya = _band_attention(qa, ka, va, bias, acg)
        yc = _diff_attention(lam4, g2, qc, kc, vc, acg, lam_init)
        xp = _merge(xp, ya, yb, yc, w_out_b, l, fg, TS_MERGE, final)
        outs["pak"].append(akv[:, :A_WIDTH].reshape(1, keep, A_HEADS, HEAD_DIM))
        outs["pav"].append(akv[:, A_WIDTH:].reshape(1, keep, A_HEADS, HEAD_DIM))
        outs["pcv"].append(tail[8 - (CONV_WIDTH - 1):][None])
        outs["pck"].append(jnp.transpose(ckr_t.reshape(C_HEADS, 2, C_QK_DIM, seq), (3, 0, 1, 2))[None])
        outs["pcc"].append(jnp.transpose(cvr_t.reshape(C_HEADS, C_V_DIM, seq), (2, 0, 1))[None])

        gate, akv, qa, ka, va, qc, kc, vc, ckr, cvr = _inproj(
            xs, g, w_in_b, l, *rope_s, nb * t, nb * t)
        unmasked = bias[N_BIAS - 1, :, :t, :A_WIN + t].reshape(A_HEADS * t, A_WIN + t)
        xs, ul = _sample_step(
            l, lam4, g2, unmasked[:, :A_WIN], unmasked[:, A_WIN:], xs, gate, qa, ka, va, qc, kc, vc,
            cak_t, cav_t, cck_t, ccv_t, state_conv[l], w_out_b, cw, fg, t, lam_init, final)
        outs["sak"].append(akv[:, :A_WIDTH].reshape(nb, t, A_HEADS, HEAD_DIM))
        outs["sav"].append(akv[:, A_WIDTH:].reshape(nb, t, A_HEADS, HEAD_DIM))
        outs["scv"].append(ul[:, 8 - (CONV_WIDTH - 1):])
        outs["sck"].append(ckr.reshape(nb, t, C_HEADS, 2, C_QK_DIM))
        outs["scc"].append(cvr.reshape(nb, t, C_HEADS, C_V_DIM))

    st = lambda k: jnp.stack(outs[k])
    return (xp.reshape(batch, seq, d_model), xs.reshape(nb, t, d_model),
            st("pak"), st("pav"), st("pcv"), st("pck"), st("pcc"),
            st("sak"), st("sav"), st("scv"), st("sck"), st("scc"))
```

```python
import functools
import math

import numpy as np
import jax
import jax.numpy as jnp
from jax import lax
from jax.experimental import pallas as pl
from jax.experimental.pallas import tpu as pltpu

F32 = jnp.float32
BF16 = jnp.bfloat16

CHUNK = 64
HEAD_DIM = 64
A_HEADS = 8
A_WIDTH = A_HEADS * HEAD_DIM
A_PREV_CHUNKS = 8
A_WIN = A_PREV_CHUNKS * CHUNK
REL_CLIP = 128
B_WIDTH = 256
CONV_WIDTH = 3
C_HEADS = 4
C_QK_DIM = 32
C_V_DIM = 64
C_WIDTH = C_HEADS * C_V_DIM
C_MAPS = 2 * C_HEADS
ROPE_DIMS = 8
ROPE_THETA = 500000.0
NORM_EPS = 1e-6
SUBLN_EPS = 1e-5
D_IN_PROJ = 4096
COL_AQ, COL_AK, COL_AV, COL_AG = 0, 512, 1024, 1536
COL_B = 2048
COL_CQ, COL_CK, COL_CV, COL_CG = 3072, 3328, 3584, 3840
GATE_W = A_WIDTH + 4 * B_WIDTH + C_WIDTH
G_AG, G_B, G_CG = 0, A_WIDTH, A_WIDTH + 4 * B_WIDTH

LANES = 128
HALF = LANES // 2
NEG = -1e30
LOG2E = math.log2(math.e)
A_QSCALE = HEAD_DIM ** -0.5 * LOG2E
C_QSCALE = C_QK_DIM ** -0.5 * LOG2E
VMEM_LIMIT = 48 * 1024 * 1024

PROJ_COLS = 512
TS_PROMPT = 512
TS_MERGE = 1024
TQ_A = 256
WIN_A = A_WIN + TQ_A
ROLL_A = 1024
N_BIAS = 4
T_C = 512


def _nt_dot(a, b):
    return lax.dot_general(a, b, (((1,), (1,)), ((), ())), preferred_element_type=F32)


def _cparams(n_axes=1):
    return pltpu.CompilerParams(dimension_semantics=("parallel",) * n_axes,
                                vmem_limit_bytes=VMEM_LIMIT)


def _resident(shape):
    nd = len(shape)
    return pl.BlockSpec(shape, lambda *_: (0,) * nd, pipeline_mode=pl.Buffered(1))


def _layer_weights(w_all, layer):
    return pl.BlockSpec((1,) + w_all.shape[1:], lambda *_: (layer, 0, 0), pipeline_mode=pl.Buffered(1))


def _rope(x, cos, sa, sb):
    return (x * cos + pltpu.roll(x, LANES - ROPE_DIMS // 2, 1) * sa
            + pltpu.roll(x, ROPE_DIMS // 2, 1) * sb)


def _rope_coefficients(base, rt_ref):
    cb, sb_ = base[0:1], base[1:2]
    ct, st = rt_ref[0], rt_ref[1]
    cos_p = cb * ct - sb_ * st
    sin_p = sb_ * ct + cb * st
    d = lax.broadcasted_iota(jnp.int32, cos_p.shape, 1) % C_QK_DIM
    half = ROPE_DIMS // 2
    cos = jnp.where(d < ROPE_DIMS, cos_p, 1.0)
    sa = jnp.where(d < half, -sin_p, 0.0)
    sb = jnp.where((d >= half) & (d < ROPE_DIMS), sin_p, 0.0)
    return cos, sa, sb


def _inproj_kernel(*refs, prompt):
    x_ref, g_ref, w_ref, rb_ref, rt_ref = refs[:5]
    if prompt:
        (cw_ref, acg_ref, akv_ref, qa_ref, ka_ref, va_ref, qc_ref, kc_ref, vc_ref, ckr_ref, cvr_ref,
         yb_ref, tail_ref, carry_ref) = refs[5:]
    else:
        gate_ref, akv_ref, qa_ref, ka_ref, va_ref, qc_ref, kc_ref, vc_ref, ckr_ref, cvr_ref = refs[5:]

    if prompt:
        @pl.when(pl.program_id(0) == 0)
        def _():
            carry_ref[...] = jnp.zeros(carry_ref.shape, F32)

    x = x_ref[...]
    ms = jnp.mean(x * x, axis=-1, keepdims=True)
    h = (x * lax.rsqrt(ms + NORM_EPS) * g_ref[...]).astype(BF16)
    cos, sa, sb = _rope_coefficients(rb_ref[0], rt_ref)

    def proj(lo, width):
        return jnp.dot(h, w_ref[0, :, lo:lo + width], preferred_element_type=F32)

    def put_f32(ref, t, val):
        if prompt:
            ref[t * LANES:(t + 1) * LANES, :] = val.T
        else:
            ref[:, t * LANES:(t + 1) * LANES] = val

    zc = proj(COL_CQ, 2 * C_WIDTH)
    for t in range(C_WIDTH // LANES):
        cq = _rope(zc[:, t * LANES:(t + 1) * LANES], cos, sa, sb)
        qc_ref[:, t * LANES:(t + 1) * LANES] = (cq * C_QSCALE).astype(BF16)
        ck = _rope(zc[:, C_WIDTH + t * LANES:C_WIDTH + (t + 1) * LANES], cos, sa, sb)
        kc_ref[:, t * LANES:(t + 1) * LANES] = ck.astype(BF16)
        put_f32(ckr_ref, t, ck)
    zc = proj(COL_CV, 2 * C_WIDTH)
    vc_ref[...] = zc[:, :C_WIDTH].astype(BF16)
    for t in range(C_WIDTH // LANES):
        put_f32(cvr_ref, t, zc[:, t * LANES:(t + 1) * LANES])
    if prompt:
        acg_ref[:, A_WIDTH:] = zc[:, C_WIDTH:]
    else:
        gate_ref[:, G_CG:G_CG + C_WIDTH] = zc[:, C_WIDTH:]

    zb = proj(COL_B, 4 * B_WIDTH)
    if prompt:
        bw = B_WIDTH
        u = zb[:, bw:2 * bw] * zb[:, 2 * bw:3 * bw]
        prev = carry_ref[...]
        u2, u1 = _conv_taps(u, prev[6:7], prev[7:8])
        cw = cw_ref[...]
        conv = u2 * cw[0:1] + u1 * cw[1:2] + u * cw[2:3]
        yb_ref[...] = (zb[:, :bw] * conv * _silu(zb[:, 3 * bw:])).astype(BF16)
        carry_ref[...] = u[u.shape[0] - 8:]
        tail_ref[...] = u[u.shape[0] - 8:]
    else:
        gate_ref[:, G_B:G_B + 4 * B_WIDTH] = zb

    zc = proj(COL_AK, A_WIDTH)
    ka_ref[...] = zc.astype(BF16)
    akv_ref[:, :A_WIDTH] = zc
    zc = proj(COL_AV, A_WIDTH)
    va_ref[...] = zc.astype(BF16)
    akv_ref[:, A_WIDTH:] = zc
    zc = proj(COL_AG, A_WIDTH)
    if prompt:
        acg_ref[:, :A_WIDTH] = zc
    else:
        gate_ref[:, G_AG:G_AG + A_WIDTH] = zc
    qa_ref[...] = (proj(COL_AQ, A_WIDTH) * A_QSCALE).astype(BF16)


def _inproj(x, g, w_all, layer, rope_base, rope_off, ts, keep, cw=None):
    prompt = cw is not None
    n, d = x.shape
    nt = n // ts
    first_keep = nt - keep // ts
    row = lambda width: pl.BlockSpec((ts, width), lambda i: (i, 0))
    sds = jax.ShapeDtypeStruct
    if prompt:
        kv_shape, kv_spec = (C_WIDTH, n), pl.BlockSpec((C_WIDTH, ts), lambda i: (0, i))
        gate_w = A_WIDTH + C_WIDTH
    else:
        kv_shape, kv_spec = (n, C_WIDTH), row(C_WIDTH)
        gate_w = GATE_W
    akv_spec = pl.BlockSpec((ts, 2 * A_WIDTH), lambda i: (jnp.maximum(i - first_keep, 0), 0))
    out_shape = [sds((n, gate_w), F32), sds((keep, 2 * A_WIDTH), F32),
                 sds((n, A_WIDTH), BF16), sds((n, A_WIDTH), BF16), sds((n, A_WIDTH), BF16),
                 sds((n, C_WIDTH), BF16), sds((n, C_WIDTH), BF16), sds((n, C_WIDTH), BF16),
                 sds(kv_shape, F32), sds(kv_shape, F32)]
    out_specs = [row(gate_w), akv_spec, row(A_WIDTH), row(A_WIDTH), row(A_WIDTH),
                 row(C_WIDTH), row(C_WIDTH), row(C_WIDTH), kv_spec, kv_spec]
    in_specs = [row(d), _resident((1, d)),
                _layer_weights(w_all, layer),
                pl.BlockSpec((1, 2, LANES), lambda i: (i, 0, 0)), _resident(rope_off.shape)]
    args = [x, g, w_all, rope_base, rope_off]
    scratch = []
    if prompt:
        in_specs.append(_resident(cw.shape))
        args.append(cw)
        out_shape += [sds((n, B_WIDTH), BF16), sds((8, B_WIDTH), F32)]
        out_specs += [row(B_WIDTH), pl.BlockSpec((8, B_WIDTH), lambda i: (0, 0))]
        scratch = [pltpu.VMEM((8, B_WIDTH), F32)]
    return pl.pallas_call(
        functools.partial(_inproj_kernel, prompt=prompt),
        grid=(nt,),
        in_specs=in_specs,
        out_specs=tuple(out_specs),
        out_shape=tuple(out_shape),
        scratch_shapes=scratch,
        compiler_params=pltpu.CompilerParams(dimension_semantics=("arbitrary",),
                                             vmem_limit_bytes=VMEM_LIMIT),
        name="inproj",
    )(*args)


def _rope_angle_tables(tile_base, row_off):
    half = ROPE_DIMS // 2
    inv_freq = ROPE_THETA ** (-jnp.arange(half, dtype=F32) * (2.0 / ROPE_DIMS))
    lane_freq = inv_freq[(np.arange(LANES) % C_QK_DIM) % half][None, :]
    a = tile_base.astype(F32)[:, None] * lane_freq
    b = row_off.astype(F32)[:, None] * lane_freq
    return jnp.stack([jnp.cos(a), jnp.sin(a)], axis=1), jnp.stack([jnp.cos(b), jnp.sin(b)], axis=0)


def _bias_kernel(u_ref, b_ref):
    v = pl.program_id(0)
    qc = lax.broadcasted_iota(jnp.int32, (TQ_A, WIN_A), 0) // CHUNK
    kc = lax.broadcasted_iota(jnp.int32, (TQ_A, WIN_A), 1) // CHUNK
    top = qc + jnp.minimum(v, 2) * (TQ_A // CHUNK)
    valid = ((kc <= top) & (kc >= top - A_PREV_CHUNKS)) | (v == N_BIAS - 1)
    for h in range(A_HEADS):
        u = jnp.broadcast_to(u_ref[0, h], (TQ_A, ROLL_A))
        t = pltpu.roll(u, 0, 1, stride=1, stride_axis=0)[:, ROLL_A - WIN_A:]
        b_ref[0, h] = jnp.where(valid, t * LOG2E, NEG)


def _bias_tables(rel_bias_l):
    tab = rel_bias_l.astype(F32)
    rows = []
    for off in (0, TQ_A, A_WIN, A_WIN):
        n_hi = off + (ROLL_A - WIN_A) - REL_CLIP
        n_lo = ROLL_A - n_hi - (2 * REL_CLIP + 1)
        rows.append(jnp.concatenate([jnp.broadcast_to(tab[:, -1:], (A_HEADS, n_hi)), tab[:, ::-1],
                                     jnp.broadcast_to(tab[:, :1], (A_HEADS, n_lo))], axis=1))
    u = jnp.stack(rows)[:, :, None, :]
    return pl.pallas_call(
        _bias_kernel,
        grid=(N_BIAS,),
        in_specs=[pl.BlockSpec((1, A_HEADS, 1, ROLL_A), lambda v: (v, 0, 0, 0))],
        out_specs=pl.BlockSpec((1, A_HEADS, TQ_A, WIN_A), lambda v: (v, 0, 0, 0)),
        out_shape=jax.ShapeDtypeStruct((N_BIAS, A_HEADS, TQ_A, WIN_A), F32),
        compiler_params=_cparams(),
        name="bias",
    )(u)


def _band_kernel(q_ref, k0_ref, k1_ref, k2_ref, v0_ref, v1_ref, v2_ref, b_ref, ag_ref, o_ref, s_ref):
    i = pl.program_id(0)
    n_pair = A_WIDTH // LANES
    lo = lax.broadcasted_iota(jnp.int32, (TQ_A, LANES), 1) < HALF
    lo_w = lax.broadcasted_iota(jnp.int32, (WIN_A, LANES), 1) < HALF

    @pl.when(i == 0)
    def _():
        s_ref[1] = jnp.zeros(s_ref.shape[1:], F32)

    def step(cur):
        for p in range(n_pair):
            cols = slice(p * LANES, (p + 1) * LANES)
            q2 = q_ref[:, cols]
            qs = jnp.concatenate([jnp.where(lo, q2, jnp.zeros_like(q2)),
                                  jnp.where(lo, jnp.zeros_like(q2), q2)], axis=0)
            kw = jnp.concatenate([r[:, cols] for r in (k0_ref, k1_ref, k2_ref)], axis=0)
            s_ref[cur, 2 * p * TQ_A:(2 * p + 2) * TQ_A, :] = _nt_dot(qs, kw)
            vw = jnp.concatenate([r[:, cols] for r in (v0_ref, v1_ref, v2_ref)], axis=0)
            outs = []
            for half in range(2):
                h = 2 * p + half
                sel_w = lo_w if half == 0 else jnp.logical_not(lo_w)
                v_aug = jnp.where(sel_w, vw, jnp.ones_like(vw))
                s = s_ref[1 - cur, h * TQ_A:(h + 1) * TQ_A, :] + b_ref[0, h]
                m = jnp.max(s, axis=-1, keepdims=True)
                e = jnp.exp2(s - m).astype(BF16)
                o = jnp.dot(e, v_aug, preferred_element_type=F32)
                outs.append(o / pltpu.roll(o, HALF, 1))
            gated = jnp.where(lo, outs[0], outs[1]) * _silu(ag_ref[:, cols])
            o_ref[:, cols] = gated.astype(BF16)

    parity = jnp.bitwise_and(i, 1)
    pl.when(parity == 0)(functools.partial(step, 0))
    pl.when(parity == 1)(functools.partial(step, 1))


def _band_attention(qa, ka, va, bias, acg):
    s = qa.shape[0]
    nq = s // TQ_A
    q_tile = lambda i: jnp.minimum(i, nq - 1)
    o_tile = lambda i: jnp.maximum(i - 1, 0)
    first = lambda tile: jnp.maximum(tile - A_WIN // TQ_A, 0)
    k_spec = lambda j: pl.BlockSpec((TQ_A, A_WIDTH), lambda i: (first(q_tile(i)) + j, 0))
    v_spec = lambda j: pl.BlockSpec((TQ_A, A_WIDTH), lambda i: (first(o_tile(i)) + j, 0))
    return pl.pallas_call(
        _band_kernel,
        grid=(nq + 1,),
        in_specs=[pl.BlockSpec((TQ_A, A_WIDTH), lambda i: (q_tile(i), 0)),
                  k_spec(0), k_spec(1), k_spec(2), v_spec(0), v_spec(1), v_spec(2),
                  pl.BlockSpec((1, A_HEADS, TQ_A, WIN_A),
                               lambda i: (jnp.minimum(o_tile(i), 2), 0, 0, 0)),
                  pl.BlockSpec((TQ_A, A_WIDTH), lambda i: (o_tile(i), 0))],
        out_specs=pl.BlockSpec((TQ_A, A_WIDTH), lambda i: (o_tile(i), 0)),
        out_shape=jax.ShapeDtypeStruct((s, A_WIDTH), BF16),
        scratch_shapes=[pltpu.VMEM((2, A_HEADS * TQ_A, WIN_A), F32)],
        compiler_params=pltpu.CompilerParams(dimension_semantics=("arbitrary",),
                                             vmem_limit_bytes=VMEM_LIMIT),
        name="band_attn",
    )(qa, ka, ka, ka, va, va, va, bias, acg)


def _diff_lambda(lam_ref, lam_init):
    a = jnp.sum(lam_ref[0:1, :] * lam_ref[1:2, :], axis=-1, keepdims=True)
    b = jnp.sum(lam_ref[2:3, :] * lam_ref[3:4, :], axis=-1, keepdims=True)
    return jnp.exp(a) - jnp.exp(b) + lam_init


def _subln(d, g2, lam_init):
    lo = lax.broadcasted_iota(jnp.int32, d.shape, 1) < HALF
    sq = d * d
    s_lo = jnp.sum(jnp.where(lo, sq, 0.0), axis=-1, keepdims=True)
    s_hi = jnp.sum(jnp.where(lo, 0.0, sq), axis=-1, keepdims=True)
    ms = jnp.where(lo, s_lo, s_hi) * (1.0 / C_V_DIM)
    return d * lax.rsqrt(ms + SUBLN_EPS) * g2 * (1.0 - lam_init)


def _diff_kernel(lam_ref, g_ref, q_ref, k_ref, v_ref, cg_ref, o_ref, qm_ref, s_ref, m_ref, acc_ref,
                 *, lam_init):
    i = pl.program_id(0)
    t = T_C
    n_grp = C_HEADS // 2
    per_grp = C_MAPS // n_grp
    q = q_ref[...]
    lane_grp = lax.broadcasted_iota(jnp.int32, (t, C_WIDTH), 1) // C_QK_DIM
    for hm in range(C_MAPS):
        g, loc = divmod(hm, per_grp)
        qm_ref[g, loc * t:(loc + 1) * t, :] = jnp.where(lane_grp == hm, q, jnp.zeros_like(q))
    m_ref[...] = jnp.full(m_ref.shape, -jnp.inf, F32)
    acc_ref[...] = jnp.zeros(acc_ref.shape, F32)
    lo = lax.broadcasted_iota(jnp.int32, (t, LANES), 1) < HALF

    def k_tile(j):
        return k_ref[pl.ds(pl.multiple_of(j * t, t), t), :]

    def v_tile(j):
        return v_ref[pl.ds(pl.multiple_of(j * t, t), t), :]

    def scores(g, k_t):
        s_ref[g] = _nt_dot(qm_ref[g], k_t)

    def softmax_pv(g, v_t, mask):
        v2 = v_t[:, g * LANES:(g + 1) * LANES]
        for half in range(2):
            h = 2 * g + half
            sel = lo if half == 0 else jnp.logical_not(lo)
            v_aug = jnp.where(sel, v2, jnp.ones_like(v2))
            ps, alphas = [], []
            for mp in range(2):
                hm = 2 * h + mp
                loc = hm - g * per_grp
                s = s_ref[g, loc * t:(loc + 1) * t, :]
                if mask is not None:
                    s = jnp.where(mask, s, -jnp.inf)
                m_old = m_ref[hm]
                m_new = jnp.maximum(m_old, jnp.max(s, axis=-1, keepdims=True))
                alphas.append(jnp.exp2(m_old - m_new))
                ps.append(jnp.exp2(s - jnp.tile(m_new, (1, t // LANES))).astype(BF16))
                m_ref[hm] = m_new
            pv = jnp.dot(jnp.concatenate(ps, axis=0), v_aug, preferred_element_type=F32)
            for mp in range(2):
                hm = 2 * h + mp
                acc_ref[hm] = alphas[mp] * acc_ref[hm] + pv[mp * t:(mp + 1) * t]

    scores(0, k_tile(0))

    def full_tile(j):
        v_t = v_tile(j)
        scores(1, k_tile(j))
        softmax_pv(0, v_t, None)
        scores(0, k_tile(j + 1))
        softmax_pv(1, v_t, None)

    odd = jnp.bitwise_and(i, 1)

    @pl.when(odd == 1)
    def _():
        full_tile(0)

    def body(jj, carry):
        j = odd + 2 * jj
        full_tile(j)
        full_tile(j + 1)
        return carry

    lax.fori_loop(0, lax.shift_right_logical(i, 1), body, 0)
    qchunk = lax.broadcasted_iota(jnp.int32, (t, t), 0) // CHUNK
    kchunk = lax.broadcasted_iota(jnp.int32, (t, t), 1) // CHUNK
    mask = kchunk <= qchunk
    v_t = v_tile(i)
    scores(1, k_tile(i))
    softmax_pv(0, v_t, mask)
    softmax_pv(1, v_t, mask)

    lam = _diff_lambda(lam_ref, lam_init)
    for p_ in range(C_WIDTH // LANES):
        d = []
        for half in range(2):
            h = 2 * p_ + half
            r = []
            for mp in range(2):
                acc = acc_ref[2 * h + mp]
                r.append(acc / pltpu.roll(acc, HALF, 1))
            d.append(r[0] - lam * r[1])
        pair = jnp.where(lo, d[0], d[1])
        tile = slice(p_ * LANES, (p_ + 1) * LANES)
        o_ref[:, tile] = (_subln(pair, g_ref[...], lam_init) * _silu(cg_ref[:, tile])).astype(BF16)


def _diff_attention(lam4, g2, qc, kc, vc, acg, lam_init):
    s = qc.shape[0]
    t = T_C
    n_grp = C_HEADS // 2
    return pl.pallas_call(
        functools.partial(_diff_kernel, lam_init=lam_init),
        grid=(s // t,),
        in_specs=[_resident((4, C_QK_DIM)), _resident((1, LANES)),
                  pl.BlockSpec((t, C_WIDTH), lambda i: (i, 0)),
                  _resident((s, C_WIDTH)), _resident((s, C_WIDTH)),
                  pl.BlockSpec((t, C_WIDTH), lambda i: (i, A_WIDTH // C_WIDTH))],
        out_specs=pl.BlockSpec((t, C_WIDTH), lambda i: (i, 0)),
        out_shape=jax.ShapeDtypeStruct((s, C_WIDTH), BF16),
        scratch_shapes=[pltpu.VMEM((n_grp, C_MAPS // n_grp * t, C_WIDTH), BF16),
                        pltpu.VMEM((n_grp, C_MAPS // n_grp * t, t), F32),
                        pltpu.VMEM((C_MAPS, t, LANES), F32),
                        pltpu.VMEM((C_MAPS, t, LANES), F32)],
        compiler_params=_cparams(),
        name="diff_attn",
    )(lam4, g2, qc, kc, vc, acg)


def _silu(t):
    return t * jax.nn.sigmoid(t)


def _conv_input(gate):
    b = gate[:, G_B:G_B + 4 * B_WIDTH]
    return b[:, B_WIDTH:2 * B_WIDTH] * b[:, 2 * B_WIDTH:3 * B_WIDTH]


def _conv_taps(u, prev0, prev1):
    row = lax.broadcasted_iota(jnp.int32, u.shape, 0)
    u1 = jnp.where(row == 0, prev1, pltpu.roll(u, 1, 0))
    u2 = jnp.where(row == 0, prev0, jnp.where(row == 1, prev1, pltpu.roll(u, 2, 0)))
    return u2, u1


def _mix_and_project(x, gate, oa, oc, u2, u1, u, cw, w_ref):
    bw = B_WIDTH
    conv = u2 * cw[0:1] + u1 * cw[1:2] + u * cw[2:3]
    ob = gate[:, G_B:G_B + bw] * conv
    y = jnp.concatenate([oa * _silu(gate[:, G_AG:G_AG + A_WIDTH]),
                         ob * _silu(gate[:, G_B + 3 * bw:G_B + 4 * bw]),
                         oc * _silu(gate[:, G_CG:G_CG + C_WIDTH])], axis=-1)
    return x + jnp.dot(y.astype(BF16), w_ref[0], preferred_element_type=F32)


def _final_norm(x, fg):
    ms = jnp.mean(x * x, axis=-1, keepdims=True)
    return x * lax.rsqrt(ms + NORM_EPS) * fg


def _merge_kernel(x_ref, ya_ref, yb_ref, yc_ref, w_ref, fg_ref, xo_ref, *, final):
    y = jnp.concatenate([ya_ref[...], yb_ref[...], yc_ref[...]], axis=-1)
    out = x_ref[...] + jnp.dot(y, w_ref[0], preferred_element_type=F32)
    xo_ref[...] = _final_norm(out, fg_ref[...]) if final else out


def _merge(x, ya, yb, yc, w_all, layer, fg, ts, final):
    n, d = x.shape
    row = lambda width: pl.BlockSpec((ts, width), lambda i: (i, 0))
    return pl.pallas_call(
        functools.partial(_merge_kernel, final=final),
        grid=(n // ts,),
        in_specs=[row(d), row(A_WIDTH), row(B_WIDTH), row(C_WIDTH),
                  _layer_weights(w_all, layer), _resident((1, d))],
        out_specs=row(d),
        out_shape=jax.ShapeDtypeStruct((n, d), F32),
        compiler_params=_cparams(),
        name="merge",
    )(x, ya, yb, yc, w_all, fg)


def _stack_heads(q, n_groups, width):
    grp = lax.broadcasted_iota(jnp.int32, q.shape, 1) // width
    return jnp.concatenate([jnp.where(grp == g, q, jnp.zeros_like(q)) for g in range(n_groups)], axis=0)


def _cached_attention(qs, kt_c, vt_c, k_n, v_n, bias_c, bias_n):
    s_c = jnp.dot(qs, kt_c, preferred_element_type=F32)
    s_n = _nt_dot(qs, k_n)
    if bias_c is not None:
        s_c, s_n = s_c + bias_c, s_n + bias_n
    m = jnp.maximum(jnp.max(s_c, axis=-1, keepdims=True), jnp.max(s_n, axis=-1, keepdims=True))
    e_c = jnp.exp2(s_c - m)
    e_n = jnp.exp2(s_n - m)
    l = jnp.sum(e_c, axis=-1, keepdims=True) + jnp.sum(e_n, axis=-1, keepdims=True)
    o = _nt_dot(e_c.astype(BF16), vt_c) + jnp.dot(e_n.astype(BF16), v_n, preferred_element_type=F32)
    return o / l


def _sample_kernel(lam_ref, g_ref, bc_ref, bn_ref,
                   x_ref, qa_ref, ka_ref, va_ref, qc_ref, kc_ref, vc_ref, gate_ref,
                   cak_ref, cav_ref, cck_ref, ccv_ref, conv_ref, w_ref, cw_ref, fg_ref,
                   xo_ref, ul_ref, *, lam_init, final):
    t = qa_ref.shape[0]
    qs = _stack_heads(qa_ref[...], A_HEADS, HEAD_DIM)
    of = _cached_attention(qs, cak_ref[0, 0].astype(BF16), cav_ref[0, 0].astype(BF16),
                           ka_ref[...], va_ref[...], bc_ref[...], bn_ref[...])
    grp = lax.broadcasted_iota(jnp.int32, (t, A_WIDTH), 1) // HEAD_DIM
    oa = jnp.zeros((t, A_WIDTH), F32)
    for h in range(A_HEADS):
        oa = jnp.where(grp == h, of[h * t:(h + 1) * t], oa)
    qs = _stack_heads(qc_ref[...], C_MAPS, C_QK_DIM)
    of = _cached_attention(qs, cck_ref[0, 0].astype(BF16), ccv_ref[0, 0].astype(BF16),
                           kc_ref[...], vc_ref[...], None, None)
    lam = _diff_lambda(lam_ref, lam_init)
    grp = lax.broadcasted_iota(jnp.int32, (t, C_WIDTH), 1) // C_V_DIM
    d = jnp.zeros((t, C_WIDTH), F32)
    for h in range(C_HEADS):
        dh = of[2 * h * t:(2 * h + 1) * t] - lam * of[(2 * h + 1) * t:(2 * h + 2) * t]
        d = jnp.where(grp == h, dh, d)
    oc = jnp.concatenate([_subln(d[:, p * LANES:(p + 1) * LANES], g_ref[...], lam_init)
                          for p in range(C_WIDTH // LANES)], axis=-1)
    gate = gate_ref[...]
    u = _conv_input(gate)
    prev = conv_ref[0]
    u2, u1 = _conv_taps(u, prev[0:1], prev[1:2])
    out = _mix_and_project(x_ref[...], gate, oa, oc, u2, u1, u, cw_ref[...], w_ref)
    ul_ref[0] = u[t - 8:]
    xo_ref[...] = _final_norm(out, fg_ref[...]) if final else out


def _sample_step(layer, lam4, g2, bias_c, bias_n, x, gate, qa, ka, va, qc, kc, vc,
                 cak_t, cav_t, cck_t, ccv_t, conv, w_all, cw, fg, t, lam_init, final):
    n, d = x.shape
    nb = n // t
    row = lambda width: pl.BlockSpec((t, width), lambda b: (b, 0))
    cache = lambda a: pl.BlockSpec((1, 1) + a.shape[2:], lambda b: (layer, b, 0, 0))
    return pl.pallas_call(
        functools.partial(_sample_kernel, lam_init=lam_init, final=final),
        grid=(nb,),
        in_specs=[_resident((4, C_QK_DIM)), _resident((1, LANES)),
                  _resident(bias_c.shape), _resident(bias_n.shape),
                  row(d), row(A_WIDTH), row(A_WIDTH), row(A_WIDTH),
                  row(C_WIDTH), row(C_WIDTH), row(C_WIDTH), row(GATE_W),
                  cache(cak_t), cache(cav_t), cache(cck_t), cache(ccv_t),
                  pl.BlockSpec((1,) + conv.shape[1:], lambda b: (b, 0, 0)),
                  _layer_weights(w_all, layer), _resident(cw.shape), _resident((1, d))],
        out_specs=(row(d), pl.BlockSpec((1, 8, B_WIDTH), lambda b: (b, 0, 0))),
        out_shape=(jax.ShapeDtypeStruct((n, d), F32),
                   jax.ShapeDtypeStruct((nb, 8, B_WIDTH), F32)),
        compiler_params=_cparams(),
        name="sample_step",
    )(lam4, g2, bias_c, bias_n, x, qa, ka, va, qc, kc, vc, gate,
      cak_t, cav_t, cck_t, ccv_t, conv, w_all, cw, fg)


def _feature_major(cache):
    nd = cache.ndim
    c = jnp.transpose(cache, (0, 1) + tuple(range(3, nd)) + (2,))
    return c.reshape(c.shape[0], c.shape[1], -1, c.shape[-1])


def kernel(x_prompt, x_sample, cache_a_k, cache_a_v, state_conv, cache_c_k, cache_c_v, norm_g, w_in, w_out, rel_bias, conv_w, lam_q1, lam_k1, lam_q2, lam_k2, subln_g, final_g):
    depth = norm_g.shape[0]
    batch, seq, d_model = x_prompt.shape
    nb, t, _ = x_sample.shape
    past = cache_c_k.shape[2]
    win = cache_a_k.shape[2]
    ts = TS_PROMPT
    assert batch == 1 and win == A_WIN and seq % T_C == 0 and seq % ts == 0 and seq >= WIN_A
    assert t % 16 == 0 and past % LANES == 0
    keep = min(A_WIN, seq)
    assert keep % ts == 0

    xp = x_prompt.reshape(seq, d_model)
    xs = x_sample.reshape(nb * t, d_model)
    rope_p = _rope_angle_tables(jnp.arange(0, seq, ts), jnp.arange(ts))
    rope_s = _rope_angle_tables(jnp.full((1,), past), jnp.tile(jnp.arange(t), nb))
    w_in_b = w_in.astype(BF16)
    w_out_b = w_out.astype(BF16)
    fg = final_g.reshape(1, d_model).astype(F32)
    cak_t, cav_t = _feature_major(cache_a_k), _feature_major(cache_a_v)
    cck_t, ccv_t = _feature_major(cache_c_k), _feature_major(cache_c_v)

    outs = {k: [] for k in ("pak", "pav", "pcv", "pck", "pcc", "sak", "sav", "scv", "sck", "scc")}
    for l in range(depth):
        final = l == depth - 1
        lam_init = 0.8 - 0.6 * math.exp(-0.3 * l)
        g = norm_g[l].reshape(1, d_model).astype(F32)
        cw = conv_w[l].astype(F32)
        lam4 = jnp.stack([lam_q1[l], lam_k1[l], lam_q2[l], lam_k2[l]]).astype(F32)
        g2 = jnp.tile(subln_g[l].astype(F32), LANES // C_V_DIM).reshape(1, LANES)
        bias = _bias_tables(rel_bias[l])

        acg, akv, qa, ka, va, qc, kc, vc, ckr_t, cvr_t, yb, tail = _inproj(
            xp, g, w_in_b, l, *rope_p, ts, keep, cw)
        ya = _band_attention(qa, ka, va, bias, acg)
        yc = _diff_attention(lam4, g2, qc, kc, vc, acg, lam_init)
        xp = _merge(xp, ya, yb, yc, w_out_b, l, fg, TS_MERGE, final)
        outs["pak"].append(akv[:, :A_WIDTH].reshape(1, keep, A_HEADS, HEAD_DIM))
        outs["pav"].append(akv[:, A_WIDTH:].reshape(1, keep, A_HEADS, HEAD_DIM))
        outs["pcv"].append(tail[8 - (CONV_WIDTH - 1):][None])
        outs["pck"].append(jnp.transpose(ckr_t.reshape(C_HEADS, 2, C_QK_DIM, seq), (3, 0, 1, 2))[None])
        outs["pcc"].append(jnp.transpose(cvr_t.reshape(C_HEADS, C_V_DIM, seq), (2, 0, 1))[None])

        gate, akv, qa, ka, va, qc, kc, vc, ckr, cvr = _inproj(
            xs, g, w_in_b, l, *rope_s, nb * t, nb * t)
        unmasked = bias[N_BIAS - 1, :, :t, :A_WIN + t].reshape(A_HEADS * t, A_WIN + t)
        xs, ul = _sample_step(
            l, lam4, g2, unmasked[:, :A_WIN], unmasked[:, A_WIN:], xs, gate, qa, ka, va, qc, kc, vc,
            cak_t, cav_t, cck_t, ccv_t, state_conv[l], w_out_b, cw, fg, t, lam_init, final)
        outs["sak"].append(akv[:, :A_WIDTH].reshape(nb, t, A_HEADS, HEAD_DIM))
        outs["sav"].append(akv[:, A_WIDTH:].reshape(nb, t, A_HEADS, HEAD_DIM))
        outs["scv"].append(ul[:, 8 - (CONV_WIDTH - 1):])
        outs["sck"].append(ckr.reshape(nb, t, C_HEADS, 2, C_QK_DIM))
        outs["scc"].append(cvr.reshape(nb, t, C_HEADS, C_V_DIM))

    st = lambda k: jnp.stack(outs[k])
    return (xp.reshape(batch, seq, d_model), xs.reshape(nb, t, d_model),
            st("pak"), st("pav"), st("pcv"), st("pck"), st("pcc"),
            st("sak"), st("sav"), st("scv"), st("sck"), st("scc"))
```

```python
import functools
import math

import numpy as np
import jax
import jax.numpy as jnp
from jax import lax
from jax.experimental import pallas as pl
from jax.experimental.pallas import tpu as pltpu

F32 = jnp.float32
BF16 = jnp.bfloat16

CHUNK = 64
HEAD_DIM = 64
A_HEADS = 8
A_WIDTH = A_HEADS * HEAD_DIM
A_PREV_CHUNKS = 8
A_WIN = A_PREV_CHUNKS * CHUNK
REL_CLIP = 128
B_WIDTH = 256
CONV_WIDTH = 3
C_HEADS = 4
C_QK_DIM = 32
C_V_DIM = 64
C_WIDTH = C_HEADS * C_V_DIM
C_MAPS = 2 * C_HEADS
ROPE_DIMS = 8
ROPE_THETA = 500000.0
NORM_EPS = 1e-6
SUBLN_EPS = 1e-5
D_IN_PROJ = 4096
COL_AQ, COL_AK, COL_AV, COL_AG = 0, 512, 1024, 1536
COL_B = 2048
COL_CQ, COL_CK, COL_CV, COL_CG = 3072, 3328, 3584, 3840
GATE_W = A_WIDTH + 4 * B_WIDTH + C_WIDTH
G_AG, G_B, G_CG = 0, A_WIDTH, A_WIDTH + 4 * B_WIDTH

LANES = 128
HALF = LANES // 2
NEG = -1e30
LOG2E = math.log2(math.e)
A_QSCALE = HEAD_DIM ** -0.5 * LOG2E
C_QSCALE = C_QK_DIM ** -0.5 * LOG2E
VMEM_LIMIT = 48 * 1024 * 1024

PROJ_COLS = 512
TS_PROMPT = 512
TS_MERGE = 1024
TQ_A = 256
WIN_A = A_WIN + TQ_A
ROLL_A = 1024
N_BIAS = 4
T_C = 512


def _nt_dot(a, b):
    return lax.dot_general(a, b, (((1,), (1,)), ((), ())), preferred_element_type=F32)


def _cparams(n_axes=1):
    return pltpu.CompilerParams(dimension_semantics=("parallel",) * n_axes,
                                vmem_limit_bytes=VMEM_LIMIT)


def _resident(shape):
    nd = len(shape)
    return pl.BlockSpec(shape, lambda *_: (0,) * nd, pipeline_mode=pl.Buffered(1))


def _layer_weights(w_all, layer):
    return pl.BlockSpec((1,) + w_all.shape[1:], lambda *_: (layer, 0, 0), pipeline_mode=pl.Buffered(1))


def _rope(x, cos, sa, sb):
    return (x * cos + pltpu.roll(x, LANES - ROPE_DIMS // 2, 1) * sa
            + pltpu.roll(x, ROPE_DIMS // 2, 1) * sb)


def _rope_coefficients(base, rt_ref):
    cb, sb_ = base[0:1], base[1:2]
    ct, st = rt_ref[0], rt_ref[1]
    cos_p = cb * ct - sb_ * st
    sin_p = sb_ * ct + cb * st
    d = lax.broadcasted_iota(jnp.int32, cos_p.shape, 1) % C_QK_DIM
    half = ROPE_DIMS // 2
    cos = jnp.where(d < ROPE_DIMS, cos_p, 1.0)
    sa = jnp.where(d < half, -sin_p, 0.0)
    sb = jnp.where((d >= half) & (d < ROPE_DIMS), sin_p, 0.0)
    return cos, sa, sb


def _inproj_kernel(*refs, prompt, n_aliased, zero_rest):
    x_ref, g_ref, w_ref, rb_ref, rt_ref = refs[:5]
    if prompt:
        cw_ref = refs[5]
        (acg_ref, akv_ref, qa_ref, ka_ref, va_ref, qc_ref, kc_ref, vc_ref, ckr_ref, cvr_ref,
         yb_ref, tail_ref, carry_ref) = refs[6 + n_aliased:]
    else:
        gate_ref, akv_ref, qa_ref, ka_ref, va_ref, qc_ref, kc_ref, vc_ref, ckr_ref, cvr_ref = refs[5:]

    if prompt:
        @pl.when(pl.program_id(0) == 0)
        def _():
            carry_ref[...] = jnp.zeros(carry_ref.shape, F32)

    x = x_ref[...]
    ms = jnp.mean(x * x, axis=-1, keepdims=True)
    h = (x * lax.rsqrt(ms + NORM_EPS) * g_ref[...]).astype(BF16)
    cos, sa, sb = _rope_coefficients(rb_ref[0], rt_ref)

    def proj(lo, width):
        return jnp.dot(h, w_ref[0, :, lo:lo + width], preferred_element_type=F32)

    def put_f32(ref, t, val):
        if prompt:
            ref[0, t * LANES:(t + 1) * LANES, :] = val.T
            if zero_rest:
                ref[1:, t * LANES:(t + 1) * LANES, :] = jnp.zeros(
                    (ref.shape[0] - 1, LANES, val.shape[0]), F32)
        else:
            ref[:, t * LANES:(t + 1) * LANES] = val

    zc = proj(COL_CQ, 2 * C_WIDTH)
    for t in range(C_WIDTH // LANES):
        cq = _rope(zc[:, t * LANES:(t + 1) * LANES], cos, sa, sb)
        qc_ref[:, t * LANES:(t + 1) * LANES] = (cq * C_QSCALE).astype(BF16)
        ck = _rope(zc[:, C_WIDTH + t * LANES:C_WIDTH + (t + 1) * LANES], cos, sa, sb)
        kc_ref[:, t * LANES:(t + 1) * LANES] = ck.astype(BF16)
        put_f32(ckr_ref, t, ck)
    zc = proj(COL_CV, 2 * C_WIDTH)
    vc_ref[...] = zc[:, :C_WIDTH].astype(BF16)
    for t in range(C_WIDTH // LANES):
        put_f32(cvr_ref, t, zc[:, t * LANES:(t + 1) * LANES])
    if prompt:
        acg_ref[:, A_WIDTH:] = zc[:, C_WIDTH:]
    else:
        gate_ref[:, G_CG:G_CG + C_WIDTH] = zc[:, C_WIDTH:]

    zb = proj(COL_B, 4 * B_WIDTH)
    if prompt:
        bw = B_WIDTH
        u = zb[:, bw:2 * bw] * zb[:, 2 * bw:3 * bw]
        prev = carry_ref[...]
        u2, u1 = _conv_taps(u, prev[6:7], prev[7:8])
        cw = cw_ref[...]
        conv = u2 * cw[0:1] + u1 * cw[1:2] + u * cw[2:3]
        yb_ref[...] = (zb[:, :bw] * conv * _silu(zb[:, 3 * bw:])).astype(BF16)
        carry_ref[...] = u[u.shape[0] - 8:]
        tail_ref[...] = u[u.shape[0] - 8:]
    else:
        gate_ref[:, G_B:G_B + 4 * B_WIDTH] = zb

    zc = proj(COL_AK, A_WIDTH)
    ka_ref[...] = zc.astype(BF16)
    akv_ref[:, :A_WIDTH] = zc
    zc = proj(COL_AV, A_WIDTH)
    va_ref[...] = zc.astype(BF16)
    akv_ref[:, A_WIDTH:] = zc
    zc = proj(COL_AG, A_WIDTH)
    if prompt:
        acg_ref[:, :A_WIDTH] = zc
    else:
        gate_ref[:, G_AG:G_AG + A_WIDTH] = zc
    qa_ref[...] = (proj(COL_AQ, A_WIDTH) * A_QSCALE).astype(BF16)


def _inproj(x, g, w_all, layer, rope_base, rope_off, ts, keep, cw=None, kv_acc=None):
    prompt = cw is not None
    depth = w_all.shape[0]
    n, d = x.shape
    nt = n // ts
    first_keep = nt - keep // ts
    row = lambda width: pl.BlockSpec((ts, width), lambda i: (i, 0))
    sds = jax.ShapeDtypeStruct
    if prompt:
        kv_shape = (depth, C_WIDTH, n)
        slabs = depth if kv_acc is None else 1
        kv_spec = pl.BlockSpec((slabs, C_WIDTH, ts), lambda i: (layer, 0, i))
        gate_w = A_WIDTH + C_WIDTH
    else:
        kv_shape, kv_spec = (n, C_WIDTH), row(C_WIDTH)
        gate_w = GATE_W
    akv_spec = pl.BlockSpec((ts, 2 * A_WIDTH), lambda i: (jnp.maximum(i - first_keep, 0), 0))
    out_shape = [sds((n, gate_w), F32), sds((keep, 2 * A_WIDTH), F32),
                 sds((n, A_WIDTH), BF16), sds((n, A_WIDTH), BF16), sds((n, A_WIDTH), BF16),
                 sds((n, C_WIDTH), BF16), sds((n, C_WIDTH), BF16), sds((n, C_WIDTH), BF16),
                 sds(kv_shape, F32), sds(kv_shape, F32)]
    out_specs = [row(gate_w), akv_spec, row(A_WIDTH), row(A_WIDTH), row(A_WIDTH),
                 row(C_WIDTH), row(C_WIDTH), row(C_WIDTH), kv_spec, kv_spec]
    in_specs = [row(d), _resident((1, d)),
                _layer_weights(w_all, layer),
                pl.BlockSpec((1, 2, LANES), lambda i: (i, 0, 0)), _resident(rope_off.shape)]
    args = [x, g, w_all, rope_base, rope_off]
    scratch = []
    aliases = {}
    if prompt:
        in_specs.append(_resident(cw.shape))
        args.append(cw)
        if kv_acc is not None:
            aliases = {len(args) + j: 8 + j for j in range(len(kv_acc))}
            in_specs += [pl.BlockSpec(memory_space=pl.ANY)] * len(kv_acc)
            args += list(kv_acc)
        out_shape += [sds((n, B_WIDTH), BF16), sds((8, B_WIDTH), F32)]
        out_specs += [row(B_WIDTH), pl.BlockSpec((8, B_WIDTH), lambda i: (0, 0))]
        scratch = [pltpu.VMEM((8, B_WIDTH), F32)]
    return pl.pallas_call(
        functools.partial(_inproj_kernel, prompt=prompt, n_aliased=len(aliases),
                          zero_rest=prompt and kv_acc is None and depth > 1),
        grid=(nt,),
        in_specs=in_specs,
        out_specs=tuple(out_specs),
        out_shape=tuple(out_shape),
        scratch_shapes=scratch,
        input_output_aliases=aliases,
        compiler_params=pltpu.CompilerParams(dimension_semantics=("arbitrary",),
                                             vmem_limit_bytes=VMEM_LIMIT),
        name="inproj",
    )(*args)


def _rope_angle_tables(tile_base, row_off):
    half = ROPE_DIMS // 2
    inv_freq = ROPE_THETA ** (-jnp.arange(half, dtype=F32) * (2.0 / ROPE_DIMS))
    lane_freq = inv_freq[(np.arange(LANES) % C_QK_DIM) % half][None, :]
    a = tile_base.astype(F32)[:, None] * lane_freq
    b = row_off.astype(F32)[:, None] * lane_freq
    return jnp.stack([jnp.cos(a), jnp.sin(a)], axis=1), jnp.stack([jnp.cos(b), jnp.sin(b)], axis=0)


def _bias_kernel(u_ref, b_ref):
    v = pl.program_id(0)
    qc = lax.broadcasted_iota(jnp.int32, (TQ_A, WIN_A), 0) // CHUNK
    kc = lax.broadcasted_iota(jnp.int32, (TQ_A, WIN_A), 1) // CHUNK
    top = qc + jnp.minimum(v, 2) * (TQ_A // CHUNK)
    valid = ((kc <= top) & (kc >= top - A_PREV_CHUNKS)) | (v == N_BIAS - 1)
    for h in range(A_HEADS):
        u = jnp.broadcast_to(u_ref[0, h], (TQ_A, ROLL_A))
        t = pltpu.roll(u, 0, 1, stride=1, stride_axis=0)[:, ROLL_A - WIN_A:]
        b_ref[0, h] = jnp.where(valid, t * LOG2E, NEG)


def _bias_tables(rel_bias_l):
    tab = rel_bias_l.astype(F32)
    rows = []
    for off in (0, TQ_A, A_WIN, A_WIN):
        n_hi = off + (ROLL_A - WIN_A) - REL_CLIP
        n_lo = ROLL_A - n_hi - (2 * REL_CLIP + 1)
        rows.append(jnp.concatenate([jnp.broadcast_to(tab[:, -1:], (A_HEADS, n_hi)), tab[:, ::-1],
                                     jnp.broadcast_to(tab[:, :1], (A_HEADS, n_lo))], axis=1))
    u = jnp.stack(rows)[:, :, None, :]
    return pl.pallas_call(
        _bias_kernel,
        grid=(N_BIAS,),
        in_specs=[pl.BlockSpec((1, A_HEADS, 1, ROLL_A), lambda v: (v, 0, 0, 0))],
        out_specs=pl.BlockSpec((1, A_HEADS, TQ_A, WIN_A), lambda v: (v, 0, 0, 0)),
        out_shape=jax.ShapeDtypeStruct((N_BIAS, A_HEADS, TQ_A, WIN_A), F32),
        compiler_params=_cparams(),
        name="bias",
    )(u)


def _band_kernel(q_ref, k0_ref, k1_ref, k2_ref, v0_ref, v1_ref, v2_ref, b_ref, ag_ref, o_ref, s_ref):
    i = pl.program_id(0)
    n_pair = A_WIDTH // LANES
    lo = lax.broadcasted_iota(jnp.int32, (TQ_A, LANES), 1) < HALF
    lo_w = lax.broadcasted_iota(jnp.int32, (WIN_A, LANES), 1) < HALF

    @pl.when(i == 0)
    def _():
        s_ref[1] = jnp.zeros(s_ref.shape[1:], F32)

    def step(cur):
        for p in range(n_pair):
            cols = slice(p * LANES, (p + 1) * LANES)
            q2 = q_ref[:, cols]
            qs = jnp.concatenate([jnp.where(lo, q2, jnp.zeros_like(q2)),
                                  jnp.where(lo, jnp.zeros_like(q2), q2)], axis=0)
            kw = jnp.concatenate([r[:, cols] for r in (k0_ref, k1_ref, k2_ref)], axis=0)
            s_ref[cur, 2 * p * TQ_A:(2 * p + 2) * TQ_A, :] = _nt_dot(qs, kw)
            vw = jnp.concatenate([r[:, cols] for r in (v0_ref, v1_ref, v2_ref)], axis=0)
            outs = []
            for half in range(2):
                h = 2 * p + half
                sel_w = lo_w if half == 0 else jnp.logical_not(lo_w)
                v_aug = jnp.where(sel_w, vw, jnp.ones_like(vw))
                s = s_ref[1 - cur, h * TQ_A:(h + 1) * TQ_A, :] + b_ref[0, h]
                m = jnp.max(s, axis=-1, keepdims=True)
                e = jnp.exp2(s - m).astype(BF16)
                o = jnp.dot(e, v_aug, preferred_element_type=F32)
                outs.append(o / pltpu.roll(o, HALF, 1))
            gated = jnp.where(lo, outs[0], outs[1]) * _silu(ag_ref[:, cols])
            o_ref[:, cols] = gated.astype(BF16)

    parity = jnp.bitwise_and(i, 1)
    pl.when(parity == 0)(functools.partial(step, 0))
    pl.when(parity == 1)(functools.partial(step, 1))


def _band_attention(qa, ka, va, bias, acg):
    s = qa.shape[0]
    nq = s // TQ_A
    q_tile = lambda i: jnp.minimum(i, nq - 1)
    o_tile = lambda i: jnp.maximum(i - 1, 0)
    first = lambda tile: jnp.maximum(tile - A_WIN // TQ_A, 0)
    k_spec = lambda j: pl.BlockSpec((TQ_A, A_WIDTH), lambda i: (first(q_tile(i)) + j, 0))
    v_spec = lambda j: pl.BlockSpec((TQ_A, A_WIDTH), lambda i: (first(o_tile(i)) + j, 0))
    return pl.pallas_call(
        _band_kernel,
        grid=(nq + 1,),
        in_specs=[pl.BlockSpec((TQ_A, A_WIDTH), lambda i: (q_tile(i), 0)),
                  k_spec(0), k_spec(1), k_spec(2), v_spec(0), v_spec(1), v_spec(2),
                  pl.BlockSpec((1, A_HEADS, TQ_A, WIN_A),
                               lambda i: (jnp.minimum(o_tile(i), 2), 0, 0, 0)),
                  pl.BlockSpec((TQ_A, A_WIDTH), lambda i: (o_tile(i), 0))],
        out_specs=pl.BlockSpec((TQ_A, A_WIDTH), lambda i: (o_tile(i), 0)),
        out_shape=jax.ShapeDtypeStruct((s, A_WIDTH), BF16),
        scratch_shapes=[pltpu.VMEM((2, A_HEADS * TQ_A, WIN_A), F32)],
        compiler_params=pltpu.CompilerParams(dimension_semantics=("arbitrary",),
                                             vmem_limit_bytes=VMEM_LIMIT),
        name="band_attn",
    )(qa, ka, ka, ka, va, va, va, bias, acg)


def _diff_lambda(lam_ref, lam_init):
    a = jnp.sum(lam_ref[0:1, :] * lam_ref[1:2, :], axis=-1, keepdims=True)
    b = jnp.sum(lam_ref[2:3, :] * lam_ref[3:4, :], axis=-1, keepdims=True)
    return jnp.exp(a) - jnp.exp(b) + lam_init


def _subln(d, g2, lam_init):
    lo = lax.broadcasted_iota(jnp.int32, d.shape, 1) < HALF
    sq = d * d
    s_lo = jnp.sum(jnp.where(lo, sq, 0.0), axis=-1, keepdims=True)
    s_hi = jnp.sum(jnp.where(lo, 0.0, sq), axis=-1, keepdims=True)
    ms = jnp.where(lo, s_lo, s_hi) * (1.0 / C_V_DIM)
    return d * lax.rsqrt(ms + SUBLN_EPS) * g2 * (1.0 - lam_init)


def _diff_kernel(lam_ref, g_ref, q_ref, k_ref, v_ref, cg_ref, o_ref, qm_ref, s_ref, m_ref, acc_ref,
                 *, lam_init):
    i = pl.program_id(0)
    t = T_C
    n_grp = C_HEADS // 2
    per_grp = C_MAPS // n_grp
    q = q_ref[...]
    lane_grp = lax.broadcasted_iota(jnp.int32, (t, C_WIDTH), 1) // C_QK_DIM
    lo = lax.broadcasted_iota(jnp.int32, (t, LANES), 1) < HALF

    def mask_queries(g):
        for loc in range(per_grp):
            hm = g * per_grp + loc
            qm_ref[g, loc * t:(loc + 1) * t, :] = jnp.where(lane_grp == hm, q, jnp.zeros_like(q))

    def k_tile(j):
        return k_ref[pl.ds(pl.multiple_of(j * t, t), t), :]

    def v_tile(j):
        return v_ref[pl.ds(pl.multiple_of(j * t, t), t), :]

    def scores(g, k_t):
        s_ref[g] = _nt_dot(qm_ref[g], k_t)

    def softmax_pv(g, v_t, mask):
        v2 = v_t[:, g * LANES:(g + 1) * LANES]
        for half in range(2):
            h = 2 * g + half
            sel = lo if half == 0 else jnp.logical_not(lo)
            v_aug = jnp.where(sel, v2, jnp.ones_like(v2))
            ps, alphas = [], []
            for mp in range(2):
                hm = 2 * h + mp
                loc = hm - g * per_grp
                s = s_ref[g, loc * t:(loc + 1) * t, :]
                if mask is not None:
                    s = jnp.where(mask, s, -jnp.inf)
                m_old = m_ref[hm]
                m_new = jnp.maximum(m_old, jnp.max(s, axis=-1, keepdims=True))
                alphas.append(jnp.exp2(m_old - m_new))
                ps.append(jnp.exp2(s - jnp.tile(m_new, (1, t // LANES))).astype(BF16))
                m_ref[hm] = m_new
            pv = jnp.dot(jnp.concatenate(ps, axis=0), v_aug, preferred_element_type=F32)
            for mp in range(2):
                hm = 2 * h + mp
                acc_ref[hm] = alphas[mp] * acc_ref[hm] + pv[mp * t:(mp + 1) * t]

    def finalize(p_, lam):
        d = []
        for half in range(2):
            h = 2 * p_ + half
            r = []
            for mp in range(2):
                acc = acc_ref[2 * h + mp]
                r.append(acc / pltpu.roll(acc, HALF, 1))
            d.append(r[0] - lam * r[1])
        pair = jnp.where(lo, d[0], d[1])
        tile = slice(p_ * LANES, (p_ + 1) * LANES)
        o_ref[:, tile] = (_subln(pair, g_ref[...], lam_init) * _silu(cg_ref[:, tile])).astype(BF16)

    mask_queries(0)
    scores(0, k_tile(0))
    mask_queries(1)
    m_ref[...] = jnp.full(m_ref.shape, -jnp.inf, F32)
    acc_ref[...] = jnp.zeros(acc_ref.shape, F32)

    def full_tile(j):
        v_t = v_tile(j)
        scores(1, k_tile(j))
        softmax_pv(0, v_t, None)
        scores(0, k_tile(j + 1))
        softmax_pv(1, v_t, None)

    odd = jnp.bitwise_and(i, 1)

    @pl.when(odd == 1)
    def _():
        full_tile(0)

    def body(jj, carry):
        j = odd + 2 * jj
        full_tile(j)
        full_tile(j + 1)
        return carry

    lax.fori_loop(0, lax.shift_right_logical(i, 1), body, 0)
    qchunk = lax.broadcasted_iota(jnp.int32, (t, t), 0) // CHUNK
    kchunk = lax.broadcasted_iota(jnp.int32, (t, t), 1) // CHUNK
    mask = kchunk <= qchunk
    v_t = v_tile(i)
    lam = _diff_lambda(lam_ref, lam_init)
    scores(1, k_tile(i))
    softmax_pv(0, v_t, mask)
    finalize(0, lam)
    softmax_pv(1, v_t, mask)
    finalize(1, lam)


def _diff_attention(lam4, g2, qc, kc, vc, acg, lam_init):
    s = qc.shape[0]
    t = T_C
    n_grp = C_HEADS // 2
    return pl.pallas_call(
        functools.partial(_diff_kernel, lam_init=lam_init),
        grid=(s // t,),
        in_specs=[_resident((4, C_QK_DIM)), _resident((1, LANES)),
                  pl.BlockSpec((t, C_WIDTH), lambda i: (i, 0)),
                  _resident((s, C_WIDTH)), _resident((s, C_WIDTH)),
                  pl.BlockSpec((t, C_WIDTH), lambda i: (i, A_WIDTH // C_WIDTH))],
        out_specs=pl.BlockSpec((t, C_WIDTH), lambda i: (i, 0)),
        out_shape=jax.ShapeDtypeStruct((s, C_WIDTH), BF16),
        scratch_shapes=[pltpu.VMEM((n_grp, C_MAPS // n_grp * t, C_WIDTH), BF16),
                        pltpu.VMEM((n_grp, C_MAPS // n_grp * t, t), F32),
                        pltpu.VMEM((C_MAPS, t, LANES), F32),
                        pltpu.VMEM((C_MAPS, t, LANES), F32)],
        compiler_params=_cparams(),
        name="diff_attn",
    )(lam4, g2, qc, kc, vc, acg)


def _silu(t):
    return t * jax.nn.sigmoid(t)


def _conv_input(gate):
    b = gate[:, G_B:G_B + 4 * B_WIDTH]
    return b[:, B_WIDTH:2 * B_WIDTH] * b[:, 2 * B_WIDTH:3 * B_WIDTH]


def _conv_taps(u, prev0, prev1):
    row = lax.broadcasted_iota(jnp.int32, u.shape, 0)
    u1 = jnp.where(row == 0, prev1, pltpu.roll(u, 1, 0))
    u2 = jnp.where(row == 0, prev0, jnp.where(row == 1, prev1, pltpu.roll(u, 2, 0)))
    return u2, u1


def _gated_mix(gate, oa, oc, u2, u1, u, cw):
    bw = B_WIDTH
    conv = u2 * cw[0:1] + u1 * cw[1:2] + u * cw[2:3]
    ob = gate[:, G_B:G_B + bw] * conv
    y = jnp.concatenate([oa * _silu(gate[:, G_AG:G_AG + A_WIDTH]),
                         ob * _silu(gate[:, G_B + 3 * bw:G_B + 4 * bw]),
                         oc * _silu(gate[:, G_CG:G_CG + C_WIDTH])], axis=-1)
    return y.astype(BF16)


def _final_norm(x, fg):
    ms = jnp.mean(x * x, axis=-1, keepdims=True)
    return x * lax.rsqrt(ms + NORM_EPS) * fg


def _merge_kernel(*refs, final):
    x_ref, *y_refs, w_ref, fg_ref, xo_ref = refs
    y = jnp.concatenate([r[...] for r in y_refs], axis=-1) if len(y_refs) > 1 else y_refs[0][...]
    out = x_ref[...] + jnp.dot(y, w_ref[0], preferred_element_type=F32)
    xo_ref[...] = _final_norm(out, fg_ref[...]) if final else out


def _merge(x, ys, w_all, layer, fg, ts, final):
    n, d = x.shape
    row = lambda width: pl.BlockSpec((ts, width), lambda i: (i, 0))
    return pl.pallas_call(
        functools.partial(_merge_kernel, final=final),
        grid=(n // ts,),
        in_specs=[row(d)] + [row(y.shape[1]) for y in ys] + [_layer_weights(w_all, layer), _resident((1, d))],
        out_specs=row(d),
        out_shape=jax.ShapeDtypeStruct((n, d), F32),
        compiler_params=_cparams(),
        name="merge",
    )(x, *ys, w_all, fg)


def _stack_heads(q, n_groups, width):
    grp = lax.broadcasted_iota(jnp.int32, q.shape, 1) // width
    return jnp.concatenate([jnp.where(grp == g, q, jnp.zeros_like(q)) for g in range(n_groups)], axis=0)


def _cached_attention(qs, kt_c, vt_c, k_n, v_n, bias_c, bias_n):
    s_c = jnp.dot(qs, kt_c, preferred_element_type=F32)
    s_n = _nt_dot(qs, k_n)
    if bias_c is not None:
        s_c, s_n = s_c + bias_c, s_n + bias_n
    m = jnp.maximum(jnp.max(s_c, axis=-1, keepdims=True), jnp.max(s_n, axis=-1, keepdims=True))
    e_c = jnp.exp2(s_c - m)
    e_n = jnp.exp2(s_n - m)
    l = jnp.sum(e_c, axis=-1, keepdims=True) + jnp.sum(e_n, axis=-1, keepdims=True)
    o = _nt_dot(e_c.astype(BF16), vt_c) + jnp.dot(e_n.astype(BF16), v_n, preferred_element_type=F32)
    return o / l


def _sample_kernel(lam_ref, g_ref, bc_ref, bn_ref,
                   qa_ref, ka_ref, va_ref, qc_ref, kc_ref, vc_ref, gate_ref,
                   cak_ref, cav_ref, cck_ref, ccv_ref, conv_ref, cw_ref,
                   y_ref, ul_ref, *, lam_init):
    t = qa_ref.shape[0]
    qs = _stack_heads(qa_ref[...], A_HEADS, HEAD_DIM)
    of = _cached_attention(qs, cak_ref[0, 0].astype(BF16), cav_ref[0, 0].astype(BF16),
                           ka_ref[...], va_ref[...], bc_ref[...], bn_ref[...])
    grp = lax.broadcasted_iota(jnp.int32, (t, A_WIDTH), 1) // HEAD_DIM
    oa = jnp.zeros((t, A_WIDTH), F32)
    for h in range(A_HEADS):
        oa = jnp.where(grp == h, of[h * t:(h + 1) * t], oa)
    qs = _stack_heads(qc_ref[...], C_MAPS, C_QK_DIM)
    of = _cached_attention(qs, cck_ref[0, 0].astype(BF16), ccv_ref[0, 0].astype(BF16),
                           kc_ref[...], vc_ref[...], None, None)
    lam = _diff_lambda(lam_ref, lam_init)
    grp = lax.broadcasted_iota(jnp.int32, (t, C_WIDTH), 1) // C_V_DIM
    d = jnp.zeros((t, C_WIDTH), F32)
    for h in range(C_HEADS):
        dh = of[2 * h * t:(2 * h + 1) * t] - lam * of[(2 * h + 1) * t:(2 * h + 2) * t]
        d = jnp.where(grp == h, dh, d)
    oc = jnp.concatenate([_subln(d[:, p * LANES:(p + 1) * LANES], g_ref[...], lam_init)
                          for p in range(C_WIDTH // LANES)], axis=-1)
    gate = gate_ref[...]
    u = _conv_input(gate)
    prev = conv_ref[0]
    u2, u1 = _conv_taps(u, prev[0:1], prev[1:2])
    y_ref[...] = _gated_mix(gate, oa, oc, u2, u1, u, cw_ref[...])
    ul_ref[0] = u[t - 8:]


def _sample_step(layer, lam4, g2, bias_c, bias_n, gate, qa, ka, va, qc, kc, vc,
                 cak_t, cav_t, cck_t, ccv_t, conv, cw, t, lam_init):
    n = gate.shape[0]
    d_mix = A_WIDTH + B_WIDTH + C_WIDTH
    nb = n // t
    row = lambda width: pl.BlockSpec((t, width), lambda b: (b, 0))
    cache = lambda a: pl.BlockSpec((1, 1) + a.shape[2:], lambda b: (layer, b, 0, 0))
    return pl.pallas_call(
        functools.partial(_sample_kernel, lam_init=lam_init),
        grid=(nb,),
        in_specs=[_resident((4, C_QK_DIM)), _resident((1, LANES)),
                  _resident(bias_c.shape), _resident(bias_n.shape),
                  row(A_WIDTH), row(A_WIDTH), row(A_WIDTH),
                  row(C_WIDTH), row(C_WIDTH), row(C_WIDTH), row(GATE_W),
                  cache(cak_t), cache(cav_t), cache(cck_t), cache(ccv_t),
                  pl.BlockSpec((1,) + conv.shape[1:], lambda b: (b, 0, 0)),
                  _resident(cw.shape)],
        out_specs=(row(d_mix), pl.BlockSpec((1, 8, B_WIDTH), lambda b: (b, 0, 0))),
        out_shape=(jax.ShapeDtypeStruct((n, d_mix), BF16),
                   jax.ShapeDtypeStruct((nb, 8, B_WIDTH), F32)),
        compiler_params=_cparams(),
        name="sample_step",
    )(lam4, g2, bias_c, bias_n, qa, ka, va, qc, kc, vc, gate,
      cak_t, cav_t, cck_t, ccv_t, conv, cw)


def _feature_major(cache):
    nd = cache.ndim
    c = jnp.transpose(cache, (0, 1) + tuple(range(3, nd)) + (2,))
    return c.reshape(c.shape[0], c.shape[1], -1, c.shape[-1])


def kernel(x_prompt, x_sample, cache_a_k, cache_a_v, state_conv, cache_c_k, cache_c_v, norm_g, w_in, w_out, rel_bias, conv_w, lam_q1, lam_k1, lam_q2, lam_k2, subln_g, final_g):
    depth = norm_g.shape[0]
    batch, seq, d_model = x_prompt.shape
    nb, t, _ = x_sample.shape
    past = cache_c_k.shape[2]
    win = cache_a_k.shape[2]
    ts = TS_PROMPT
    assert batch == 1 and win == A_WIN and seq % T_C == 0 and seq % ts == 0 and seq >= WIN_A
    assert t % 16 == 0 and past % LANES == 0
    keep = min(A_WIN, seq)
    assert keep % ts == 0

    xp = x_prompt.reshape(seq, d_model)
    xs = x_sample.reshape(nb * t, d_model)
    rope_p = _rope_angle_tables(jnp.arange(0, seq, ts), jnp.arange(ts))
    rope_s = _rope_angle_tables(jnp.full((1,), past), jnp.tile(jnp.arange(t), nb))
    w_in_b = w_in.astype(BF16)
    w_out_b = w_out.astype(BF16)
    fg = final_g.reshape(1, d_model).astype(F32)
    cak_t, cav_t = _feature_major(cache_a_k), _feature_major(cache_a_v)
    cck_t, ccv_t = _feature_major(cache_c_k), _feature_major(cache_c_v)

    outs = {k: [] for k in ("pak", "pav", "pcv", "sak", "sav", "scv", "sck", "scc")}
    kv_acc = None
    for l in range(depth):
        final = l == depth - 1
        lam_init = 0.8 - 0.6 * math.exp(-0.3 * l)
        g = norm_g[l].reshape(1, d_model).astype(F32)
        cw = conv_w[l].astype(F32)
        lam4 = jnp.stack([lam_q1[l], lam_k1[l], lam_q2[l], lam_k2[l]]).astype(F32)
        g2 = jnp.tile(subln_g[l].astype(F32), LANES // C_V_DIM).reshape(1, LANES)
        bias = _bias_tables(rel_bias[l])

        acg, akv, qa, ka, va, qc, kc, vc, ck_all, cv_all, yb, tail = _inproj(
            xp, g, w_in_b, l, *rope_p, ts, keep, cw, kv_acc)
        kv_acc = (ck_all, cv_all)
        ya = _band_attention(qa, ka, va, bias, acg)
        yc = _diff_attention(lam4, g2, qc, kc, vc, acg, lam_init)
        xp = _merge(xp, (ya, yb, yc), w_out_b, l, fg, TS_MERGE, final)
        outs["pak"].append(akv[:, :A_WIDTH].reshape(1, keep, A_HEADS, HEAD_DIM))
        outs["pav"].append(akv[:, A_WIDTH:].reshape(1, keep, A_HEADS, HEAD_DIM))
        outs["pcv"].append(tail[8 - (CONV_WIDTH - 1):][None])

        gate, akv, qa, ka, va, qc, kc, vc, ckr, cvr = _inproj(
            xs, g, w_in_b, l, *rope_s, nb * t, nb * t)
        unmasked = bias[N_BIAS - 1, :, :t, :A_WIN + t].reshape(A_HEADS * t, A_WIN + t)
        ys, ul = _sample_step(
            l, lam4, g2, unmasked[:, :A_WIN], unmasked[:, A_WIN:], gate, qa, ka, va, qc, kc, vc,
            cak_t, cav_t, cck_t, ccv_t, state_conv[l], cw, t, lam_init)
        xs = _merge(xs, (ys,), w_out_b, l, fg, nb * t, final)
        outs["sak"].append(akv[:, :A_WIDTH].reshape(nb, t, A_HEADS, HEAD_DIM))
        outs["sav"].append(akv[:, A_WIDTH:].reshape(nb, t, A_HEADS, HEAD_DIM))
        outs["scv"].append(ul[:, 8 - (CONV_WIDTH - 1):])
        outs["sck"].append(ckr.reshape(nb, t, C_HEADS, 2, C_QK_DIM))
        outs["scc"].append(cvr.reshape(nb, t, C_HEADS, C_V_DIM))

    st = lambda k: jnp.stack(outs[k])
    ck_all, cv_all = kv_acc
    pck = jnp.transpose(ck_all.reshape(depth, C_HEADS, 2, C_QK_DIM, seq), (0, 4, 1, 2, 3))[:, None]
    pcc = jnp.transpose(cv_all.reshape(depth, C_HEADS, C_V_DIM, seq), (0, 3, 1, 2))[:, None]
    return (xp.reshape(batch, seq, d_model), xs.reshape(nb, t, d_model),
            st("pak"), st("pav"), st("pcv"), pck, pcc,
            st("sak"), st("sav"), st("scv"), st("sck"), st("scc"))
```

```python
import functools
import math

import numpy as np
import jax
import jax.numpy as jnp
from jax import lax
from jax.experimental import pallas as pl
from jax.experimental.pallas import tpu as pltpu

F32 = jnp.float32
BF16 = jnp.bfloat16

CHUNK = 64
HEAD_DIM = 64
A_HEADS = 8
A_WIDTH = A_HEADS * HEAD_DIM
A_PREV_CHUNKS = 8
A_WIN = A_PREV_CHUNKS * CHUNK
REL_CLIP = 128
B_WIDTH = 256
CONV_WIDTH = 3
C_HEADS = 4
C_QK_DIM = 32
C_V_DIM = 64
C_WIDTH = C_HEADS * C_V_DIM
C_MAPS = 2 * C_HEADS
ROPE_DIMS = 8
ROPE_THETA = 500000.0
NORM_EPS = 1e-6
SUBLN_EPS = 1e-5
D_IN_PROJ = 4096
COL_AQ, COL_AK, COL_AV, COL_AG = 0, 512, 1024, 1536
COL_B = 2048
COL_CQ, COL_CK, COL_CV, COL_CG = 3072, 3328, 3584, 3840
GATE_W = A_WIDTH + 4 * B_WIDTH + C_WIDTH
G_AG, G_B, G_CG = 0, A_WIDTH, A_WIDTH + 4 * B_WIDTH

LANES = 128
HALF = LANES // 2
NEG = -1e30
LOG2E = math.log2(math.e)
A_QSCALE = HEAD_DIM ** -0.5 * LOG2E
C_QSCALE = C_QK_DIM ** -0.5 * LOG2E
VMEM_LIMIT = 48 * 1024 * 1024

PROJ_COLS = 512
TS_PROMPT = 512
TS_MERGE = 1024
TQ_A = 256
WIN_A = A_WIN + TQ_A
ROLL_A = 1024
N_BIAS = 4
T_C = 512


def _nt_dot(a, b):
    return lax.dot_general(a, b, (((1,), (1,)), ((), ())), preferred_element_type=F32)


def _normalize_pair(acc_even, acc_odd, lo):
    o = jnp.where(lo, acc_even, acc_odd)
    l = jnp.where(lo, acc_odd, acc_even)
    return o / pltpu.roll(l, HALF, 1)


def _cparams(n_axes=1):
    return pltpu.CompilerParams(dimension_semantics=("parallel",) * n_axes,
                                vmem_limit_bytes=VMEM_LIMIT)


def _resident(shape):
    nd = len(shape)
    return pl.BlockSpec(shape, lambda *_: (0,) * nd, pipeline_mode=pl.Buffered(1))


def _layer_weights(w_all, layer):
    return pl.BlockSpec((1,) + w_all.shape[1:], lambda *_: (layer, 0, 0), pipeline_mode=pl.Buffered(1))


def _rope(x, cos, sa, sb):
    return (x * cos + pltpu.roll(x, LANES - ROPE_DIMS // 2, 1) * sa
            + pltpu.roll(x, ROPE_DIMS // 2, 1) * sb)


def _rope_coefficients(base, rt_ref):
    cb, sb_ = base[0:1], base[1:2]
    ct, st = rt_ref[0], rt_ref[1]
    cos_p = cb * ct - sb_ * st
    sin_p = sb_ * ct + cb * st
    d = lax.broadcasted_iota(jnp.int32, cos_p.shape, 1) % C_QK_DIM
    half = ROPE_DIMS // 2
    cos = jnp.where(d < ROPE_DIMS, cos_p, 1.0)
    sa = jnp.where(d < half, -sin_p, 0.0)
    sb = jnp.where((d >= half) & (d < ROPE_DIMS), sin_p, 0.0)
    return cos, sa, sb


def _inproj_kernel(*refs, prompt, n_aliased, zero_rest):
    x_ref, g_ref, w_ref, rb_ref, rt_ref = refs[:5]
    if prompt:
        cw_ref = refs[5]
        (acg_ref, akv_ref, qa_ref, ka_ref, va_ref, qc_ref, kc_ref, vc_ref, ckr_ref, cvr_ref,
         yb_ref, tail_ref, carry_ref) = refs[6 + n_aliased:]
    else:
        gate_ref, akv_ref, qa_ref, ka_ref, va_ref, qc_ref, kc_ref, vc_ref, ckr_ref, cvr_ref = refs[5:]

    if prompt:
        @pl.when(pl.program_id(0) == 0)
        def _():
            carry_ref[...] = jnp.zeros(carry_ref.shape, F32)

    x = x_ref[...]
    ms = jnp.mean(x * x, axis=-1, keepdims=True)
    h = (x * lax.rsqrt(ms + NORM_EPS) * g_ref[...]).astype(BF16)
    cos, sa, sb = _rope_coefficients(rb_ref[0], rt_ref)

    def proj(lo, width):
        return jnp.dot(h, w_ref[0, :, lo:lo + width], preferred_element_type=F32)

    def put_f32(ref, t, val):
        if prompt:
            ref[0, t * LANES:(t + 1) * LANES, :] = val.T
            if zero_rest:
                ref[1:, t * LANES:(t + 1) * LANES, :] = jnp.zeros(
                    (ref.shape[0] - 1, LANES, val.shape[0]), F32)
        else:
            ref[:, t * LANES:(t + 1) * LANES] = val

    zc = proj(COL_CQ, 2 * C_WIDTH)
    for t in range(C_WIDTH // LANES):
        cq = _rope(zc[:, t * LANES:(t + 1) * LANES], cos, sa, sb)
        qc_ref[:, t * LANES:(t + 1) * LANES] = (cq * C_QSCALE).astype(BF16)
        ck = _rope(zc[:, C_WIDTH + t * LANES:C_WIDTH + (t + 1) * LANES], cos, sa, sb)
        kc_ref[:, t * LANES:(t + 1) * LANES] = ck.astype(BF16)
        put_f32(ckr_ref, t, ck)
    zc = proj(COL_CV, 2 * C_WIDTH)
    vc_ref[...] = zc[:, :C_WIDTH].astype(BF16)
    for t in range(C_WIDTH // LANES):
        put_f32(cvr_ref, t, zc[:, t * LANES:(t + 1) * LANES])
    if prompt:
        acg_ref[:, A_WIDTH:] = zc[:, C_WIDTH:]
    else:
        gate_ref[:, G_CG:G_CG + C_WIDTH] = zc[:, C_WIDTH:]

    zb = proj(COL_B, 4 * B_WIDTH)
    if prompt:
        bw = B_WIDTH
        u = zb[:, bw:2 * bw] * zb[:, 2 * bw:3 * bw]
        prev = carry_ref[...]
        u2, u1 = _conv_taps(u, prev[6:7], prev[7:8])
        cw = cw_ref[...]
        conv = u2 * cw[0:1] + u1 * cw[1:2] + u * cw[2:3]
        yb_ref[...] = (zb[:, :bw] * conv * _silu(zb[:, 3 * bw:])).astype(BF16)
        carry_ref[...] = u[u.shape[0] - 8:]
        tail_ref[...] = u[u.shape[0] - 8:]
    else:
        gate_ref[:, G_B:G_B + 4 * B_WIDTH] = zb

    zc = proj(COL_AK, A_WIDTH)
    ka_ref[...] = zc.astype(BF16)
    akv_ref[:, :A_WIDTH] = zc
    zc = proj(COL_AV, A_WIDTH)
    va_ref[...] = zc.astype(BF16)
    akv_ref[:, A_WIDTH:] = zc
    zc = proj(COL_AG, A_WIDTH)
    if prompt:
        acg_ref[:, :A_WIDTH] = zc
    else:
        gate_ref[:, G_AG:G_AG + A_WIDTH] = zc
    qa_ref[...] = (proj(COL_AQ, A_WIDTH) * A_QSCALE).astype(BF16)


def _inproj(x, g, w_all, layer, rope_base, rope_off, ts, keep, cw=None, kv_acc=None):
    prompt = cw is not None
    depth = w_all.shape[0]
    n, d = x.shape
    nt = n // ts
    first_keep = nt - keep // ts
    row = lambda width: pl.BlockSpec((ts, width), lambda i: (i, 0))
    sds = jax.ShapeDtypeStruct
    if prompt:
        kv_shape = (depth, C_WIDTH, n)
        slabs = depth if kv_acc is None else 1
        kv_spec = pl.BlockSpec((slabs, C_WIDTH, ts), lambda i: (layer, 0, i))
        gate_w = A_WIDTH + C_WIDTH
    else:
        kv_shape, kv_spec = (n, C_WIDTH), row(C_WIDTH)
        gate_w = GATE_W
    akv_spec = pl.BlockSpec((ts, 2 * A_WIDTH), lambda i: (jnp.maximum(i - first_keep, 0), 0))
    out_shape = [sds((n, gate_w), F32), sds((keep, 2 * A_WIDTH), F32),
                 sds((n, A_WIDTH), BF16), sds((n, A_WIDTH), BF16), sds((n, A_WIDTH), BF16),
                 sds((n, C_WIDTH), BF16), sds((n, C_WIDTH), BF16), sds((n, C_WIDTH), BF16),
                 sds(kv_shape, F32), sds(kv_shape, F32)]
    out_specs = [row(gate_w), akv_spec, row(A_WIDTH), row(A_WIDTH), row(A_WIDTH),
                 row(C_WIDTH), row(C_WIDTH), row(C_WIDTH), kv_spec, kv_spec]
    in_specs = [row(d), _resident((1, d)),
                _layer_weights(w_all, layer),
                pl.BlockSpec((1, 2, LANES), lambda i: (i, 0, 0)), _resident(rope_off.shape)]
    args = [x, g, w_all, rope_base, rope_off]
    scratch = []
    aliases = {}
    if prompt:
        in_specs.append(_resident(cw.shape))
        args.append(cw)
        if kv_acc is not None:
            aliases = {len(args) + j: 8 + j for j in range(len(kv_acc))}
            in_specs += [pl.BlockSpec(memory_space=pl.ANY)] * len(kv_acc)
            args += list(kv_acc)
        out_shape += [sds((n, B_WIDTH), BF16), sds((8, B_WIDTH), F32)]
        out_specs += [row(B_WIDTH), pl.BlockSpec((8, B_WIDTH), lambda i: (0, 0))]
        scratch = [pltpu.VMEM((8, B_WIDTH), F32)]
    return pl.pallas_call(
        functools.partial(_inproj_kernel, prompt=prompt, n_aliased=len(aliases),
                          zero_rest=prompt and kv_acc is None and depth > 1),
        grid=(nt,),
        in_specs=in_specs,
        out_specs=tuple(out_specs),
        out_shape=tuple(out_shape),
        scratch_shapes=scratch,
        input_output_aliases=aliases,
        compiler_params=pltpu.CompilerParams(dimension_semantics=("arbitrary",),
                                             vmem_limit_bytes=VMEM_LIMIT),
        name="inproj",
    )(*args)


def _rope_angle_tables(tile_base, row_off):
    half = ROPE_DIMS // 2
    inv_freq = ROPE_THETA ** (-jnp.arange(half, dtype=F32) * (2.0 / ROPE_DIMS))
    lane_freq = inv_freq[(np.arange(LANES) % C_QK_DIM) % half][None, :]
    a = tile_base.astype(F32)[:, None] * lane_freq
    b = row_off.astype(F32)[:, None] * lane_freq
    return jnp.stack([jnp.cos(a), jnp.sin(a)], axis=1), jnp.stack([jnp.cos(b), jnp.sin(b)], axis=0)


def _bias_kernel(u_ref, b_ref):
    v = pl.program_id(0)
    qc = lax.broadcasted_iota(jnp.int32, (TQ_A, WIN_A), 0) // CHUNK
    kc = lax.broadcasted_iota(jnp.int32, (TQ_A, WIN_A), 1) // CHUNK
    top = qc + jnp.minimum(v, 2) * (TQ_A // CHUNK)
    valid = ((kc <= top) & (kc >= top - A_PREV_CHUNKS)) | (v == N_BIAS - 1)
    for h in range(A_HEADS):
        u = jnp.broadcast_to(u_ref[0, h], (TQ_A, ROLL_A))
        t = pltpu.roll(u, 0, 1, stride=1, stride_axis=0)[:, ROLL_A - WIN_A:]
        b_ref[0, h] = jnp.where(valid, t * LOG2E, NEG)


def _bias_tables(rel_bias_l):
    tab = rel_bias_l.astype(F32)
    rows = []
    for off in (0, TQ_A, A_WIN, A_WIN):
        n_hi = off + (ROLL_A - WIN_A) - REL_CLIP
        n_lo = ROLL_A - n_hi - (2 * REL_CLIP + 1)
        rows.append(jnp.concatenate([jnp.broadcast_to(tab[:, -1:], (A_HEADS, n_hi)), tab[:, ::-1],
                                     jnp.broadcast_to(tab[:, :1], (A_HEADS, n_lo))], axis=1))
    u = jnp.stack(rows)[:, :, None, :]
    return pl.pallas_call(
        _bias_kernel,
        grid=(N_BIAS,),
        in_specs=[pl.BlockSpec((1, A_HEADS, 1, ROLL_A), lambda v: (v, 0, 0, 0))],
        out_specs=pl.BlockSpec((1, A_HEADS, TQ_A, WIN_A), lambda v: (v, 0, 0, 0)),
        out_shape=jax.ShapeDtypeStruct((N_BIAS, A_HEADS, TQ_A, WIN_A), F32),
        compiler_params=_cparams(),
        name="bias",
    )(u)


def _band_kernel(q_ref, k0_ref, k1_ref, k2_ref, v0_ref, v1_ref, v2_ref, b_ref, ag_ref, o_ref, s_ref):
    i = pl.program_id(0)
    n_pair = A_WIDTH // LANES
    lo = lax.broadcasted_iota(jnp.int32, (TQ_A, LANES), 1) < HALF
    lo_w = lax.broadcasted_iota(jnp.int32, (WIN_A, LANES), 1) < HALF

    @pl.when(i == 0)
    def _():
        s_ref[1] = jnp.zeros(s_ref.shape[1:], F32)

    def step(cur):
        for p in range(n_pair):
            cols = slice(p * LANES, (p + 1) * LANES)
            q2 = q_ref[:, cols]
            qs = jnp.concatenate([jnp.where(lo, q2, jnp.zeros_like(q2)),
                                  jnp.where(lo, jnp.zeros_like(q2), q2)], axis=0)
            kw = jnp.concatenate([r[:, cols] for r in (k0_ref, k1_ref, k2_ref)], axis=0)
            s_ref[cur, 2 * p * TQ_A:(2 * p + 2) * TQ_A, :] = _nt_dot(qs, kw)
            vw = jnp.concatenate([r[:, cols] for r in (v0_ref, v1_ref, v2_ref)], axis=0)
            outs = []
            for half in range(2):
                h = 2 * p + half
                sel_w = lo_w if half == 0 else jnp.logical_not(lo_w)
                v_aug = jnp.where(sel_w, vw, jnp.ones_like(vw))
                s = s_ref[1 - cur, h * TQ_A:(h + 1) * TQ_A, :] + b_ref[0, h]
                m = jnp.max(s, axis=-1, keepdims=True)
                e = jnp.exp2(s - m).astype(BF16)
                outs.append(jnp.dot(e, v_aug, preferred_element_type=F32))
            o_ref[:, cols] = (_normalize_pair(outs[0], outs[1], lo) * _silu(ag_ref[:, cols])).astype(BF16)

    parity = jnp.bitwise_and(i, 1)
    pl.when(parity == 0)(functools.partial(step, 0))
    pl.when(parity == 1)(functools.partial(step, 1))


def _band_attention(qa, ka, va, bias, acg):
    s = qa.shape[0]
    nq = s // TQ_A
    q_tile = lambda i: jnp.minimum(i, nq - 1)
    o_tile = lambda i: jnp.maximum(i - 1, 0)
    first = lambda tile: jnp.maximum(tile - A_WIN // TQ_A, 0)
    k_spec = lambda j: pl.BlockSpec((TQ_A, A_WIDTH), lambda i: (first(q_tile(i)) + j, 0))
    v_spec = lambda j: pl.BlockSpec((TQ_A, A_WIDTH), lambda i: (first(o_tile(i)) + j, 0))
    return pl.pallas_call(
        _band_kernel,
        grid=(nq + 1,),
        in_specs=[pl.BlockSpec((TQ_A, A_WIDTH), lambda i: (q_tile(i), 0)),
                  k_spec(0), k_spec(1), k_spec(2), v_spec(0), v_spec(1), v_spec(2),
                  pl.BlockSpec((1, A_HEADS, TQ_A, WIN_A),
                               lambda i: (jnp.minimum(o_tile(i), 2), 0, 0, 0)),
                  pl.BlockSpec((TQ_A, A_WIDTH), lambda i: (o_tile(i), 0))],
        out_specs=pl.BlockSpec((TQ_A, A_WIDTH), lambda i: (o_tile(i), 0)),
        out_shape=jax.ShapeDtypeStruct((s, A_WIDTH), BF16),
        scratch_shapes=[pltpu.VMEM((2, A_HEADS * TQ_A, WIN_A), F32)],
        compiler_params=pltpu.CompilerParams(dimension_semantics=("arbitrary",),
                                             vmem_limit_bytes=VMEM_LIMIT),
        name="band_attn",
    )(qa, ka, ka, ka, va, va, va, bias, acg)


def _diff_lambda(lam_ref, lam_init):
    a = jnp.sum(lam_ref[0:1, :] * lam_ref[1:2, :], axis=-1, keepdims=True)
    b = jnp.sum(lam_ref[2:3, :] * lam_ref[3:4, :], axis=-1, keepdims=True)
    return jnp.exp(a) - jnp.exp(b) + lam_init


def _subln(d, g2, lam_init):
    lo = lax.broadcasted_iota(jnp.int32, d.shape, 1) < HALF
    sq = d * d
    s_lo = jnp.sum(jnp.where(lo, sq, 0.0), axis=-1, keepdims=True)
    s_hi = jnp.sum(jnp.where(lo, 0.0, sq), axis=-1, keepdims=True)
    ms = jnp.where(lo, s_lo, s_hi) * (1.0 / C_V_DIM)
    return d * lax.rsqrt(ms + SUBLN_EPS) * g2 * (1.0 - lam_init)


def _diff_kernel(lam_ref, g_ref, q_ref, k_ref, v_ref, cg_ref, o_ref, qm_ref, s_ref, m_ref, acc_ref,
                 *, lam_init):
    i = pl.program_id(0)
    t = T_C
    n_grp = C_HEADS // 2
    per_grp = C_MAPS // n_grp
    q = q_ref[...]
    lane_grp = lax.broadcasted_iota(jnp.int32, (t, C_WIDTH), 1) // C_QK_DIM
    lo = lax.broadcasted_iota(jnp.int32, (t, LANES), 1) < HALF

    def mask_queries(g):
        for loc in range(per_grp):
            hm = g * per_grp + loc
            qm_ref[g, loc * t:(loc + 1) * t, :] = jnp.where(lane_grp == hm, q, jnp.zeros_like(q))

    def k_tile(j):
        return k_ref[pl.ds(pl.multiple_of(j * t, t), t), :]

    def v_tile(j):
        return v_ref[pl.ds(pl.multiple_of(j * t, t), t), :]

    def scores(g, k_t):
        s_ref[g] = _nt_dot(qm_ref[g], k_t)

    def softmax_pv(g, v_t, mask):
        v2 = v_t[:, g * LANES:(g + 1) * LANES]
        for half in range(2):
            h = 2 * g + half
            sel = lo if half == 0 else jnp.logical_not(lo)
            v_aug = jnp.where(sel, v2, jnp.ones_like(v2))
            ps, alphas = [], []
            for mp in range(2):
                hm = 2 * h + mp
                loc = hm - g * per_grp
                s = s_ref[g, loc * t:(loc + 1) * t, :]
                if mask is not None:
                    s = jnp.where(mask, s, -jnp.inf)
                m_old = m_ref[hm]
                m_new = jnp.maximum(m_old, jnp.max(s, axis=-1, keepdims=True))
                alphas.append(jnp.exp2(m_old - m_new))
                ps.append(jnp.exp2(s - jnp.tile(m_new, (1, t // LANES))).astype(BF16))
                m_ref[hm] = m_new
            pv = jnp.dot(jnp.concatenate(ps, axis=0), v_aug, preferred_element_type=F32)
            for mp in range(2):
                hm = 2 * h + mp
                acc_ref[hm] = alphas[mp] * acc_ref[hm] + pv[mp * t:(mp + 1) * t]

    def finalize(p_, lam):
        r = [_normalize_pair(acc_ref[4 * p_ + mp], acc_ref[4 * p_ + 2 + mp], lo) for mp in range(2)]
        pair = r[0] - lam * r[1]
        tile = slice(p_ * LANES, (p_ + 1) * LANES)
        o_ref[:, tile] = (_subln(pair, g_ref[...], lam_init) * _silu(cg_ref[:, tile])).astype(BF16)

    mask_queries(0)
    scores(0, k_tile(0))
    mask_queries(1)
    m_ref[...] = jnp.full(m_ref.shape, -jnp.inf, F32)
    acc_ref[...] = jnp.zeros(acc_ref.shape, F32)

    def full_tile(j):
        v_t = v_tile(j)
        scores(1, k_tile(j))
        softmax_pv(0, v_t, None)
        scores(0, k_tile(j + 1))
        softmax_pv(1, v_t, None)

    odd = jnp.bitwise_and(i, 1)

    @pl.when(odd == 1)
    def _():
        full_tile(0)

    def body(jj, carry):
        j = odd + 2 * jj
        full_tile(j)
        full_tile(j + 1)
        return carry

    lax.fori_loop(0, lax.shift_right_logical(i, 1), body, 0)
    qchunk = lax.broadcasted_iota(jnp.int32, (t, t), 0) // CHUNK
    kchunk = lax.broadcasted_iota(jnp.int32, (t, t), 1) // CHUNK
    mask = kchunk <= qchunk
    v_t = v_tile(i)
    lam = _diff_lambda(lam_ref, lam_init)
    scores(1, k_tile(i))
    softmax_pv(0, v_t, mask)
    finalize(0, lam)
    softmax_pv(1, v_t, mask)
    finalize(1, lam)


def _diff_attention(lam4, g2, qc, kc, vc, acg, lam_init):
    s = qc.shape[0]
    t = T_C
    n_grp = C_HEADS // 2
    return pl.pallas_call(
        functools.partial(_diff_kernel, lam_init=lam_init),
        grid=(s // t,),
        in_specs=[_resident((4, C_QK_DIM)), _resident((1, LANES)),
                  pl.BlockSpec((t, C_WIDTH), lambda i: (i, 0)),
                  _resident((s, C_WIDTH)), _resident((s, C_WIDTH)),
                  pl.BlockSpec((t, C_WIDTH), lambda i: (i, A_WIDTH // C_WIDTH))],
        out_specs=pl.BlockSpec((t, C_WIDTH), lambda i: (i, 0)),
        out_shape=jax.ShapeDtypeStruct((s, C_WIDTH), BF16),
        scratch_shapes=[pltpu.VMEM((n_grp, C_MAPS // n_grp * t, C_WIDTH), BF16),
                        pltpu.VMEM((n_grp, C_MAPS // n_grp * t, t), F32),
                        pltpu.VMEM((C_MAPS, t, LANES), F32),
                        pltpu.VMEM((C_MAPS, t, LANES), F32)],
        compiler_params=_cparams(),
        name="diff_attn",
    )(lam4, g2, qc, kc, vc, acg)


def _silu(t):
    return t * jax.nn.sigmoid(t)


def _conv_input(gate):
    b = gate[:, G_B:G_B + 4 * B_WIDTH]
    return b[:, B_WIDTH:2 * B_WIDTH] * b[:, 2 * B_WIDTH:3 * B_WIDTH]


def _conv_taps(u, prev0, prev1):
    row = lax.broadcasted_iota(jnp.int32, u.shape, 0)
    u1 = jnp.where(row == 0, prev1, pltpu.roll(u, 1, 0))
    u2 = jnp.where(row == 0, prev0, jnp.where(row == 1, prev1, pltpu.roll(u, 2, 0)))
    return u2, u1


def _gated_mix(gate, oa, oc, u2, u1, u, cw):
    bw = B_WIDTH
    conv = u2 * cw[0:1] + u1 * cw[1:2] + u * cw[2:3]
    ob = gate[:, G_B:G_B + bw] * conv
    y = jnp.concatenate([oa * _silu(gate[:, G_AG:G_AG + A_WIDTH]),
                         ob * _silu(gate[:, G_B + 3 * bw:G_B + 4 * bw]),
                         oc * _silu(gate[:, G_CG:G_CG + C_WIDTH])], axis=-1)
    return y.astype(BF16)


def _final_norm(x, fg):
    ms = jnp.mean(x * x, axis=-1, keepdims=True)
    return x * lax.rsqrt(ms + NORM_EPS) * fg


def _merge_kernel(*refs, final):
    x_ref, *y_refs, w_ref, fg_ref, xo_ref = refs
    y = jnp.concatenate([r[...] for r in y_refs], axis=-1) if len(y_refs) > 1 else y_refs[0][...]
    out = x_ref[...] + jnp.dot(y, w_ref[0], preferred_element_type=F32)
    xo_ref[...] = _final_norm(out, fg_ref[...]) if final else out


def _merge(x, ys, w_all, layer, fg, ts, final):
    n, d = x.shape
    row = lambda width: pl.BlockSpec((ts, width), lambda i: (i, 0))
    return pl.pallas_call(
        functools.partial(_merge_kernel, final=final),
        grid=(n // ts,),
        in_specs=[row(d)] + [row(y.shape[1]) for y in ys] + [_layer_weights(w_all, layer), _resident((1, d))],
        out_specs=row(d),
        out_shape=jax.ShapeDtypeStruct((n, d), F32),
        compiler_params=_cparams(),
        name="merge",
    )(x, *ys, w_all, fg)


def _stack_heads(q, n_groups, width):
    grp = lax.broadcasted_iota(jnp.int32, q.shape, 1) // width
    return jnp.concatenate([jnp.where(grp == g, q, jnp.zeros_like(q)) for g in range(n_groups)], axis=0)


def _cached_attention(qs, kt_c, vt_c, k_n, v_n, bias_c, bias_n):
    s_c = jnp.dot(qs, kt_c, preferred_element_type=F32)
    s_n = _nt_dot(qs, k_n)
    if bias_c is not None:
        s_c, s_n = s_c + bias_c, s_n + bias_n
    m = jnp.maximum(jnp.max(s_c, axis=-1, keepdims=True), jnp.max(s_n, axis=-1, keepdims=True))
    e_c = jnp.exp2(s_c - m)
    e_n = jnp.exp2(s_n - m)
    l = jnp.sum(e_c, axis=-1, keepdims=True) + jnp.sum(e_n, axis=-1, keepdims=True)
    o = _nt_dot(e_c.astype(BF16), vt_c) + jnp.dot(e_n.astype(BF16), v_n, preferred_element_type=F32)
    return o / l


def _sample_kernel(lam_ref, g_ref, bc_ref, bn_ref,
                   qa_ref, ka_ref, va_ref, qc_ref, kc_ref, vc_ref, gate_ref,
                   cak_ref, cav_ref, cck_ref, ccv_ref, conv_ref, cw_ref,
                   y_ref, ul_ref, *, lam_init):
    t = qa_ref.shape[0]
    qs = _stack_heads(qa_ref[...], A_HEADS, HEAD_DIM)
    of = _cached_attention(qs, cak_ref[0, 0].astype(BF16), cav_ref[0, 0].astype(BF16),
                           ka_ref[...], va_ref[...], bc_ref[...], bn_ref[...])
    grp = lax.broadcasted_iota(jnp.int32, (t, A_WIDTH), 1) // HEAD_DIM
    oa = jnp.zeros((t, A_WIDTH), F32)
    for h in range(A_HEADS):
        oa = jnp.where(grp == h, of[h * t:(h + 1) * t], oa)
    qs = _stack_heads(qc_ref[...], C_MAPS, C_QK_DIM)
    of = _cached_attention(qs, cck_ref[0, 0].astype(BF16), ccv_ref[0, 0].astype(BF16),
                           kc_ref[...], vc_ref[...], None, None)
    lam = _diff_lambda(lam_ref, lam_init)
    grp = lax.broadcasted_iota(jnp.int32, (t, C_WIDTH), 1) // C_V_DIM
    d = jnp.zeros((t, C_WIDTH), F32)
    for h in range(C_HEADS):
        dh = of[2 * h * t:(2 * h + 1) * t] - lam * of[(2 * h + 1) * t:(2 * h + 2) * t]
        d = jnp.where(grp == h, dh, d)
    oc = jnp.concatenate([_subln(d[:, p * LANES:(p + 1) * LANES], g_ref[...], lam_init)
                          for p in range(C_WIDTH // LANES)], axis=-1)
    gate = gate_ref[...]
    u = _conv_input(gate)
    prev = conv_ref[0]
    u2, u1 = _conv_taps(u, prev[0:1], prev[1:2])
    y_ref[...] = _gated_mix(gate, oa, oc, u2, u1, u, cw_ref[...])
    ul_ref[0] = u[t - 8:]


def _sample_step(layer, lam4, g2, bias_c, bias_n, gate, qa, ka, va, qc, kc, vc,
                 cak_t, cav_t, cck_t, ccv_t, conv, cw, t, lam_init):
    n = gate.shape[0]
    d_mix = A_WIDTH + B_WIDTH + C_WIDTH
    nb = n // t
    row = lambda width: pl.BlockSpec((t, width), lambda b: (b, 0))
    cache = lambda a: pl.BlockSpec((1, 1) + a.shape[2:], lambda b: (layer, b, 0, 0))
    return pl.pallas_call(
        functools.partial(_sample_kernel, lam_init=lam_init),
        grid=(nb,),
        in_specs=[_resident((4, C_QK_DIM)), _resident((1, LANES)),
                  _resident(bias_c.shape), _resident(bias_n.shape),
                  row(A_WIDTH), row(A_WIDTH), row(A_WIDTH),
                  row(C_WIDTH), row(C_WIDTH), row(C_WIDTH), row(GATE_W),
                  cache(cak_t), cache(cav_t), cache(cck_t), cache(ccv_t),
                  pl.BlockSpec((1,) + conv.shape[1:], lambda b: (b, 0, 0)),
                  _resident(cw.shape)],
        out_specs=(row(d_mix), pl.BlockSpec((1, 8, B_WIDTH), lambda b: (b, 0, 0))),
        out_shape=(jax.ShapeDtypeStruct((n, d_mix), BF16),
                   jax.ShapeDtypeStruct((nb, 8, B_WIDTH), F32)),
        compiler_params=_cparams(),
        name="sample_step",
    )(lam4, g2, bias_c, bias_n, qa, ka, va, qc, kc, vc, gate,
      cak_t, cav_t, cck_t, ccv_t, conv, cw)


def _feature_major(cache):
    nd = cache.ndim
    c = jnp.transpose(cache, (0, 1) + tuple(range(3, nd)) + (2,))
    return c.reshape(c.shape[0], c.shape[1], -1, c.shape[-1])


def kernel(x_prompt, x_sample, cache_a_k, cache_a_v, state_conv, cache_c_k, cache_c_v, norm_g, w_in, w_out, rel_bias, conv_w, lam_q1, lam_k1, lam_q2, lam_k2, subln_g, final_g):
    depth = norm_g.shape[0]
    batch, seq, d_model = x_prompt.shape
    nb, t, _ = x_sample.shape
    past = cache_c_k.shape[2]
    win = cache_a_k.shape[2]
    ts = TS_PROMPT
    assert batch == 1 and win == A_WIN and seq % T_C == 0 and seq % ts == 0 and seq >= WIN_A
    assert t % 16 == 0 and past % LANES == 0
    keep = min(A_WIN, seq)
    assert keep % ts == 0

    xp = x_prompt.reshape(seq, d_model)
    xs = x_sample.reshape(nb * t, d_model)
    rope_p = _rope_angle_tables(jnp.arange(0, seq, ts), jnp.arange(ts))
    rope_s = _rope_angle_tables(jnp.full((1,), past), jnp.tile(jnp.arange(t), nb))
    w_in_b = w_in.astype(BF16)
    w_out_b = w_out.astype(BF16)
    fg = final_g.reshape(1, d_model).astype(F32)
    cak_t, cav_t = _feature_major(cache_a_k), _feature_major(cache_a_v)
    cck_t, ccv_t = _feature_major(cache_c_k), _feature_major(cache_c_v)

    outs = {k: [] for k in ("pak", "pav", "pcv", "sak", "sav", "scv", "sck", "scc")}
    kv_acc = None
    for l in range(depth):
        final = l == depth - 1
        lam_init = 0.8 - 0.6 * math.exp(-0.3 * l)
        g = norm_g[l].reshape(1, d_model).astype(F32)
        cw = conv_w[l].astype(F32)
        lam4 = jnp.stack([lam_q1[l], lam_k1[l], lam_q2[l], lam_k2[l]]).astype(F32)
        g2 = jnp.tile(subln_g[l].astype(F32), LANES // C_V_DIM).reshape(1, LANES)
        bias = _bias_tables(rel_bias[l])

        acg, akv, qa, ka, va, qc, kc, vc, ck_all, cv_all, yb, tail = _inproj(
            xp, g, w_in_b, l, *rope_p, ts, keep, cw, kv_acc)
        kv_acc = (ck_all, cv_all)
        ya = _band_attention(qa, ka, va, bias, acg)
        yc = _diff_attention(lam4, g2, qc, kc, vc, acg, lam_init)
        xp = _merge(xp, (ya, yb, yc), w_out_b, l, fg, TS_MERGE, final)
        outs["pak"].append(akv[:, :A_WIDTH].reshape(1, keep, A_HEADS, HEAD_DIM))
        outs["pav"].append(akv[:, A_WIDTH:].reshape(1, keep, A_HEADS, HEAD_DIM))
        outs["pcv"].append(tail[8 - (CONV_WIDTH - 1):][None])

        gate, akv, qa, ka, va, qc, kc, vc, ckr, cvr = _inproj(
            xs, g, w_in_b, l, *rope_s, nb * t, nb * t)
        unmasked = bias[N_BIAS - 1, :, :t, :A_WIN + t].reshape(A_HEADS * t, A_WIN + t)
        ys, ul = _sample_step(
            l, lam4, g2, unmasked[:, :A_WIN], unmasked[:, A_WIN:], gate, qa, ka, va, qc, kc, vc,
            cak_t, cav_t, cck_t, ccv_t, state_conv[l], cw, t, lam_init)
        xs = _merge(xs, (ys,), w_out_b, l, fg, nb * t, final)
        outs["sak"].append(akv[:, :A_WIDTH].reshape(nb, t, A_HEADS, HEAD_DIM))
        outs["sav"].append(akv[:, A_WIDTH:].reshape(nb, t, A_HEADS, HEAD_DIM))
        outs["scv"].append(ul[:, 8 - (CONV_WIDTH - 1):])
        outs["sck"].append(ckr.reshape(nb, t, C_HEADS, 2, C_QK_DIM))
        outs["scc"].append(cvr.reshape(nb, t, C_HEADS, C_V_DIM))

    st = lambda k: jnp.stack(outs[k])
    ck_all, cv_all = kv_acc
    pck = jnp.transpose(ck_all.reshape(depth, C_HEADS, 2, C_QK_DIM, seq), (0, 4, 1, 2, 3))[:, None]
    pcc = jnp.transpose(cv_all.reshape(depth, C_HEADS, C_V_DIM, seq), (0, 3, 1, 2))[:, None]
    return (xp.reshape(batch, seq, d_model), xs.reshape(nb, t, d_model),
            st("pak"), st("pav"), st("pcv"), pck, pcc,
            st("sak"), st("sav"), st("scv"), st("sck"), st("scc"))
```

```python
import functools
import math

import numpy as np
import jax
import jax.numpy as jnp
from jax import lax
from jax.experimental import pallas as pl
from jax.experimental.pallas import tpu as pltpu

F32 = jnp.float32
BF16 = jnp.bfloat16

CHUNK = 64
HEAD_DIM = 64
A_HEADS = 8
A_WIDTH = A_HEADS * HEAD_DIM
A_PREV_CHUNKS = 8
A_WIN = A_PREV_CHUNKS * CHUNK
REL_CLIP = 128
B_WIDTH = 256
CONV_WIDTH = 3
C_HEADS = 4
C_QK_DIM = 32
C_V_DIM = 64
C_WIDTH = C_HEADS * C_V_DIM
C_MAPS = 2 * C_HEADS
ROPE_DIMS = 8
ROPE_THETA = 500000.0
NORM_EPS = 1e-6
SUBLN_EPS = 1e-5
D_IN_PROJ = 4096
COL_AQ, COL_AK, COL_AV, COL_AG = 0, 512, 1024, 1536
COL_B = 2048
COL_CQ, COL_CK, COL_CV, COL_CG = 3072, 3328, 3584, 3840
GATE_W = A_WIDTH + 4 * B_WIDTH + C_WIDTH
G_AG, G_B, G_CG = 0, A_WIDTH, A_WIDTH + 4 * B_WIDTH

LANES = 128
HALF = LANES // 2
NEG = -1e30
LOG2E = math.log2(math.e)
A_QSCALE = HEAD_DIM ** -0.5 * LOG2E
C_QSCALE = C_QK_DIM ** -0.5 * LOG2E
VMEM_LIMIT = 48 * 1024 * 1024

TS_PROMPT = 512
TS_MERGE = 1024
TQ_A = 256
WIN_A = A_WIN + TQ_A
ROLL_A = 1024
N_BIAS = 3
T_C = 512


def _nt_dot(a, b):
    return lax.dot_general(a, b, (((1,), (1,)), ((), ())), preferred_element_type=F32)


def _normalize_pair(acc_even, acc_odd, lo):
    o = jnp.where(lo, acc_even, acc_odd)
    l = jnp.where(lo, acc_odd, acc_even)
    return o / pltpu.roll(l, HALF, 1)


def _cparams(n_axes=1):
    return pltpu.CompilerParams(dimension_semantics=("parallel",) * n_axes,
                                vmem_limit_bytes=VMEM_LIMIT)


def _resident(shape):
    nd = len(shape)
    return pl.BlockSpec(shape, lambda *_: (0,) * nd, pipeline_mode=pl.Buffered(1))


def _layer_weights(w_all, layer):
    return pl.BlockSpec((1,) + w_all.shape[1:], lambda *_: (layer, 0, 0), pipeline_mode=pl.Buffered(1))


def _rope(x, cos, sa, sb):
    return (x * cos + pltpu.roll(x, LANES - ROPE_DIMS // 2, 1) * sa
            + pltpu.roll(x, ROPE_DIMS // 2, 1) * sb)


def _rope_coefficients(base, rt_ref):
    cb, sb_ = base[0:1], base[1:2]
    ct, st = rt_ref[0], rt_ref[1]
    cos_p = cb * ct - sb_ * st
    sin_p = sb_ * ct + cb * st
    d = lax.broadcasted_iota(jnp.int32, cos_p.shape, 1) % C_QK_DIM
    half = ROPE_DIMS // 2
    cos = jnp.where(d < ROPE_DIMS, cos_p, 1.0)
    sa = jnp.where(d < half, -sin_p, 0.0)
    sb = jnp.where((d >= half) & (d < ROPE_DIMS), sin_p, 0.0)
    return cos, sa, sb


def _inproj_kernel(*refs, prompt, n_aliased, zero_rest, merged):
    x_ref, g_ref, w_ref, rb_ref, rt_ref = refs[:5]
    if prompt:
        cw_ref = refs[5]
        n_merge_in = 4 if merged else 0
        y_refs, wo_ref = refs[6:6 + n_merge_in - 1], refs[6 + n_merge_in - 1]
        outs = refs[6 + n_merge_in + n_aliased:]
        (acg_ref, akv_ref, qa_ref, ka_ref, va_ref, qc_ref, kc_ref, vc_ref, ckr_ref, cvr_ref,
         yb_ref, tail_ref) = outs[:12]
        carry_ref = outs[-1]
    else:
        gate_ref, akv_ref, qa_ref, ka_ref, va_ref, qc_ref, kc_ref, vc_ref, ckr_ref, cvr_ref = refs[5:]

    if prompt:
        @pl.when(pl.program_id(0) == 0)
        def _():
            carry_ref[...] = jnp.zeros(carry_ref.shape, F32)

    x = x_ref[...]
    if merged:
        y = jnp.concatenate([r[...] for r in y_refs], axis=-1)
        x = x + jnp.dot(y, wo_ref[0], preferred_element_type=F32)
        outs[12][...] = x
    ms = jnp.mean(x * x, axis=-1, keepdims=True)
    h = (x * lax.rsqrt(ms + NORM_EPS) * g_ref[...]).astype(BF16)
    cos, sa, sb = _rope_coefficients(rb_ref[0], rt_ref)

    def proj(lo, width):
        return jnp.dot(h, w_ref[0, :, lo:lo + width], preferred_element_type=F32)

    def put_f32(ref, t, val):
        if prompt:
            ref[0, t * LANES:(t + 1) * LANES, :] = val.T
            if zero_rest:
                ref[1:, t * LANES:(t + 1) * LANES, :] = jnp.zeros(
                    (ref.shape[0] - 1, LANES, val.shape[0]), F32)
        else:
            ref[:, t * LANES:(t + 1) * LANES] = val

    zc = proj(COL_CQ, 2 * C_WIDTH)
    for t in range(C_WIDTH // LANES):
        cq = _rope(zc[:, t * LANES:(t + 1) * LANES], cos, sa, sb)
        qc_ref[:, t * LANES:(t + 1) * LANES] = (cq * C_QSCALE).astype(BF16)
        ck = _rope(zc[:, C_WIDTH + t * LANES:C_WIDTH + (t + 1) * LANES], cos, sa, sb)
        kc_ref[:, t * LANES:(t + 1) * LANES] = ck.astype(BF16)
        put_f32(ckr_ref, t, ck)
    zc = proj(COL_CV, 2 * C_WIDTH)
    vc_ref[...] = zc[:, :C_WIDTH].astype(BF16)
    for t in range(C_WIDTH // LANES):
        put_f32(cvr_ref, t, zc[:, t * LANES:(t + 1) * LANES])
    if prompt:
        acg_ref[:, A_WIDTH:] = zc[:, C_WIDTH:]
    else:
        gate_ref[:, G_CG:G_CG + C_WIDTH] = zc[:, C_WIDTH:]

    zb = proj(COL_B, 4 * B_WIDTH)
    if prompt:
        bw = B_WIDTH
        u = zb[:, bw:2 * bw] * zb[:, 2 * bw:3 * bw]
        prev = carry_ref[...]
        u2, u1 = _conv_taps(u, prev[6:7], prev[7:8])
        cw = cw_ref[...]
        conv = u2 * cw[0:1] + u1 * cw[1:2] + u * cw[2:3]
        yb_ref[...] = (zb[:, :bw] * conv * _silu(zb[:, 3 * bw:])).astype(BF16)
        carry_ref[...] = u[u.shape[0] - 8:]
        tail_ref[...] = u[u.shape[0] - 8:]
    else:
        gate_ref[:, G_B:G_B + 4 * B_WIDTH] = zb

    zc = proj(COL_AK, A_WIDTH)
    ka_ref[...] = zc.astype(BF16)
    akv_ref[:, :A_WIDTH] = zc
    zc = proj(COL_AV, A_WIDTH)
    va_ref[...] = zc.astype(BF16)
    akv_ref[:, A_WIDTH:] = zc
    zc = proj(COL_AG, A_WIDTH)
    if prompt:
        acg_ref[:, :A_WIDTH] = zc
    else:
        gate_ref[:, G_AG:G_AG + A_WIDTH] = zc
    qa_ref[...] = (proj(COL_AQ, A_WIDTH) * A_QSCALE).astype(BF16)


def _inproj(x, g, w_all, layer, rope_base, rope_off, ts, keep, cw=None, kv_acc=None, merge_in=None):
    prompt = cw is not None
    depth = w_all.shape[0]
    n, d = x.shape
    nt = n // ts
    first_keep = nt - keep // ts
    row = lambda width: pl.BlockSpec((ts, width), lambda i: (i, 0))
    sds = jax.ShapeDtypeStruct
    if prompt:
        kv_shape = (depth, C_WIDTH, n)
        slabs = depth if kv_acc is None else 1
        kv_spec = pl.BlockSpec((slabs, C_WIDTH, ts), lambda i: (layer, 0, i))
        gate_w = A_WIDTH + C_WIDTH
    else:
        kv_shape, kv_spec = (n, C_WIDTH), row(C_WIDTH)
        gate_w = GATE_W
    akv_spec = pl.BlockSpec((ts, 2 * A_WIDTH), lambda i: (jnp.maximum(i - first_keep, 0), 0))
    out_shape = [sds((n, gate_w), F32), sds((keep, 2 * A_WIDTH), F32),
                 sds((n, A_WIDTH), BF16), sds((n, A_WIDTH), BF16), sds((n, A_WIDTH), BF16),
                 sds((n, C_WIDTH), BF16), sds((n, C_WIDTH), BF16), sds((n, C_WIDTH), BF16),
                 sds(kv_shape, F32), sds(kv_shape, F32)]
    out_specs = [row(gate_w), akv_spec, row(A_WIDTH), row(A_WIDTH), row(A_WIDTH),
                 row(C_WIDTH), row(C_WIDTH), row(C_WIDTH), kv_spec, kv_spec]
    in_specs = [row(d), _resident((1, d)),
                _layer_weights(w_all, layer),
                pl.BlockSpec((1, 2, LANES), lambda i: (i, 0, 0)), _resident(rope_off.shape)]
    args = [x, g, w_all, rope_base, rope_off]
    scratch = []
    aliases = {}
    if prompt:
        in_specs.append(_resident(cw.shape))
        args.append(cw)
        if merge_in is not None:
            ys, wo_all, prev_layer = merge_in
            in_specs += [row(y.shape[1]) for y in ys] + [_layer_weights(wo_all, prev_layer)]
            args += list(ys) + [wo_all]
        if kv_acc is not None:
            aliases = {len(args) + j: 8 + j for j in range(len(kv_acc))}
            in_specs += [pl.BlockSpec(memory_space=pl.ANY)] * len(kv_acc)
            args += list(kv_acc)
        out_shape += [sds((n, B_WIDTH), BF16), sds((8, B_WIDTH), F32)]
        out_specs += [row(B_WIDTH), pl.BlockSpec((8, B_WIDTH), lambda i: (0, 0))]
        if merge_in is not None:
            out_shape.append(sds((n, d), F32))
            out_specs.append(row(d))
        scratch = [pltpu.VMEM((8, B_WIDTH), F32)]
    return pl.pallas_call(
        functools.partial(_inproj_kernel, prompt=prompt, n_aliased=len(aliases),
                          zero_rest=prompt and kv_acc is None and depth > 1,
                          merged=merge_in is not None),
        grid=(nt,),
        in_specs=in_specs,
        out_specs=tuple(out_specs),
        out_shape=tuple(out_shape),
        scratch_shapes=scratch,
        input_output_aliases=aliases,
        compiler_params=pltpu.CompilerParams(dimension_semantics=("arbitrary",),
                                             vmem_limit_bytes=VMEM_LIMIT),
        name="inproj",
    )(*args)


def _rope_angle_tables(tile_base, row_off):
    half = ROPE_DIMS // 2
    inv_freq = ROPE_THETA ** (-jnp.arange(half, dtype=F32) * (2.0 / ROPE_DIMS))
    lane_freq = inv_freq[(np.arange(LANES) % C_QK_DIM) % half][None, :]
    a = tile_base.astype(F32)[:, None] * lane_freq
    b = row_off.astype(F32)[:, None] * lane_freq
    return jnp.stack([jnp.cos(a), jnp.sin(a)], axis=1), jnp.stack([jnp.cos(b), jnp.sin(b)], axis=0)


def _bias_kernel(u_ref, b_ref):
    v = pl.program_id(0)
    qc = lax.broadcasted_iota(jnp.int32, (TQ_A, WIN_A), 0) // CHUNK
    kc = lax.broadcasted_iota(jnp.int32, (TQ_A, WIN_A), 1) // CHUNK
    top = qc + v * (TQ_A // CHUNK)
    valid = (kc <= top) & (kc >= top - A_PREV_CHUNKS)
    for h in range(A_HEADS):
        u = jnp.broadcast_to(u_ref[0, h], (TQ_A, ROLL_A))
        t = pltpu.roll(u, 0, 1, stride=1, stride_axis=0)[:, ROLL_A - WIN_A:]
        b_ref[0, h] = jnp.where(valid, t * LOG2E, NEG)


def _bias_tables(rel_bias_l):
    tab = rel_bias_l.astype(F32)
    rows = []
    for off in (0, TQ_A, A_WIN):
        n_hi = off + (ROLL_A - WIN_A) - REL_CLIP
        n_lo = ROLL_A - n_hi - (2 * REL_CLIP + 1)
        rows.append(jnp.concatenate([jnp.broadcast_to(tab[:, -1:], (A_HEADS, n_hi)), tab[:, ::-1],
                                     jnp.broadcast_to(tab[:, :1], (A_HEADS, n_lo))], axis=1))
    u = jnp.stack(rows)[:, :, None, :]
    return pl.pallas_call(
        _bias_kernel,
        grid=(N_BIAS,),
        in_specs=[pl.BlockSpec((1, A_HEADS, 1, ROLL_A), lambda v: (v, 0, 0, 0))],
        out_specs=pl.BlockSpec((1, A_HEADS, TQ_A, WIN_A), lambda v: (v, 0, 0, 0)),
        out_shape=jax.ShapeDtypeStruct((N_BIAS, A_HEADS, TQ_A, WIN_A), F32),
        compiler_params=_cparams(),
        name="bias",
    )(u)


def _band_kernel(q_ref, k0_ref, k1_ref, k2_ref, v0_ref, v1_ref, v2_ref, b_ref, ag_ref, o_ref, s_ref):
    i = pl.program_id(0)
    n_pair = A_WIDTH // LANES
    lo = lax.broadcasted_iota(jnp.int32, (TQ_A, LANES), 1) < HALF
    lo_w = lax.broadcasted_iota(jnp.int32, (WIN_A, LANES), 1) < HALF

    @pl.when(i == 0)
    def _():
        s_ref[1] = jnp.zeros(s_ref.shape[1:], F32)

    def step(cur):
        for p in range(n_pair):
            cols = slice(p * LANES, (p + 1) * LANES)
            q2 = q_ref[:, cols]
            qs = jnp.concatenate([jnp.where(lo, q2, jnp.zeros_like(q2)),
                                  jnp.where(lo, jnp.zeros_like(q2), q2)], axis=0)
            kw = jnp.concatenate([r[:, cols] for r in (k0_ref, k1_ref, k2_ref)], axis=0)
            s_ref[cur, 2 * p * TQ_A:(2 * p + 2) * TQ_A, :] = _nt_dot(qs, kw)
            vw = jnp.concatenate([r[:, cols] for r in (v0_ref, v1_ref, v2_ref)], axis=0)
            outs = []
            for half in range(2):
                h = 2 * p + half
                sel_w = lo_w if half == 0 else jnp.logical_not(lo_w)
                v_aug = jnp.where(sel_w, vw, jnp.ones_like(vw))
                s = s_ref[1 - cur, h * TQ_A:(h + 1) * TQ_A, :] + b_ref[0, h]
                m = jnp.max(s, axis=-1, keepdims=True)
                e = jnp.exp2(s - m).astype(BF16)
                outs.append(jnp.dot(e, v_aug, preferred_element_type=F32))
            o_ref[:, cols] = (_normalize_pair(outs[0], outs[1], lo) * _silu(ag_ref[:, cols])).astype(BF16)

    parity = jnp.bitwise_and(i, 1)
    pl.when(parity == 0)(functools.partial(step, 0))
    pl.when(parity == 1)(functools.partial(step, 1))


def _band_attention(qa, ka, va, bias, acg):
    s = qa.shape[0]
    nq = s // TQ_A
    q_tile = lambda i: jnp.minimum(i, nq - 1)
    o_tile = lambda i: jnp.maximum(i - 1, 0)
    first = lambda tile: jnp.maximum(tile - A_WIN // TQ_A, 0)
    k_spec = lambda j: pl.BlockSpec((TQ_A, A_WIDTH), lambda i: (first(q_tile(i)) + j, 0))
    v_spec = lambda j: pl.BlockSpec((TQ_A, A_WIDTH), lambda i: (first(o_tile(i)) + j, 0))
    return pl.pallas_call(
        _band_kernel,
        grid=(nq + 1,),
        in_specs=[pl.BlockSpec((TQ_A, A_WIDTH), lambda i: (q_tile(i), 0)),
                  k_spec(0), k_spec(1), k_spec(2), v_spec(0), v_spec(1), v_spec(2),
                  pl.BlockSpec((1, A_HEADS, TQ_A, WIN_A),
                               lambda i: (jnp.minimum(o_tile(i), N_BIAS - 1), 0, 0, 0)),
                  pl.BlockSpec((TQ_A, A_WIDTH), lambda i: (o_tile(i), 0))],
        out_specs=pl.BlockSpec((TQ_A, A_WIDTH), lambda i: (o_tile(i), 0)),
        out_shape=jax.ShapeDtypeStruct((s, A_WIDTH), BF16),
        scratch_shapes=[pltpu.VMEM((2, A_HEADS * TQ_A, WIN_A), F32)],
        compiler_params=pltpu.CompilerParams(dimension_semantics=("arbitrary",),
                                             vmem_limit_bytes=VMEM_LIMIT),
        name="band_attn",
    )(qa, ka, ka, ka, va, va, va, bias, acg)


def _diff_lambda(lam_ref, lam_init):
    a = jnp.sum(lam_ref[0:1, :] * lam_ref[1:2, :], axis=-1, keepdims=True)
    b = jnp.sum(lam_ref[2:3, :] * lam_ref[3:4, :], axis=-1, keepdims=True)
    return jnp.exp(a) - jnp.exp(b) + lam_init


def _subln(d, g2, lam_init):
    lo = lax.broadcasted_iota(jnp.int32, d.shape, 1) < HALF
    sq = d * d
    s_lo = jnp.sum(jnp.where(lo, sq, 0.0), axis=-1, keepdims=True)
    s_hi = jnp.sum(jnp.where(lo, 0.0, sq), axis=-1, keepdims=True)
    ms = jnp.where(lo, s_lo, s_hi) * (1.0 / C_V_DIM)
    return d * lax.rsqrt(ms + SUBLN_EPS) * g2 * (1.0 - lam_init)


def _diff_kernel(lam_ref, g_ref, q_ref, k_ref, v_ref, cg_ref, o_ref, qm_ref, s_ref, m_ref, acc_ref,
                 *, lam_init):
    i = pl.program_id(0)
    t = T_C
    n_grp = C_HEADS // 2
    per_grp = C_MAPS // n_grp
    q = q_ref[...]
    lane_grp = lax.broadcasted_iota(jnp.int32, (t, C_WIDTH), 1) // C_QK_DIM
    lo = lax.broadcasted_iota(jnp.int32, (t, LANES), 1) < HALF

    def mask_queries(g):
        for loc in range(per_grp):
            hm = g * per_grp + loc
            qm_ref[g, loc * t:(loc + 1) * t, :] = jnp.where(lane_grp == hm, q, jnp.zeros_like(q))

    def k_tile(j):
        return k_ref[pl.ds(pl.multiple_of(j * t, t), t), :]

    def v_tile(j):
        return v_ref[pl.ds(pl.multiple_of(j * t, t), t), :]

    def scores(g, k_t):
        s_ref[g] = _nt_dot(qm_ref[g], k_t)

    def softmax_pv(g, v_t, mask):
        v2 = v_t[:, g * LANES:(g + 1) * LANES]
        for half in range(2):
            h = 2 * g + half
            sel = lo if half == 0 else jnp.logical_not(lo)
            v_aug = jnp.where(sel, v2, jnp.ones_like(v2))
            ps, alphas = [], []
            for mp in range(2):
                hm = 2 * h + mp
                loc = hm - g * per_grp
                s = s_ref[g, loc * t:(loc + 1) * t, :]
                if mask is not None:
                    s = jnp.where(mask, s, -jnp.inf)
                m_old = m_ref[hm]
                m_new = jnp.maximum(m_old, jnp.max(s, axis=-1, keepdims=True))
                alphas.append(jnp.exp2(m_old - m_new))
                ps.append(jnp.exp2(s - jnp.tile(m_new, (1, t // LANES))).astype(BF16))
                m_ref[hm] = m_new
            pv = jnp.dot(jnp.concatenate(ps, axis=0), v_aug, preferred_element_type=F32)
            for mp in range(2):
                hm = 2 * h + mp
                acc_ref[hm] = alphas[mp] * acc_ref[hm] + pv[mp * t:(mp + 1) * t]

    def finalize(p_, lam):
        r = [_normalize_pair(acc_ref[4 * p_ + mp], acc_ref[4 * p_ + 2 + mp], lo) for mp in range(2)]
        pair = r[0] - lam * r[1]
        tile = slice(p_ * LANES, (p_ + 1) * LANES)
        o_ref[:, tile] = (_subln(pair, g_ref[...], lam_init) * _silu(cg_ref[:, tile])).astype(BF16)

    mask_queries(0)
    scores(0, k_tile(0))
    mask_queries(1)
    m_ref[...] = jnp.full(m_ref.shape, -jnp.inf, F32)
    acc_ref[...] = jnp.zeros(acc_ref.shape, F32)

    def full_tile(j):
        v_t = v_tile(j)
        scores(1, k_tile(j))
        softmax_pv(0, v_t, None)
        scores(0, k_tile(j + 1))
        softmax_pv(1, v_t, None)

    odd = jnp.bitwise_and(i, 1)

    @pl.when(odd == 1)
    def _():
        full_tile(0)

    def body(jj, carry):
        j = odd + 2 * jj
        full_tile(j)
        full_tile(j + 1)
        return carry

    lax.fori_loop(0, lax.shift_right_logical(i, 1), body, 0)
    qchunk = lax.broadcasted_iota(jnp.int32, (t, t), 0) // CHUNK
    kchunk = lax.broadcasted_iota(jnp.int32, (t, t), 1) // CHUNK
    mask = kchunk <= qchunk
    v_t = v_tile(i)
    lam = _diff_lambda(lam_ref, lam_init)
    scores(1, k_tile(i))
    softmax_pv(0, v_t, mask)
    finalize(0, lam)
    softmax_pv(1, v_t, mask)
    finalize(1, lam)


def _diff_attention(lam4, g2, qc, kc, vc, acg, lam_init):
    s = qc.shape[0]
    t = T_C
    n_grp = C_HEADS // 2
    return pl.pallas_call(
        functools.partial(_diff_kernel, lam_init=lam_init),
        grid=(s // t,),
        in_specs=[_resident((4, C_QK_DIM)), _resident((1, LANES)),
                  pl.BlockSpec((t, C_WIDTH), lambda i: (i, 0)),
                  _resident((s, C_WIDTH)), _resident((s, C_WIDTH)),
                  pl.BlockSpec((t, C_WIDTH), lambda i: (i, A_WIDTH // C_WIDTH))],
        out_specs=pl.BlockSpec((t, C_WIDTH), lambda i: (i, 0)),
        out_shape=jax.ShapeDtypeStruct((s, C_WIDTH), BF16),
        scratch_shapes=[pltpu.VMEM((n_grp, C_MAPS // n_grp * t, C_WIDTH), BF16),
                        pltpu.VMEM((n_grp, C_MAPS // n_grp * t, t), F32),
                        pltpu.VMEM((C_MAPS, t, LANES), F32),
                        pltpu.VMEM((C_MAPS, t, LANES), F32)],
        compiler_params=_cparams(),
        name="diff_attn",
    )(lam4, g2, qc, kc, vc, acg)


def _silu(t):
    return t * jax.nn.sigmoid(t)


def _conv_input(gate):
    b = gate[:, G_B:G_B + 4 * B_WIDTH]
    return b[:, B_WIDTH:2 * B_WIDTH] * b[:, 2 * B_WIDTH:3 * B_WIDTH]


def _conv_taps(u, prev0, prev1):
    row = lax.broadcasted_iota(jnp.int32, u.shape, 0)
    u1 = jnp.where(row == 0, prev1, pltpu.roll(u, 1, 0))
    u2 = jnp.where(row == 0, prev0, jnp.where(row == 1, prev1, pltpu.roll(u, 2, 0)))
    return u2, u1


def _gated_mix(gate, oa, oc, u2, u1, u, cw):
    bw = B_WIDTH
    conv = u2 * cw[0:1] + u1 * cw[1:2] + u * cw[2:3]
    ob = gate[:, G_B:G_B + bw] * conv
    y = jnp.concatenate([oa * _silu(gate[:, G_AG:G_AG + A_WIDTH]),
                         ob * _silu(gate[:, G_B + 3 * bw:G_B + 4 * bw]),
                         oc * _silu(gate[:, G_CG:G_CG + C_WIDTH])], axis=-1)
    return y.astype(BF16)


def _final_norm(x, fg):
    ms = jnp.mean(x * x, axis=-1, keepdims=True)
    return x * lax.rsqrt(ms + NORM_EPS) * fg


def _merge_kernel(*refs, final):
    x_ref, *y_refs, w_ref, fg_ref, xo_ref = refs
    y = jnp.concatenate([r[...] for r in y_refs], axis=-1) if len(y_refs) > 1 else y_refs[0][...]
    out = x_ref[...] + jnp.dot(y, w_ref[0], preferred_element_type=F32)
    xo_ref[...] = _final_norm(out, fg_ref[...]) if final else out


def _merge(x, ys, w_all, layer, fg, ts, final):
    n, d = x.shape
    row = lambda width: pl.BlockSpec((ts, width), lambda i: (i, 0))
    return pl.pallas_call(
        functools.partial(_merge_kernel, final=final),
        grid=(n // ts,),
        in_specs=[row(d)] + [row(y.shape[1]) for y in ys] + [_layer_weights(w_all, layer), _resident((1, d))],
        out_specs=row(d),
        out_shape=jax.ShapeDtypeStruct((n, d), F32),
        compiler_params=_cparams(),
        name="merge",
    )(x, *ys, w_all, fg)


def _stack_heads(q, n_groups, width):
    grp = lax.broadcasted_iota(jnp.int32, q.shape, 1) // width
    return jnp.concatenate([jnp.where(grp == g, q, jnp.zeros_like(q)) for g in range(n_groups)], axis=0)


def _cached_attention(qs, kt_c, vt_c, k_n, v_n, bias_c, bias_n):
    s_c = jnp.dot(qs, kt_c, preferred_element_type=F32)
    s_n = _nt_dot(qs, k_n)
    if bias_c is not None:
        s_c, s_n = s_c + bias_c, s_n + bias_n
    m = jnp.maximum(jnp.max(s_c, axis=-1, keepdims=True), jnp.max(s_n, axis=-1, keepdims=True))
    e_c = jnp.exp2(s_c - m)
    e_n = jnp.exp2(s_n - m)
    l = jnp.sum(e_c, axis=-1, keepdims=True) + jnp.sum(e_n, axis=-1, keepdims=True)
    o = _nt_dot(e_c.astype(BF16), vt_c) + jnp.dot(e_n.astype(BF16), v_n, preferred_element_type=F32)
    return o / l


def _sample_kernel(lam_ref, g_ref, bc_ref, bn_ref,
                   qa_ref, ka_ref, va_ref, qc_ref, kc_ref, vc_ref, gate_ref,
                   cak_ref, cav_ref, cck_ref, ccv_ref, conv_ref, cw_ref,
                   y_ref, ul_ref, *, lam_init):
    t = qa_ref.shape[0]
    qs = _stack_heads(qa_ref[...], A_HEADS, HEAD_DIM)
    of = _cached_attention(qs, cak_ref[0, 0].astype(BF16), cav_ref[0, 0].astype(BF16),
                           ka_ref[...], va_ref[...], bc_ref[...], bn_ref[...])
    grp = lax.broadcasted_iota(jnp.int32, (t, A_WIDTH), 1) // HEAD_DIM
    oa = jnp.zeros((t, A_WIDTH), F32)
    for h in range(A_HEADS):
        oa = jnp.where(grp == h, of[h * t:(h + 1) * t], oa)
    qs = _stack_heads(qc_ref[...], C_MAPS, C_QK_DIM)
    of = _cached_attention(qs, cck_ref[0, 0].astype(BF16), ccv_ref[0, 0].astype(BF16),
                           kc_ref[...], vc_ref[...], None, None)
    lam = _diff_lambda(lam_ref, lam_init)
    grp = lax.broadcasted_iota(jnp.int32, (t, C_WIDTH), 1) // C_V_DIM
    d = jnp.zeros((t, C_WIDTH), F32)
    for h in range(C_HEADS):
        dh = of[2 * h * t:(2 * h + 1) * t] - lam * of[(2 * h + 1) * t:(2 * h + 2) * t]
        d = jnp.where(grp == h, dh, d)
    oc = jnp.concatenate([_subln(d[:, p * LANES:(p + 1) * LANES], g_ref[...], lam_init)
                          for p in range(C_WIDTH // LANES)], axis=-1)
    gate = gate_ref[...]
    u = _conv_input(gate)
    prev = conv_ref[0]
    u2, u1 = _conv_taps(u, prev[0:1], prev[1:2])
    y_ref[...] = _gated_mix(gate, oa, oc, u2, u1, u, cw_ref[...])
    ul_ref[0] = u[t - 8:]


def _sample_step(layer, lam4, g2, bias_c, bias_n, gate, qa, ka, va, qc, kc, vc,
                 cak_t, cav_t, cck_t, ccv_t, conv, cw, t, lam_init):
    n = gate.shape[0]
    d_mix = A_WIDTH + B_WIDTH + C_WIDTH
    nb = n // t
    row = lambda width: pl.BlockSpec((t, width), lambda b: (b, 0))
    cache = lambda a: pl.BlockSpec((1, 1) + a.shape[2:], lambda b: (layer, b, 0, 0))
    return pl.pallas_call(
        functools.partial(_sample_kernel, lam_init=lam_init),
        grid=(nb,),
        in_specs=[_resident((4, C_QK_DIM)), _resident((1, LANES)),
                  _resident(bias_c.shape), _resident(bias_n.shape),
                  row(A_WIDTH), row(A_WIDTH), row(A_WIDTH),
                  row(C_WIDTH), row(C_WIDTH), row(C_WIDTH), row(GATE_W),
                  cache(cak_t), cache(cav_t), cache(cck_t), cache(ccv_t),
                  pl.BlockSpec((1,) + conv.shape[1:], lambda b: (b, 0, 0)),
                  _resident(cw.shape)],
        out_specs=(row(d_mix), pl.BlockSpec((1, 8, B_WIDTH), lambda b: (b, 0, 0))),
        out_shape=(jax.ShapeDtypeStruct((n, d_mix), BF16),
                   jax.ShapeDtypeStruct((nb, 8, B_WIDTH), F32)),
        compiler_params=_cparams(),
        name="sample_step",
    )(lam4, g2, bias_c, bias_n, qa, ka, va, qc, kc, vc, gate,
      cak_t, cav_t, cck_t, ccv_t, conv, cw)


def _feature_major(cache):
    nd = cache.ndim
    c = jnp.transpose(cache, (0, 1) + tuple(range(3, nd)) + (2,))
    return c.reshape(c.shape[0], c.shape[1], -1, c.shape[-1])


def kernel(x_prompt, x_sample, cache_a_k, cache_a_v, state_conv, cache_c_k, cache_c_v, norm_g, w_in, w_out, rel_bias, conv_w, lam_q1, lam_k1, lam_q2, lam_k2, subln_g, final_g):
    depth = norm_g.shape[0]
    batch, seq, d_model = x_prompt.shape
    nb, t, _ = x_sample.shape
    past = cache_c_k.shape[2]
    win = cache_a_k.shape[2]
    ts = TS_PROMPT
    assert batch == 1 and win == A_WIN and seq % T_C == 0 and seq % ts == 0 and seq >= WIN_A
    assert t % 16 == 0 and t <= CHUNK and past % LANES == 0
    keep = min(A_WIN, seq)
    assert keep % ts == 0

    xp = x_prompt.reshape(seq, d_model)
    xs = x_sample.reshape(nb * t, d_model)
    rope_p = _rope_angle_tables(jnp.arange(0, seq, ts), jnp.arange(ts))
    rope_s = _rope_angle_tables(jnp.full((1,), past), jnp.tile(jnp.arange(t), nb))
    w_in_b = w_in.astype(BF16)
    w_out_b = w_out.astype(BF16)
    fg = final_g.reshape(1, d_model).astype(F32)
    cak_t, cav_t = _feature_major(cache_a_k), _feature_major(cache_a_v)
    cck_t, ccv_t = _feature_major(cache_c_k), _feature_major(cache_c_v)

    outs = {k: [] for k in ("pak", "pav", "pcv", "sak", "sav", "scv", "sck", "scc")}
    kv_acc = None
    pending = None
    for l in range(depth):
        final = l == depth - 1
        lam_init = 0.8 - 0.6 * math.exp(-0.3 * l)
        g = norm_g[l].reshape(1, d_model).astype(F32)
        cw = conv_w[l].astype(F32)
        lam4 = jnp.stack([lam_q1[l], lam_k1[l], lam_q2[l], lam_k2[l]]).astype(F32)
        g2 = jnp.tile(subln_g[l].astype(F32), LANES // C_V_DIM).reshape(1, LANES)
        bias = _bias_tables(rel_bias[l])

        res = _inproj(xp, g, w_in_b, l, *rope_p, ts, keep, cw, kv_acc, pending)
        acg, akv, qa, ka, va, qc, kc, vc, ck_all, cv_all, yb, tail = res[:12]
        if pending is not None:
            xp = res[12]
        kv_acc = (ck_all, cv_all)
        ya = _band_attention(qa, ka, va, bias, acg)
        yc = _diff_attention(lam4, g2, qc, kc, vc, acg, lam_init)
        pending = ((ya, yb, yc), w_out_b, l)
        outs["pak"].append(akv[:, :A_WIDTH].reshape(1, keep, A_HEADS, HEAD_DIM))
        outs["pav"].append(akv[:, A_WIDTH:].reshape(1, keep, A_HEADS, HEAD_DIM))
        outs["pcv"].append(tail[8 - (CONV_WIDTH - 1):][None])

        gate, akv, qa, ka, va, qc, kc, vc, ckr, cvr = _inproj(
            xs, g, w_in_b, l, *rope_s, nb * t, nb * t)
        unmasked = bias[N_BIAS - 1, :, :t, :A_WIN + t].reshape(A_HEADS * t, A_WIN + t)
        ys, ul = _sample_step(
            l, lam4, g2, unmasked[:, :A_WIN], unmasked[:, A_WIN:], gate, qa, ka, va, qc, kc, vc,
            cak_t, cav_t, cck_t, ccv_t, state_conv[l], cw, t, lam_init)
        xs = _merge(xs, (ys,), w_out_b, l, fg, nb * t, final)
        outs["sak"].append(akv[:, :A_WIDTH].reshape(nb, t, A_HEADS, HEAD_DIM))
        outs["sav"].append(akv[:, A_WIDTH:].reshape(nb, t, A_HEADS, HEAD_DIM))
        outs["scv"].append(ul[:, 8 - (CONV_WIDTH - 1):])
        outs["sck"].append(ckr.reshape(nb, t, C_HEADS, 2, C_QK_DIM))
        outs["scc"].append(cvr.reshape(nb, t, C_HEADS, C_V_DIM))

    xp = _merge(xp, pending[0], w_out_b, depth - 1, fg, TS_MERGE, True)
    st = lambda k: jnp.stack(outs[k])
    ck_all, cv_all = kv_acc
    pck = jnp.transpose(ck_all.reshape(depth, C_HEADS, 2, C_QK_DIM, seq), (0, 4, 1, 2, 3))[:, None]
    pcc = jnp.transpose(cv_all.reshape(depth, C_HEADS, C_V_DIM, seq), (0, 3, 1, 2))[:, None]
    return (xp.reshape(batch, seq, d_model), xs.reshape(nb, t, d_model),
            st("pak"), st("pav"), st("pcv"), pck, pcc,
            st("sak"), st("sav"), st("scv"), st("sck"), st("scc"))
```

```python
import functools
import math

import numpy as np
import jax
import jax.numpy as jnp
from jax import lax
from jax.experimental import pallas as pl
from jax.experimental.pallas import tpu as pltpu

F32 = jnp.float32
BF16 = jnp.bfloat16

CHUNK = 64
HEAD_DIM = 64
A_HEADS = 8
A_WIDTH = A_HEADS * HEAD_DIM
A_PREV_CHUNKS = 8
A_WIN = A_PREV_CHUNKS * CHUNK
REL_CLIP = 128
B_WIDTH = 256
CONV_WIDTH = 3
C_HEADS = 4
C_QK_DIM = 32
C_V_DIM = 64
C_WIDTH = C_HEADS * C_V_DIM
C_MAPS = 2 * C_HEADS
ROPE_DIMS = 8
ROPE_THETA = 500000.0
NORM_EPS = 1e-6
SUBLN_EPS = 1e-5
D_IN_PROJ = 4096
COL_AQ, COL_AK, COL_AV, COL_AG = 0, 512, 1024, 1536
COL_B = 2048
COL_CQ, COL_CK, COL_CV, COL_CG = 3072, 3328, 3584, 3840
GATE_W = A_WIDTH + 4 * B_WIDTH + C_WIDTH
G_AG, G_B, G_CG = 0, A_WIDTH, A_WIDTH + 4 * B_WIDTH

LANES = 128
HALF = LANES // 2
NEG = -1e30
LOG2E = math.log2(math.e)
A_QSCALE = HEAD_DIM ** -0.5 * LOG2E
C_QSCALE = C_QK_DIM ** -0.5 * LOG2E
VMEM_LIMIT = 48 * 1024 * 1024

TS_PROMPT = 512
TS_MERGE = 1024
TQ_A = 256
WIN_A = A_WIN + TQ_A
ROLL_A = 1024
N_BIAS = 3
T_C = 512


def _nt_dot(a, b):
    return lax.dot_general(a, b, (((1,), (1,)), ((), ())), preferred_element_type=F32)


def _normalize_pair(acc_even, acc_odd, lo):
    o = jnp.where(lo, acc_even, acc_odd)
    l = jnp.where(lo, acc_odd, acc_even)
    return o / pltpu.roll(l, HALF, 1)


def _cparams(n_axes=1):
    return pltpu.CompilerParams(dimension_semantics=("parallel",) * n_axes,
                                vmem_limit_bytes=VMEM_LIMIT)


def _resident(shape):
    nd = len(shape)
    return pl.BlockSpec(shape, lambda *_: (0,) * nd, pipeline_mode=pl.Buffered(1))


def _layer_weights(w_all, layer):
    return pl.BlockSpec((1,) + w_all.shape[1:], lambda *_: (layer, 0, 0), pipeline_mode=pl.Buffered(1))


def _rope(x, cos, sa, sb):
    return (x * cos + pltpu.roll(x, LANES - ROPE_DIMS // 2, 1) * sa
            + pltpu.roll(x, ROPE_DIMS // 2, 1) * sb)


def _rope_coefficients(base, rt_ref):
    cb, sb_ = base[0:1], base[1:2]
    ct, st = rt_ref[0], rt_ref[1]
    cos_p = cb * ct - sb_ * st
    sin_p = sb_ * ct + cb * st
    d = lax.broadcasted_iota(jnp.int32, cos_p.shape, 1) % C_QK_DIM
    half = ROPE_DIMS // 2
    cos = jnp.where(d < ROPE_DIMS, cos_p, 1.0)
    sa = jnp.where(d < half, -sin_p, 0.0)
    sb = jnp.where((d >= half) & (d < ROPE_DIMS), sin_p, 0.0)
    return cos, sa, sb


def _inproj_kernel(*refs, prompt, n_aliased, zero_rest, merged):
    x_ref, g_ref, w_ref, rb_ref, rt_ref = refs[:5]
    if prompt:
        cw_ref = refs[5]
        n_merge_in = 4 if merged else 0
        y_refs, wo_ref = refs[6:6 + n_merge_in - 1], refs[6 + n_merge_in - 1]
        outs = refs[6 + n_merge_in + n_aliased:]
        (acg_ref, akv_ref, qa_ref, ka_ref, va_ref, qc_ref, kc_ref, vc_ref, ckr_ref, cvr_ref,
         yb_ref, tail_ref) = outs[:12]
        carry_ref = outs[-1]
    else:
        gate_ref, akv_ref, qa_ref, ka_ref, va_ref, qc_ref, kc_ref, vc_ref, ckr_ref, cvr_ref = refs[5:]

    if prompt:
        @pl.when(pl.program_id(0) == 0)
        def _():
            carry_ref[...] = jnp.zeros(carry_ref.shape, F32)

    x = x_ref[...]
    if merged:
        y = jnp.concatenate([r[...] for r in y_refs], axis=-1)
        x = x + jnp.dot(y, wo_ref[0], preferred_element_type=F32)
        outs[12][...] = x
    ms = jnp.mean(x * x, axis=-1, keepdims=True)
    h = (x * lax.rsqrt(ms + NORM_EPS) * g_ref[...]).astype(BF16)
    cos, sa, sb = _rope_coefficients(rb_ref[0], rt_ref)

    def proj(lo, width):
        return jnp.dot(h, w_ref[0, :, lo:lo + width], preferred_element_type=F32)

    def put_f32(ref, t, val):
        if prompt:
            ref[0, t * LANES:(t + 1) * LANES, :] = val.T
            if zero_rest:
                ref[1:, t * LANES:(t + 1) * LANES, :] = jnp.zeros(
                    (ref.shape[0] - 1, LANES, val.shape[0]), F32)
        else:
            ref[:, t * LANES:(t + 1) * LANES] = val

    zc = proj(COL_CQ, 2 * C_WIDTH)
    for t in range(C_WIDTH // LANES):
        cq = _rope(zc[:, t * LANES:(t + 1) * LANES], cos, sa, sb)
        qc_ref[:, t * LANES:(t + 1) * LANES] = (cq * C_QSCALE).astype(BF16)
        ck = _rope(zc[:, C_WIDTH + t * LANES:C_WIDTH + (t + 1) * LANES], cos, sa, sb)
        kc_ref[:, t * LANES:(t + 1) * LANES] = ck.astype(BF16)
        put_f32(ckr_ref, t, ck)
    zc = proj(COL_CV, 2 * C_WIDTH)
    vc_ref[...] = zc[:, :C_WIDTH].astype(BF16)
    for t in range(C_WIDTH // LANES):
        put_f32(cvr_ref, t, zc[:, t * LANES:(t + 1) * LANES])
    if prompt:
        acg_ref[:, A_WIDTH:] = zc[:, C_WIDTH:]
    else:
        gate_ref[:, G_CG:G_CG + C_WIDTH] = zc[:, C_WIDTH:]

    zb = proj(COL_B, 4 * B_WIDTH)
    if prompt:
        bw = B_WIDTH
        u = zb[:, bw:2 * bw] * zb[:, 2 * bw:3 * bw]
        prev = carry_ref[...]
        u2, u1 = _conv_taps(u, prev[6:7], prev[7:8])
        cw = cw_ref[...]
        conv = u2 * cw[0:1] + u1 * cw[1:2] + u * cw[2:3]
        yb_ref[...] = (zb[:, :bw] * conv * _silu(zb[:, 3 * bw:])).astype(BF16)
        carry_ref[...] = u[u.shape[0] - 8:]
        tail_ref[...] = u[u.shape[0] - 8:]
    else:
        gate_ref[:, G_B:G_B + 4 * B_WIDTH] = zb

    zc = proj(COL_AK, A_WIDTH)
    ka_ref[...] = zc.astype(BF16)
    akv_ref[:, :A_WIDTH] = zc
    zc = proj(COL_AV, A_WIDTH)
    va_ref[...] = zc.astype(BF16)
    akv_ref[:, A_WIDTH:] = zc
    zc = proj(COL_AG, A_WIDTH)
    if prompt:
        acg_ref[:, :A_WIDTH] = zc
    else:
        gate_ref[:, G_AG:G_AG + A_WIDTH] = zc
    qa_ref[...] = (proj(COL_AQ, A_WIDTH) * A_QSCALE).astype(BF16)


def _inproj(x, g, w_all, layer, rope_base, rope_off, ts, keep, cw=None, kv_acc=None, merge_in=None):
    prompt = cw is not None
    depth = w_all.shape[0]
    n, d = x.shape
    nt = n // ts
    first_keep = nt - keep // ts
    row = lambda width: pl.BlockSpec((ts, width), lambda i: (i, 0))
    sds = jax.ShapeDtypeStruct
    if prompt:
        kv_shape = (depth, C_WIDTH, n)
        slabs = depth if kv_acc is None else 1
        kv_spec = pl.BlockSpec((slabs, C_WIDTH, ts), lambda i: (layer, 0, i))
        gate_w = A_WIDTH + C_WIDTH
    else:
        kv_shape, kv_spec = (n, C_WIDTH), row(C_WIDTH)
        gate_w = GATE_W
    akv_spec = pl.BlockSpec((ts, 2 * A_WIDTH), lambda i: (jnp.maximum(i - first_keep, 0), 0))
    out_shape = [sds((n, gate_w), F32), sds((keep, 2 * A_WIDTH), F32),
                 sds((n, A_WIDTH), BF16), sds((n, A_WIDTH), BF16), sds((n, A_WIDTH), BF16),
                 sds((n, C_WIDTH), BF16), sds((n, C_WIDTH), BF16), sds((n, C_WIDTH), BF16),
                 sds(kv_shape, F32), sds(kv_shape, F32)]
    out_specs = [row(gate_w), akv_spec, row(A_WIDTH), row(A_WIDTH), row(A_WIDTH),
                 row(C_WIDTH), row(C_WIDTH), row(C_WIDTH), kv_spec, kv_spec]
    in_specs = [row(d), _resident((1, d)),
                _layer_weights(w_all, layer),
                pl.BlockSpec((1, 2, LANES), lambda i: (i, 0, 0)), _resident(rope_off.shape)]
    args = [x, g, w_all, rope_base, rope_off]
    scratch = []
    aliases = {}
    if prompt:
        in_specs.append(_resident(cw.shape))
        args.append(cw)
        if merge_in is not None:
            ys, wo_all, prev_layer = merge_in
            in_specs += [row(y.shape[1]) for y in ys] + [_layer_weights(wo_all, prev_layer)]
            args += list(ys) + [wo_all]
        if kv_acc is not None:
            aliases = {len(args) + j: 8 + j for j in range(len(kv_acc))}
            in_specs += [pl.BlockSpec(memory_space=pl.ANY)] * len(kv_acc)
            args += list(kv_acc)
        out_shape += [sds((n, B_WIDTH), BF16), sds((8, B_WIDTH), F32)]
        out_specs += [row(B_WIDTH), pl.BlockSpec((8, B_WIDTH), lambda i: (0, 0))]
        if merge_in is not None:
            out_shape.append(sds((n, d), F32))
            out_specs.append(row(d))
        scratch = [pltpu.VMEM((8, B_WIDTH), F32)]
    return pl.pallas_call(
        functools.partial(_inproj_kernel, prompt=prompt, n_aliased=len(aliases),
                          zero_rest=prompt and kv_acc is None and depth > 1,
                          merged=merge_in is not None),
        grid=(nt,),
        in_specs=in_specs,
        out_specs=tuple(out_specs),
        out_shape=tuple(out_shape),
        scratch_shapes=scratch,
        input_output_aliases=aliases,
        compiler_params=pltpu.CompilerParams(dimension_semantics=("arbitrary",),
                                             vmem_limit_bytes=VMEM_LIMIT),
        name="inproj",
    )(*args)


def _rope_angle_tables(tile_base, row_off):
    half = ROPE_DIMS // 2
    inv_freq = ROPE_THETA ** (-jnp.arange(half, dtype=F32) * (2.0 / ROPE_DIMS))
    lane_freq = inv_freq[(np.arange(LANES) % C_QK_DIM) % half][None, :]
    a = tile_base.astype(F32)[:, None] * lane_freq
    b = row_off.astype(F32)[:, None] * lane_freq
    return jnp.stack([jnp.cos(a), jnp.sin(a)], axis=1), jnp.stack([jnp.cos(b), jnp.sin(b)], axis=0)


def _bias_kernel(u_ref, b_ref):
    v = pl.program_id(0)
    qc = lax.broadcasted_iota(jnp.int32, (TQ_A, WIN_A), 0) // CHUNK
    kc = lax.broadcasted_iota(jnp.int32, (TQ_A, WIN_A), 1) // CHUNK
    top = qc + v * (TQ_A // CHUNK)
    valid = (kc <= top) & (kc >= top - A_PREV_CHUNKS)
    for h in range(A_HEADS):
        u = jnp.broadcast_to(u_ref[0, h], (TQ_A, ROLL_A))
        t = pltpu.roll(u, 0, 1, stride=1, stride_axis=0)[:, ROLL_A - WIN_A:]
        b_ref[0, h] = jnp.where(valid, t * LOG2E, NEG)


def _bias_tables(rel_bias_l):
    tab = rel_bias_l.astype(F32)
    rows = []
    for off in (0, TQ_A, A_WIN):
        n_hi = off + (ROLL_A - WIN_A) - REL_CLIP
        n_lo = ROLL_A - n_hi - (2 * REL_CLIP + 1)
        rows.append(jnp.concatenate([jnp.broadcast_to(tab[:, -1:], (A_HEADS, n_hi)), tab[:, ::-1],
                                     jnp.broadcast_to(tab[:, :1], (A_HEADS, n_lo))], axis=1))
    u = jnp.stack(rows)[:, :, None, :]
    return pl.pallas_call(
        _bias_kernel,
        grid=(N_BIAS,),
        in_specs=[pl.BlockSpec((1, A_HEADS, 1, ROLL_A), lambda v: (v, 0, 0, 0))],
        out_specs=pl.BlockSpec((1, A_HEADS, TQ_A, WIN_A), lambda v: (v, 0, 0, 0)),
        out_shape=jax.ShapeDtypeStruct((N_BIAS, A_HEADS, TQ_A, WIN_A), F32),
        compiler_params=_cparams(),
        name="bias",
    )(u)


def _band_kernel(q_ref, k0_ref, k1_ref, k2_ref, v0_ref, v1_ref, v2_ref, b_ref, ag_ref, o_ref, s_ref):
    i = pl.program_id(0)
    n_pair = A_WIDTH // LANES
    lo = lax.broadcasted_iota(jnp.int32, (TQ_A, LANES), 1) < HALF
    lo_w = lax.broadcasted_iota(jnp.int32, (WIN_A, LANES), 1) < HALF

    @pl.when(i == 0)
    def _():
        s_ref[1] = jnp.zeros(s_ref.shape[1:], F32)

    def step(cur):
        for p in range(n_pair):
            cols = slice(p * LANES, (p + 1) * LANES)
            q2 = q_ref[:, cols]
            qs = jnp.concatenate([jnp.where(lo, q2, jnp.zeros_like(q2)),
                                  jnp.where(lo, jnp.zeros_like(q2), q2)], axis=0)
            kw = jnp.concatenate([r[:, cols] for r in (k0_ref, k1_ref, k2_ref)], axis=0)
            s_ref[cur, 2 * p * TQ_A:(2 * p + 2) * TQ_A, :] = _nt_dot(qs, kw)
            vw = jnp.concatenate([r[:, cols] for r in (v0_ref, v1_ref, v2_ref)], axis=0)
            outs = []
            for half in range(2):
                h = 2 * p + half
                sel_w = lo_w if half == 0 else jnp.logical_not(lo_w)
                v_aug = jnp.where(sel_w, vw, jnp.ones_like(vw))
                s = s_ref[1 - cur, h * TQ_A:(h + 1) * TQ_A, :] + b_ref[0, h]
                m = jnp.max(s, axis=-1, keepdims=True)
                e = jnp.exp2(s - m).astype(BF16)
                outs.append(jnp.dot(e, v_aug, preferred_element_type=F32))
            o_ref[:, cols] = (_normalize_pair(outs[0], outs[1], lo) * _silu(ag_ref[:, cols])).astype(BF16)

    parity = jnp.bitwise_and(i, 1)
    pl.when(parity == 0)(functools.partial(step, 0))
    pl.when(parity == 1)(functools.partial(step, 1))


def _band_attention(qa, ka, va, bias, acg):
    s = qa.shape[0]
    nq = s // TQ_A
    q_tile = lambda i: jnp.minimum(i, nq - 1)
    o_tile = lambda i: jnp.maximum(i - 1, 0)
    first = lambda tile: jnp.maximum(tile - A_WIN // TQ_A, 0)
    k_spec = lambda j: pl.BlockSpec((TQ_A, A_WIDTH), lambda i: (first(q_tile(i)) + j, 0))
    v_spec = lambda j: pl.BlockSpec((TQ_A, A_WIDTH), lambda i: (first(o_tile(i)) + j, 0))
    return pl.pallas_call(
        _band_kernel,
        grid=(nq + 1,),
        in_specs=[pl.BlockSpec((TQ_A, A_WIDTH), lambda i: (q_tile(i), 0)),
                  k_spec(0), k_spec(1), k_spec(2), v_spec(0), v_spec(1), v_spec(2),
                  pl.BlockSpec((1, A_HEADS, TQ_A, WIN_A),
                               lambda i: (jnp.minimum(o_tile(i), N_BIAS - 1), 0, 0, 0)),
                  pl.BlockSpec((TQ_A, A_WIDTH), lambda i: (o_tile(i), 0))],
        out_specs=pl.BlockSpec((TQ_A, A_WIDTH), lambda i: (o_tile(i), 0)),
        out_shape=jax.ShapeDtypeStruct((s, A_WIDTH), BF16),
        scratch_shapes=[pltpu.VMEM((2, A_HEADS * TQ_A, WIN_A), F32)],
        compiler_params=pltpu.CompilerParams(dimension_semantics=("arbitrary",),
                                             vmem_limit_bytes=VMEM_LIMIT),
        name="band_attn",
    )(qa, ka, ka, ka, va, va, va, bias, acg)


def _diff_lambda(lam_ref, lam_init):
    a = jnp.sum(lam_ref[0:1, :] * lam_ref[1:2, :], axis=-1, keepdims=True)
    b = jnp.sum(lam_ref[2:3, :] * lam_ref[3:4, :], axis=-1, keepdims=True)
    return jnp.exp(a) - jnp.exp(b) + lam_init


def _subln(d, g2, lam_init):
    lo = lax.broadcasted_iota(jnp.int32, d.shape, 1) < HALF
    sq = d * d
    s_lo = jnp.sum(jnp.where(lo, sq, 0.0), axis=-1, keepdims=True)
    s_hi = jnp.sum(jnp.where(lo, 0.0, sq), axis=-1, keepdims=True)
    ms = jnp.where(lo, s_lo, s_hi) * (1.0 / C_V_DIM)
    return d * lax.rsqrt(ms + SUBLN_EPS) * g2 * (1.0 - lam_init)


def _diff_kernel(lam_ref, g_ref, q_ref, k_ref, v_ref, cg_ref, o_ref, qm_ref, s_ref, m_ref, acc_ref,
                 *, lam_init):
    i = pl.program_id(0)
    t = T_C
    n_grp = C_HEADS // 2
    per_grp = C_MAPS // n_grp
    q = q_ref[...]
    lane_grp = lax.broadcasted_iota(jnp.int32, (t, C_WIDTH), 1) // C_QK_DIM
    lo = lax.broadcasted_iota(jnp.int32, (t, LANES), 1) < HALF

    def mask_queries(g):
        for loc in range(per_grp):
            hm = g * per_grp + loc
            qm_ref[g, loc * t:(loc + 1) * t, :] = jnp.where(lane_grp == hm, q, jnp.zeros_like(q))

    def k_tile(j):
        return k_ref[pl.ds(pl.multiple_of(j * t, t), t), :]

    def v_tile(j):
        return v_ref[pl.ds(pl.multiple_of(j * t, t), t), :]

    def scores(g, k_t):
        s_ref[g] = _nt_dot(qm_ref[g], k_t)

    def diagonal_probs(s, m_old):
        per_tile = LANES // CHUNK
        part_lane = lax.broadcasted_iota(jnp.int32, (CHUNK, LANES), 1)
        m_parts, p_parts = [], []
        for a in range(t // CHUNK):
            rows = slice(a * CHUNK, (a + 1) * CHUNK)
            n_full, part = divmod(a + 1, per_tile)
            pieces = [s[rows, :n_full * LANES]] if n_full else []
            if part:
                edge = s[rows, n_full * LANES:(n_full + 1) * LANES]
                pieces.append(jnp.where(part_lane < part * CHUNK, edge, -jnp.inf))
            sa = jnp.concatenate(pieces, axis=1) if len(pieces) > 1 else pieces[0]
            width = sa.shape[1]
            mn = jnp.maximum(m_old[rows], jnp.max(sa, axis=-1, keepdims=True))
            pa = jnp.exp2(sa - jnp.tile(mn, (1, width // LANES))).astype(BF16)
            if width < t:
                pa = jnp.concatenate([pa, jnp.zeros((CHUNK, t - width), BF16)], axis=1)
            m_parts.append(mn)
            p_parts.append(pa)
        return jnp.concatenate(m_parts, axis=0), jnp.concatenate(p_parts, axis=0)

    def softmax_pv(g, v_t, diagonal):
        v2 = v_t[:, g * LANES:(g + 1) * LANES]
        for half in range(2):
            h = 2 * g + half
            sel = lo if half == 0 else jnp.logical_not(lo)
            v_aug = jnp.where(sel, v2, jnp.ones_like(v2))
            ps, alphas = [], []
            for mp in range(2):
                hm = 2 * h + mp
                loc = hm - g * per_grp
                s = s_ref[g, loc * t:(loc + 1) * t, :]
                m_old = m_ref[hm]
                if diagonal:
                    m_new, p = diagonal_probs(s, m_old)
                else:
                    m_new = jnp.maximum(m_old, jnp.max(s, axis=-1, keepdims=True))
                    p = jnp.exp2(s - jnp.tile(m_new, (1, t // LANES))).astype(BF16)
                alphas.append(jnp.exp2(m_old - m_new))
                ps.append(p)
                m_ref[hm] = m_new
            pv = jnp.dot(jnp.concatenate(ps, axis=0), v_aug, preferred_element_type=F32)
            for mp in range(2):
                hm = 2 * h + mp
                acc_ref[hm] = alphas[mp] * acc_ref[hm] + pv[mp * t:(mp + 1) * t]

    def finalize(p_, lam):
        r = [_normalize_pair(acc_ref[4 * p_ + mp], acc_ref[4 * p_ + 2 + mp], lo) for mp in range(2)]
        pair = r[0] - lam * r[1]
        tile = slice(p_ * LANES, (p_ + 1) * LANES)
        o_ref[:, tile] = (_subln(pair, g_ref[...], lam_init) * _silu(cg_ref[:, tile])).astype(BF16)

    mask_queries(0)
    scores(0, k_tile(0))
    mask_queries(1)
    m_ref[...] = jnp.full(m_ref.shape, -jnp.inf, F32)
    acc_ref[...] = jnp.zeros(acc_ref.shape, F32)

    def full_tile(j):
        v_t = v_tile(j)
        scores(1, k_tile(j))
        softmax_pv(0, v_t, False)
        scores(0, k_tile(j + 1))
        softmax_pv(1, v_t, False)

    odd = jnp.bitwise_and(i, 1)

    @pl.when(odd == 1)
    def _():
        full_tile(0)

    def body(jj, carry):
        j = odd + 2 * jj
        full_tile(j)
        full_tile(j + 1)
        return carry

    lax.fori_loop(0, lax.shift_right_logical(i, 1), body, 0)
    v_t = v_tile(i)
    lam = _diff_lambda(lam_ref, lam_init)
    scores(1, k_tile(i))
    softmax_pv(0, v_t, True)
    finalize(0, lam)
    softmax_pv(1, v_t, True)
    finalize(1, lam)


def _diff_attention(lam4, g2, qc, kc, vc, acg, lam_init):
    s = qc.shape[0]
    t = T_C
    n_grp = C_HEADS // 2
    return pl.pallas_call(
        functools.partial(_diff_kernel, lam_init=lam_init),
        grid=(s // t,),
        in_specs=[_resident((4, C_QK_DIM)), _resident((1, LANES)),
                  pl.BlockSpec((t, C_WIDTH), lambda i: (i, 0)),
                  _resident((s, C_WIDTH)), _resident((s, C_WIDTH)),
                  pl.BlockSpec((t, C_WIDTH), lambda i: (i, A_WIDTH // C_WIDTH))],
        out_specs=pl.BlockSpec((t, C_WIDTH), lambda i: (i, 0)),
        out_shape=jax.ShapeDtypeStruct((s, C_WIDTH), BF16),
        scratch_shapes=[pltpu.VMEM((n_grp, C_MAPS // n_grp * t, C_WIDTH), BF16),
                        pltpu.VMEM((n_grp, C_MAPS // n_grp * t, t), F32),
                        pltpu.VMEM((C_MAPS, t, LANES), F32),
                        pltpu.VMEM((C_MAPS, t, LANES), F32)],
        compiler_params=_cparams(),
        name="diff_attn",
    )(lam4, g2, qc, kc, vc, acg)


def _silu(t):
    return t * jax.nn.sigmoid(t)


def _conv_input(gate):
    b = gate[:, G_B:G_B + 4 * B_WIDTH]
    return b[:, B_WIDTH:2 * B_WIDTH] * b[:, 2 * B_WIDTH:3 * B_WIDTH]


def _conv_taps(u, prev0, prev1):
    row = lax.broadcasted_iota(jnp.int32, u.shape, 0)
    u1 = jnp.where(row == 0, prev1, pltpu.roll(u, 1, 0))
    u2 = jnp.where(row == 0, prev0, jnp.where(row == 1, prev1, pltpu.roll(u, 2, 0)))
    return u2, u1


def _gated_mix(gate, oa, oc, u2, u1, u, cw):
    bw = B_WIDTH
    conv = u2 * cw[0:1] + u1 * cw[1:2] + u * cw[2:3]
    ob = gate[:, G_B:G_B + bw] * conv
    y = jnp.concatenate([oa * _silu(gate[:, G_AG:G_AG + A_WIDTH]),
                         ob * _silu(gate[:, G_B + 3 * bw:G_B + 4 * bw]),
                         oc * _silu(gate[:, G_CG:G_CG + C_WIDTH])], axis=-1)
    return y.astype(BF16)


def _final_norm(x, fg):
    ms = jnp.mean(x * x, axis=-1, keepdims=True)
    return x * lax.rsqrt(ms + NORM_EPS) * fg


def _merge_kernel(*refs, final):
    x_ref, *y_refs, w_ref, fg_ref, xo_ref = refs
    y = jnp.concatenate([r[...] for r in y_refs], axis=-1) if len(y_refs) > 1 else y_refs[0][...]
    out = x_ref[...] + jnp.dot(y, w_ref[0], preferred_element_type=F32)
    xo_ref[...] = _final_norm(out, fg_ref[...]) if final else out


def _merge(x, ys, w_all, layer, fg, ts, final):
    n, d = x.shape
    row = lambda width: pl.BlockSpec((ts, width), lambda i: (i, 0))
    return pl.pallas_call(
        functools.partial(_merge_kernel, final=final),
        grid=(n // ts,),
        in_specs=[row(d)] + [row(y.shape[1]) for y in ys] + [_layer_weights(w_all, layer), _resident((1, d))],
        out_specs=row(d),
        out_shape=jax.ShapeDtypeStruct((n, d), F32),
        compiler_params=_cparams(),
        name="merge",
    )(x, *ys, w_all, fg)


def _stack_heads(q, n_groups, width):
    grp = lax.broadcasted_iota(jnp.int32, q.shape, 1) // width
    return jnp.concatenate([jnp.where(grp == g, q, jnp.zeros_like(q)) for g in range(n_groups)], axis=0)


def _cached_attention(qs, kt_c, vt_c, k_n, v_n, bias_c, bias_n):
    s_c = jnp.dot(qs, kt_c, preferred_element_type=F32)
    s_n = _nt_dot(qs, k_n)
    if bias_c is not None:
        s_c, s_n = s_c + bias_c, s_n + bias_n
    m = jnp.maximum(jnp.max(s_c, axis=-1, keepdims=True), jnp.max(s_n, axis=-1, keepdims=True))
    e_c = jnp.exp2(s_c - m)
    e_n = jnp.exp2(s_n - m)
    l = jnp.sum(e_c, axis=-1, keepdims=True) + jnp.sum(e_n, axis=-1, keepdims=True)
    o = _nt_dot(e_c.astype(BF16), vt_c) + jnp.dot(e_n.astype(BF16), v_n, preferred_element_type=F32)
    return o / l


def _sample_kernel(lam_ref, g_ref, bc_ref, bn_ref,
                   qa_ref, ka_ref, va_ref, qc_ref, kc_ref, vc_ref, gate_ref,
                   cak_ref, cav_ref, cck_ref, ccv_ref, conv_ref, cw_ref,
                   y_ref, ul_ref, *, lam_init):
    t = qa_ref.shape[0]
    qs = _stack_heads(qa_ref[...], A_HEADS, HEAD_DIM)
    of = _cached_attention(qs, cak_ref[0, 0].astype(BF16), cav_ref[0, 0].astype(BF16),
                           ka_ref[...], va_ref[...], bc_ref[...], bn_ref[...])
    grp = lax.broadcasted_iota(jnp.int32, (t, A_WIDTH), 1) // HEAD_DIM
    oa = jnp.zeros((t, A_WIDTH), F32)
    for h in range(A_HEADS):
        oa = jnp.where(grp == h, of[h * t:(h + 1) * t], oa)
    qs = _stack_heads(qc_ref[...], C_MAPS, C_QK_DIM)
    of = _cached_attention(qs, cck_ref[0, 0].astype(BF16), ccv_ref[0, 0].astype(BF16),
                           kc_ref[...], vc_ref[...], None, None)
    lam = _diff_lambda(lam_ref, lam_init)
    grp = lax.broadcasted_iota(jnp.int32, (t, C_WIDTH), 1) // C_V_DIM
    d = jnp.zeros((t, C_WIDTH), F32)
    for h in range(C_HEADS):
        dh = of[2 * h * t:(2 * h + 1) * t] - lam * of[(2 * h + 1) * t:(2 * h + 2) * t]
        d = jnp.where(grp == h, dh, d)
    oc = jnp.concatenate([_subln(d[:, p * LANES:(p + 1) * LANES], g_ref[...], lam_init)
                          for p in range(C_WIDTH // LANES)], axis=-1)
    gate = gate_ref[...]
    u = _conv_input(gate)
    prev = conv_ref[0]
    u2, u1 = _conv_taps(u, prev[0:1], prev[1:2])
    y_ref[...] = _gated_mix(gate, oa, oc, u2, u1, u, cw_ref[...])
    ul_ref[0] = u[t - 8:]


def _sample_step(layer, lam4, g2, bias_c, bias_n, gate, qa, ka, va, qc, kc, vc,
                 cak_t, cav_t, cck_t, ccv_t, conv, cw, t, lam_init):
    n = gate.shape[0]
    d_mix = A_WIDTH + B_WIDTH + C_WIDTH
    nb = n // t
    row = lambda width: pl.BlockSpec((t, width), lambda b: (b, 0))
    cache = lambda a: pl.BlockSpec((1, 1) + a.shape[2:], lambda b: (layer, b, 0, 0))
    return pl.pallas_call(
        functools.partial(_sample_kernel, lam_init=lam_init),
        grid=(nb,),
        in_specs=[_resident((4, C_QK_DIM)), _resident((1, LANES)),
                  _resident(bias_c.shape), _resident(bias_n.shape),
                  row(A_WIDTH), row(A_WIDTH), row(A_WIDTH),
                  row(C_WIDTH), row(C_WIDTH), row(C_WIDTH), row(GATE_W),
                  cache(cak_t), cache(cav_t), cache(cck_t), cache(ccv_t),
                  pl.BlockSpec((1,) + conv.shape[1:], lambda b: (b, 0, 0)),
                  _resident(cw.shape)],
        out_specs=(row(d_mix), pl.BlockSpec((1, 8, B_WIDTH), lambda b: (b, 0, 0))),
        out_shape=(jax.ShapeDtypeStruct((n, d_mix), BF16),
                   jax.ShapeDtypeStruct((nb, 8, B_WIDTH), F32)),
        compiler_params=_cparams(),
        name="sample_step",
    )(lam4, g2, bias_c, bias_n, qa, ka, va, qc, kc, vc, gate,
      cak_t, cav_t, cck_t, ccv_t, conv, cw)


def _feature_major(cache):
    nd = cache.ndim
    c = jnp.transpose(cache, (0, 1) + tuple(range(3, nd)) + (2,))
    return c.reshape(c.shape[0], c.shape[1], -1, c.shape[-1])


def kernel(x_prompt, x_sample, cache_a_k, cache_a_v, state_conv, cache_c_k, cache_c_v, norm_g, w_in, w_out, rel_bias, conv_w, lam_q1, lam_k1, lam_q2, lam_k2, subln_g, final_g):
    depth = norm_g.shape[0]
    batch, seq, d_model = x_prompt.shape
    nb, t, _ = x_sample.shape
    past = cache_c_k.shape[2]
    win = cache_a_k.shape[2]
    ts = TS_PROMPT
    assert batch == 1 and win == A_WIN and seq % T_C == 0 and seq % ts == 0 and seq >= WIN_A
    assert t % 16 == 0 and t <= CHUNK and past % LANES == 0
    keep = min(A_WIN, seq)
    assert keep % ts == 0

    xp = x_prompt.reshape(seq, d_model)
    xs = x_sample.reshape(nb * t, d_model)
    rope_p = _rope_angle_tables(jnp.arange(0, seq, ts), jnp.arange(ts))
    rope_s = _rope_angle_tables(jnp.full((1,), past), jnp.tile(jnp.arange(t), nb))
    w_in_b = w_in.astype(BF16)
    w_out_b = w_out.astype(BF16)
    fg = final_g.reshape(1, d_model).astype(F32)
    cak_t, cav_t = _feature_major(cache_a_k), _feature_major(cache_a_v)
    cck_t, ccv_t = _feature_major(cache_c_k), _feature_major(cache_c_v)

    outs = {k: [] for k in ("pak", "pav", "pcv", "sak", "sav", "scv", "sck", "scc")}
    kv_acc = None
    pending = None
    for l in range(depth):
        final = l == depth - 1
        lam_init = 0.8 - 0.6 * math.exp(-0.3 * l)
        g = norm_g[l].reshape(1, d_model).astype(F32)
        cw = conv_w[l].astype(F32)
        lam4 = jnp.stack([lam_q1[l], lam_k1[l], lam_q2[l], lam_k2[l]]).astype(F32)
        g2 = jnp.tile(subln_g[l].astype(F32), LANES // C_V_DIM).reshape(1, LANES)
        bias = _bias_tables(rel_bias[l])

        res = _inproj(xp, g, w_in_b, l, *rope_p, ts, keep, cw, kv_acc, pending)
        acg, akv, qa, ka, va, qc, kc, vc, ck_all, cv_all, yb, tail = res[:12]
        if pending is not None:
            xp = res[12]
        kv_acc = (ck_all, cv_all)
        ya = _band_attention(qa, ka, va, bias, acg)
        yc = _diff_attention(lam4, g2, qc, kc, vc, acg, lam_init)
        pending = ((ya, yb, yc), w_out_b, l)
        outs["pak"].append(akv[:, :A_WIDTH].reshape(1, keep, A_HEADS, HEAD_DIM))
        outs["pav"].append(akv[:, A_WIDTH:].reshape(1, keep, A_HEADS, HEAD_DIM))
        outs["pcv"].append(tail[8 - (CONV_WIDTH - 1):][None])

        gate, akv, qa, ka, va, qc, kc, vc, ckr, cvr = _inproj(
            xs, g, w_in_b, l, *rope_s, nb * t, nb * t)
        unmasked = bias[N_BIAS - 1, :, :t, :A_WIN + t].reshape(A_HEADS * t, A_WIN + t)
        ys, ul = _sample_step(
            l, lam4, g2, unmasked[:, :A_WIN], unmasked[:, A_WIN:], gate, qa, ka, va, qc, kc, vc,
            cak_t, cav_t, cck_t, ccv_t, state_conv[l], cw, t, lam_init)
        xs = _merge(xs, (ys,), w_out_b, l, fg, nb * t, final)
        outs["sak"].append(akv[:, :A_WIDTH].reshape(nb, t, A_HEADS, HEAD_DIM))
        outs["sav"].append(akv[:, A_WIDTH:].reshape(nb, t, A_HEADS, HEAD_DIM))
        outs["scv"].append(ul[:, 8 - (CONV_WIDTH - 1):])
        outs["sck"].append(ckr.reshape(nb, t, C_HEADS, 2, C_QK_DIM))
        outs["scc"].append(cvr.reshape(nb, t, C_HEADS, C_V_DIM))

    xp = _merge(xp, pending[0], w_out_b, depth - 1, fg, TS_MERGE, True)
    st = lambda k: jnp.stack(outs[k])
    ck_all, cv_all = kv_acc
    pck = jnp.transpose(ck_all.reshape(depth, C_HEADS, 2, C_QK_DIM, seq), (0, 4, 1, 2, 3))[:, None]
    pcc = jnp.transpose(cv_all.reshape(depth, C_HEADS, C_V_DIM, seq), (0, 3, 1, 2))[:, None]
    return (xp.reshape(batch, seq, d_model), xs.reshape(nb, t, d_model),
            st("pak"), st("pav"), st("pcv"), pck, pcc,
            st("sak"), st("sav"), st("scv"), st("sck"), st("scc"))
```

```python
import functools
import math

import numpy as np
import jax
import jax.numpy as jnp
from jax import lax
from jax.experimental import pallas as pl
from jax.experimental.pallas import tpu as pltpu

F32 = jnp.float32
BF16 = jnp.bfloat16

CHUNK = 64
HEAD_DIM = 64
A_HEADS = 8
A_WIDTH = A_HEADS * HEAD_DIM
A_PREV_CHUNKS = 8
A_WIN = A_PREV_CHUNKS * CHUNK
REL_CLIP = 128
B_WIDTH = 256
CONV_WIDTH = 3
C_HEADS = 4
C_QK_DIM = 32
C_V_DIM = 64
C_WIDTH = C_HEADS * C_V_DIM
C_MAPS = 2 * C_HEADS
ROPE_DIMS = 8
ROPE_THETA = 500000.0
NORM_EPS = 1e-6
SUBLN_EPS = 1e-5
D_IN_PROJ = 4096
COL_AQ, COL_AK, COL_AV, COL_AG = 0, 512, 1024, 1536
COL_B = 2048
COL_CQ, COL_CK, COL_CV, COL_CG = 3072, 3328, 3584, 3840
GATE_W = A_WIDTH + 4 * B_WIDTH + C_WIDTH
G_AG, G_B, G_CG = 0, A_WIDTH, A_WIDTH + 4 * B_WIDTH

LANES = 128
HALF = LANES // 2
NEG = -1e30
LOG2E = math.log2(math.e)
A_QSCALE = HEAD_DIM ** -0.5 * LOG2E
C_QSCALE = C_QK_DIM ** -0.5 * LOG2E
VMEM_LIMIT = 48 * 1024 * 1024

TS_PROMPT = 512
TS_MERGE = 1024
TQ_A = 256
WIN_A = A_WIN + TQ_A
ROLL_A = 1024
N_BIAS = 3
T_C = 512


def _nt_dot(a, b):
    return lax.dot_general(a, b, (((1,), (1,)), ((), ())), preferred_element_type=F32)


def _normalize_pair(acc_even, acc_odd, lo):
    o = jnp.where(lo, acc_even, acc_odd)
    l = jnp.where(lo, acc_odd, acc_even)
    return o / pltpu.roll(l, HALF, 1)


def _cparams(n_axes=1):
    return pltpu.CompilerParams(dimension_semantics=("parallel",) * n_axes,
                                vmem_limit_bytes=VMEM_LIMIT)


def _resident(shape):
    nd = len(shape)
    return pl.BlockSpec(shape, lambda *_: (0,) * nd, pipeline_mode=pl.Buffered(1))


def _layer_weights(w_all, layer):
    return pl.BlockSpec((1,) + w_all.shape[1:], lambda *_: (layer, 0, 0), pipeline_mode=pl.Buffered(1))


def _rope(x, cos, sa, sb):
    return (x * cos + pltpu.roll(x, LANES - ROPE_DIMS // 2, 1) * sa
            + pltpu.roll(x, ROPE_DIMS // 2, 1) * sb)


def _rope_coefficients(base, rt_ref):
    cb, sb_ = base[0:1], base[1:2]
    ct, st = rt_ref[0], rt_ref[1]
    cos_p = cb * ct - sb_ * st
    sin_p = sb_ * ct + cb * st
    d = lax.broadcasted_iota(jnp.int32, cos_p.shape, 1) % C_QK_DIM
    half = ROPE_DIMS // 2
    cos = jnp.where(d < ROPE_DIMS, cos_p, 1.0)
    sa = jnp.where(d < half, -sin_p, 0.0)
    sb = jnp.where((d >= half) & (d < ROPE_DIMS), sin_p, 0.0)
    return cos, sa, sb


def _inproj_kernel(*refs, prompt, n_aliased, zero_rest, merged):
    x_ref, g_ref, w_ref, rb_ref, rt_ref = refs[:5]
    if prompt:
        cw_ref = refs[5]
        n_merge_in = 4 if merged else 0
        y_refs, wo_ref = refs[6:6 + n_merge_in - 1], refs[6 + n_merge_in - 1]
        outs = refs[6 + n_merge_in + n_aliased:]
        (acg_ref, akv_ref, qa_ref, ka_ref, va_ref, qc_ref, kc_ref, vc_ref, ckr_ref, cvr_ref,
         yb_ref, tail_ref) = outs[:12]
        carry_ref = outs[-1]
    else:
        gate_ref, akv_ref, qa_ref, ka_ref, va_ref, qc_ref, kc_ref, vc_ref, ckr_ref, cvr_ref = refs[5:]

    if prompt:
        @pl.when(pl.program_id(0) == 0)
        def _():
            carry_ref[...] = jnp.zeros(carry_ref.shape, F32)

    x = x_ref[...]
    if merged:
        y = jnp.concatenate([r[...] for r in y_refs], axis=-1)
        x = x + jnp.dot(y, wo_ref[0], preferred_element_type=F32)
        outs[12][...] = x
    ms = jnp.mean(x * x, axis=-1, keepdims=True)
    h = (x * lax.rsqrt(ms + NORM_EPS) * g_ref[...]).astype(BF16)
    cos, sa, sb = _rope_coefficients(rb_ref[0], rt_ref)

    def proj(lo, width):
        return jnp.dot(h, w_ref[0, :, lo:lo + width], preferred_element_type=F32)

    def put_f32(ref, t, val):
        if prompt:
            ref[0, t * LANES:(t + 1) * LANES, :] = val.T
            if zero_rest:
                ref[1:, t * LANES:(t + 1) * LANES, :] = jnp.zeros(
                    (ref.shape[0] - 1, LANES, val.shape[0]), F32)
        else:
            ref[:, t * LANES:(t + 1) * LANES] = val

    zc = proj(COL_CQ, 2 * C_WIDTH)
    for t in range(C_WIDTH // LANES):
        cq = _rope(zc[:, t * LANES:(t + 1) * LANES], cos, sa, sb)
        qc_ref[:, t * LANES:(t + 1) * LANES] = (cq * C_QSCALE).astype(BF16)
        ck = _rope(zc[:, C_WIDTH + t * LANES:C_WIDTH + (t + 1) * LANES], cos, sa, sb)
        kc_ref[:, t * LANES:(t + 1) * LANES] = ck.astype(BF16)
        put_f32(ckr_ref, t, ck)
    zc = proj(COL_CV, 2 * C_WIDTH)
    vc_ref[...] = zc[:, :C_WIDTH].astype(BF16)
    for t in range(C_WIDTH // LANES):
        put_f32(cvr_ref, t, zc[:, t * LANES:(t + 1) * LANES])
    if prompt:
        acg_ref[:, A_WIDTH:] = zc[:, C_WIDTH:]
    else:
        gate_ref[:, G_CG:G_CG + C_WIDTH] = zc[:, C_WIDTH:]

    zb = proj(COL_B, 4 * B_WIDTH)
    if prompt:
        bw = B_WIDTH
        u = zb[:, bw:2 * bw] * zb[:, 2 * bw:3 * bw]
        prev = carry_ref[...]
        u2, u1 = _conv_taps(u, prev[6:7], prev[7:8])
        cw = cw_ref[...]
        conv = u2 * cw[0:1] + u1 * cw[1:2] + u * cw[2:3]
        yb_ref[...] = (zb[:, :bw] * conv * _silu(zb[:, 3 * bw:])).astype(BF16)
        carry_ref[...] = u[u.shape[0] - 8:]
        tail_ref[...] = u[u.shape[0] - 8:]
    else:
        gate_ref[:, G_B:G_B + 4 * B_WIDTH] = zb

    zc = proj(COL_AK, A_WIDTH)
    ka_ref[...] = zc.astype(BF16)
    akv_ref[:, :A_WIDTH] = zc
    zc = proj(COL_AV, A_WIDTH)
    va_ref[...] = zc.astype(BF16)
    akv_ref[:, A_WIDTH:] = zc
    zc = proj(COL_AG, A_WIDTH)
    if prompt:
        acg_ref[:, :A_WIDTH] = zc
    else:
        gate_ref[:, G_AG:G_AG + A_WIDTH] = zc
    qa_ref[...] = (proj(COL_AQ, A_WIDTH) * A_QSCALE).astype(BF16)


def _inproj(x, g, w_all, layer, rope_base, rope_off, ts, keep, cw=None, kv_acc=None, merge_in=None):
    prompt = cw is not None
    depth = w_all.shape[0]
    n, d = x.shape
    nt = n // ts
    first_keep = nt - keep // ts
    row = lambda width: pl.BlockSpec((ts, width), lambda i: (i, 0))
    sds = jax.ShapeDtypeStruct
    if prompt:
        kv_shape = (depth, C_WIDTH, n)
        slabs = depth if kv_acc is None else 1
        kv_spec = pl.BlockSpec((slabs, C_WIDTH, ts), lambda i: (layer, 0, i))
        gate_w = A_WIDTH + C_WIDTH
    else:
        kv_shape, kv_spec = (n, C_WIDTH), row(C_WIDTH)
        gate_w = GATE_W
    akv_spec = pl.BlockSpec((ts, 2 * A_WIDTH), lambda i: (jnp.maximum(i - first_keep, 0), 0))
    out_shape = [sds((n, gate_w), F32), sds((keep, 2 * A_WIDTH), F32),
                 sds((n, A_WIDTH), BF16), sds((n, A_WIDTH), BF16), sds((n, A_WIDTH), BF16),
                 sds((n, C_WIDTH), BF16), sds((n, C_WIDTH), BF16), sds((n, C_WIDTH), BF16),
                 sds(kv_shape, F32), sds(kv_shape, F32)]
    out_specs = [row(gate_w), akv_spec, row(A_WIDTH), row(A_WIDTH), row(A_WIDTH),
                 row(C_WIDTH), row(C_WIDTH), row(C_WIDTH), kv_spec, kv_spec]
    in_specs = [row(d), _resident((1, d)),
                _layer_weights(w_all, layer),
                pl.BlockSpec((1, 2, LANES), lambda i: (i, 0, 0)), _resident(rope_off.shape)]
    args = [x, g, w_all, rope_base, rope_off]
    scratch = []
    aliases = {}
    if prompt:
        in_specs.append(_resident(cw.shape))
        args.append(cw)
        if merge_in is not None:
            ys, wo_all, prev_layer = merge_in
            in_specs += [row(y.shape[1]) for y in ys] + [_layer_weights(wo_all, prev_layer)]
            args += list(ys) + [wo_all]
        if kv_acc is not None:
            aliases = {len(args) + j: 8 + j for j in range(len(kv_acc))}
            in_specs += [pl.BlockSpec(memory_space=pl.ANY)] * len(kv_acc)
            args += list(kv_acc)
        out_shape += [sds((n, B_WIDTH), BF16), sds((8, B_WIDTH), F32)]
        out_specs += [row(B_WIDTH), pl.BlockSpec((8, B_WIDTH), lambda i: (0, 0))]
        if merge_in is not None:
            out_shape.append(sds((n, d), F32))
            out_specs.append(row(d))
        scratch = [pltpu.VMEM((8, B_WIDTH), F32)]
    return pl.pallas_call(
        functools.partial(_inproj_kernel, prompt=prompt, n_aliased=len(aliases),
                          zero_rest=prompt and kv_acc is None and depth > 1,
                          merged=merge_in is not None),
        grid=(nt,),
        in_specs=in_specs,
        out_specs=tuple(out_specs),
        out_shape=tuple(out_shape),
        scratch_shapes=scratch,
        input_output_aliases=aliases,
        compiler_params=pltpu.CompilerParams(dimension_semantics=("arbitrary",),
                                             vmem_limit_bytes=VMEM_LIMIT),
        name="inproj",
    )(*args)


def _rope_angle_tables(tile_base, row_off):
    half = ROPE_DIMS // 2
    inv_freq = ROPE_THETA ** (-jnp.arange(half, dtype=F32) * (2.0 / ROPE_DIMS))
    lane_freq = inv_freq[(np.arange(LANES) % C_QK_DIM) % half][None, :]
    a = tile_base.astype(F32)[:, None] * lane_freq
    b = row_off.astype(F32)[:, None] * lane_freq
    return jnp.stack([jnp.cos(a), jnp.sin(a)], axis=1), jnp.stack([jnp.cos(b), jnp.sin(b)], axis=0)


def _bias_kernel(u_ref, b_ref):
    v = pl.program_id(0)
    qc = lax.broadcasted_iota(jnp.int32, (TQ_A, WIN_A), 0) // CHUNK
    kc = lax.broadcasted_iota(jnp.int32, (TQ_A, WIN_A), 1) // CHUNK
    top = qc + v * (TQ_A // CHUNK)
    valid = (kc <= top) & (kc >= top - A_PREV_CHUNKS)
    for h in range(A_HEADS):
        u = jnp.broadcast_to(u_ref[0, h], (TQ_A, ROLL_A))
        t = pltpu.roll(u, 0, 1, stride=1, stride_axis=0)[:, ROLL_A - WIN_A:]
        b_ref[0, h] = jnp.where(valid, t * LOG2E, NEG)


def _bias_tables(rel_bias_l):
    tab = rel_bias_l.astype(F32)
    rows = []
    for off in (0, TQ_A, A_WIN):
        n_hi = off + (ROLL_A - WIN_A) - REL_CLIP
        n_lo = ROLL_A - n_hi - (2 * REL_CLIP + 1)
        rows.append(jnp.concatenate([jnp.broadcast_to(tab[:, -1:], (A_HEADS, n_hi)), tab[:, ::-1],
                                     jnp.broadcast_to(tab[:, :1], (A_HEADS, n_lo))], axis=1))
    u = jnp.stack(rows)[:, :, None, :]
    return pl.pallas_call(
        _bias_kernel,
        grid=(N_BIAS,),
        in_specs=[pl.BlockSpec((1, A_HEADS, 1, ROLL_A), lambda v: (v, 0, 0, 0))],
        out_specs=pl.BlockSpec((1, A_HEADS, TQ_A, WIN_A), lambda v: (v, 0, 0, 0)),
        out_shape=jax.ShapeDtypeStruct((N_BIAS, A_HEADS, TQ_A, WIN_A), F32),
        compiler_params=_cparams(),
        name="bias",
    )(u)


def _band_kernel(q_ref, k0_ref, k1_ref, k2_ref, v0_ref, v1_ref, v2_ref, b_ref, ag_ref, o_ref, s_ref):
    i = pl.program_id(0)
    n_pair = A_WIDTH // LANES
    lo = lax.broadcasted_iota(jnp.int32, (TQ_A, LANES), 1) < HALF
    lo_w = lax.broadcasted_iota(jnp.int32, (WIN_A, LANES), 1) < HALF

    @pl.when(i == 0)
    def _():
        s_ref[1] = jnp.zeros(s_ref.shape[1:], F32)

    def probabilities(slot, h, steady):
        if not steady:
            s = s_ref[slot, h * TQ_A:(h + 1) * TQ_A, :] + b_ref[0, h]
            return jnp.exp2(s - jnp.max(s, axis=-1, keepdims=True)).astype(BF16)
        parts = []
        for a in range(TQ_A // CHUNK):
            first = (a * CHUNK) // LANES
            last = ((a + A_PREV_CHUNKS) * CHUNK) // LANES
            cols = slice(first * LANES, (last + 1) * LANES)
            sa = (s_ref[slot, h * TQ_A + a * CHUNK:h * TQ_A + (a + 1) * CHUNK, cols]
                  + b_ref[0, h, a * CHUNK:(a + 1) * CHUNK, cols])
            ea = jnp.exp2(sa - jnp.max(sa, axis=-1, keepdims=True)).astype(BF16)
            pads = [jnp.zeros((CHUNK, first * LANES), BF16), ea,
                    jnp.zeros((CHUNK, WIN_A - (last + 1) * LANES), BF16)]
            parts.append(jnp.concatenate([x for x in pads if x.shape[1]], axis=1))
        return jnp.concatenate(parts, axis=0)

    def step(cur, steady):
        for p in range(n_pair):
            cols = slice(p * LANES, (p + 1) * LANES)
            q2 = q_ref[:, cols]
            qs = jnp.concatenate([jnp.where(lo, q2, jnp.zeros_like(q2)),
                                  jnp.where(lo, jnp.zeros_like(q2), q2)], axis=0)
            kw = jnp.concatenate([r[:, cols] for r in (k0_ref, k1_ref, k2_ref)], axis=0)
            s_ref[cur, 2 * p * TQ_A:(2 * p + 2) * TQ_A, :] = _nt_dot(qs, kw)
            vw = jnp.concatenate([r[:, cols] for r in (v0_ref, v1_ref, v2_ref)], axis=0)
            outs = []
            for half in range(2):
                h = 2 * p + half
                sel_w = lo_w if half == 0 else jnp.logical_not(lo_w)
                v_aug = jnp.where(sel_w, vw, jnp.ones_like(vw))
                e = probabilities(1 - cur, h, steady)
                outs.append(jnp.dot(e, v_aug, preferred_element_type=F32))
            o_ref[:, cols] = (_normalize_pair(outs[0], outs[1], lo) * _silu(ag_ref[:, cols])).astype(BF16)

    parity = jnp.bitwise_and(i, 1)
    steady = i > A_WIN // TQ_A
    for cur in range(2):
        pl.when(jnp.logical_and(parity == cur, steady))(functools.partial(step, cur, True))
        pl.when(jnp.logical_and(parity == cur, jnp.logical_not(steady)))(functools.partial(step, cur, False))


def _band_attention(qa, ka, va, bias, acg):
    s = qa.shape[0]
    nq = s // TQ_A
    q_tile = lambda i: jnp.minimum(i, nq - 1)
    o_tile = lambda i: jnp.maximum(i - 1, 0)
    first = lambda tile: jnp.maximum(tile - A_WIN // TQ_A, 0)
    k_spec = lambda j: pl.BlockSpec((TQ_A, A_WIDTH), lambda i: (first(q_tile(i)) + j, 0))
    v_spec = lambda j: pl.BlockSpec((TQ_A, A_WIDTH), lambda i: (first(o_tile(i)) + j, 0))
    return pl.pallas_call(
        _band_kernel,
        grid=(nq + 1,),
        in_specs=[pl.BlockSpec((TQ_A, A_WIDTH), lambda i: (q_tile(i), 0)),
                  k_spec(0), k_spec(1), k_spec(2), v_spec(0), v_spec(1), v_spec(2),
                  pl.BlockSpec((1, A_HEADS, TQ_A, WIN_A),
                               lambda i: (jnp.minimum(o_tile(i), N_BIAS - 1), 0, 0, 0)),
                  pl.BlockSpec((TQ_A, A_WIDTH), lambda i: (o_tile(i), 0))],
        out_specs=pl.BlockSpec((TQ_A, A_WIDTH), lambda i: (o_tile(i), 0)),
        out_shape=jax.ShapeDtypeStruct((s, A_WIDTH), BF16),
        scratch_shapes=[pltpu.VMEM((2, A_HEADS * TQ_A, WIN_A), F32)],
        compiler_params=pltpu.CompilerParams(dimension_semantics=("arbitrary",),
                                             vmem_limit_bytes=VMEM_LIMIT),
        name="band_attn",
    )(qa, ka, ka, ka, va, va, va, bias, acg)


def _diff_lambda(lam_ref, lam_init):
    a = jnp.sum(lam_ref[0:1, :] * lam_ref[1:2, :], axis=-1, keepdims=True)
    b = jnp.sum(lam_ref[2:3, :] * lam_ref[3:4, :], axis=-1, keepdims=True)
    return jnp.exp(a) - jnp.exp(b) + lam_init


def _subln(d, g2, lam_init):
    lo = lax.broadcasted_iota(jnp.int32, d.shape, 1) < HALF
    sq = d * d
    s_lo = jnp.sum(jnp.where(lo, sq, 0.0), axis=-1, keepdims=True)
    s_hi = jnp.sum(jnp.where(lo, 0.0, sq), axis=-1, keepdims=True)
    ms = jnp.where(lo, s_lo, s_hi) * (1.0 / C_V_DIM)
    return d * lax.rsqrt(ms + SUBLN_EPS) * g2 * (1.0 - lam_init)


def _diff_kernel(lam_ref, g_ref, q_ref, k_ref, v_ref, cg_ref, o_ref, qm_ref, s_ref, m_ref, acc_ref,
                 *, lam_init):
    i = pl.program_id(0)
    t = T_C
    n_grp = C_HEADS // 2
    per_grp = C_MAPS // n_grp
    q = q_ref[...]
    lane_grp = lax.broadcasted_iota(jnp.int32, (t, C_WIDTH), 1) // C_QK_DIM
    lo = lax.broadcasted_iota(jnp.int32, (t, LANES), 1) < HALF

    def mask_queries(g):
        for loc in range(per_grp):
            hm = g * per_grp + loc
            qm_ref[g, loc * t:(loc + 1) * t, :] = jnp.where(lane_grp == hm, q, jnp.zeros_like(q))

    def k_tile(j):
        return k_ref[pl.ds(pl.multiple_of(j * t, t), t), :]

    def v_tile(j):
        return v_ref[pl.ds(pl.multiple_of(j * t, t), t), :]

    def scores(g, k_t):
        s_ref[g] = _nt_dot(qm_ref[g], k_t)

    def diagonal_probs(s, m_old):
        per_tile = LANES // CHUNK
        part_lane = lax.broadcasted_iota(jnp.int32, (CHUNK, LANES), 1)
        m_parts, p_parts = [], []
        for a in range(t // CHUNK):
            rows = slice(a * CHUNK, (a + 1) * CHUNK)
            n_full, part = divmod(a + 1, per_tile)
            pieces = [s[rows, :n_full * LANES]] if n_full else []
            if part:
                edge = s[rows, n_full * LANES:(n_full + 1) * LANES]
                pieces.append(jnp.where(part_lane < part * CHUNK, edge, -jnp.inf))
            sa = jnp.concatenate(pieces, axis=1) if len(pieces) > 1 else pieces[0]
            width = sa.shape[1]
            mn = jnp.maximum(m_old[rows], jnp.max(sa, axis=-1, keepdims=True))
            pa = jnp.exp2(sa - jnp.tile(mn, (1, width // LANES))).astype(BF16)
            if width < t:
                pa = jnp.concatenate([pa, jnp.zeros((CHUNK, t - width), BF16)], axis=1)
            m_parts.append(mn)
            p_parts.append(pa)
        return jnp.concatenate(m_parts, axis=0), jnp.concatenate(p_parts, axis=0)

    def softmax_pv(g, v_t, diagonal):
        v2 = v_t[:, g * LANES:(g + 1) * LANES]
        for half in range(2):
            h = 2 * g + half
            sel = lo if half == 0 else jnp.logical_not(lo)
            v_aug = jnp.where(sel, v2, jnp.ones_like(v2))
            ps, alphas = [], []
            for mp in range(2):
                hm = 2 * h + mp
                loc = hm - g * per_grp
                s = s_ref[g, loc * t:(loc + 1) * t, :]
                m_old = m_ref[hm]
                if diagonal:
                    m_new, p = diagonal_probs(s, m_old)
                else:
                    m_new = jnp.maximum(m_old, jnp.max(s, axis=-1, keepdims=True))
                    p = jnp.exp2(s - jnp.tile(m_new, (1, t // LANES))).astype(BF16)
                alphas.append(jnp.exp2(m_old - m_new))
                ps.append(p)
                m_ref[hm] = m_new
            pv = jnp.dot(jnp.concatenate(ps, axis=0), v_aug, preferred_element_type=F32)
            for mp in range(2):
                hm = 2 * h + mp
                acc_ref[hm] = alphas[mp] * acc_ref[hm] + pv[mp * t:(mp + 1) * t]

    def finalize(p_, lam):
        r = [_normalize_pair(acc_ref[4 * p_ + mp], acc_ref[4 * p_ + 2 + mp], lo) for mp in range(2)]
        pair = r[0] - lam * r[1]
        tile = slice(p_ * LANES, (p_ + 1) * LANES)
        o_ref[:, tile] = (_subln(pair, g_ref[...], lam_init) * _silu(cg_ref[:, tile])).astype(BF16)

    mask_queries(0)
    scores(0, k_tile(0))
    mask_queries(1)
    m_ref[...] = jnp.full(m_ref.shape, -jnp.inf, F32)
    acc_ref[...] = jnp.zeros(acc_ref.shape, F32)

    def full_tile(j):
        v_t = v_tile(j)
        scores(1, k_tile(j))
        softmax_pv(0, v_t, False)
        scores(0, k_tile(j + 1))
        softmax_pv(1, v_t, False)

    odd = jnp.bitwise_and(i, 1)

    @pl.when(odd == 1)
    def _():
        full_tile(0)

    def body(jj, carry):
        j = odd + 2 * jj
        full_tile(j)
        full_tile(j + 1)
        return carry

    lax.fori_loop(0, lax.shift_right_logical(i, 1), body, 0)
    v_t = v_tile(i)
    lam = _diff_lambda(lam_ref, lam_init)
    scores(1, k_tile(i))
    softmax_pv(0, v_t, True)
    finalize(0, lam)
    softmax_pv(1, v_t, True)
    finalize(1, lam)


def _diff_attention(lam4, g2, qc, kc, vc, acg, lam_init):
    s = qc.shape[0]
    t = T_C
    n_grp = C_HEADS // 2
    return pl.pallas_call(
        functools.partial(_diff_kernel, lam_init=lam_init),
        grid=(s // t,),
        in_specs=[_resident((4, C_QK_DIM)), _resident((1, LANES)),
                  pl.BlockSpec((t, C_WIDTH), lambda i: (i, 0)),
                  _resident((s, C_WIDTH)), _resident((s, C_WIDTH)),
                  pl.BlockSpec((t, C_WIDTH), lambda i: (i, A_WIDTH // C_WIDTH))],
        out_specs=pl.BlockSpec((t, C_WIDTH), lambda i: (i, 0)),
        out_shape=jax.ShapeDtypeStruct((s, C_WIDTH), BF16),
        scratch_shapes=[pltpu.VMEM((n_grp, C_MAPS // n_grp * t, C_WIDTH), BF16),
                        pltpu.VMEM((n_grp, C_MAPS // n_grp * t, t), F32),
                        pltpu.VMEM((C_MAPS, t, LANES), F32),
                        pltpu.VMEM((C_MAPS, t, LANES), F32)],
        compiler_params=_cparams(),
        name="diff_attn",
    )(lam4, g2, qc, kc, vc, acg)


def _silu(t):
    return t * jax.nn.sigmoid(t)


def _conv_input(gate):
    b = gate[:, G_B:G_B + 4 * B_WIDTH]
    return b[:, B_WIDTH:2 * B_WIDTH] * b[:, 2 * B_WIDTH:3 * B_WIDTH]


def _conv_taps(u, prev0, prev1):
    row = lax.broadcasted_iota(jnp.int32, u.shape, 0)
    u1 = jnp.where(row == 0, prev1, pltpu.roll(u, 1, 0))
    u2 = jnp.where(row == 0, prev0, jnp.where(row == 1, prev1, pltpu.roll(u, 2, 0)))
    return u2, u1


def _gated_mix(gate, oa, oc, u2, u1, u, cw):
    bw = B_WIDTH
    conv = u2 * cw[0:1] + u1 * cw[1:2] + u * cw[2:3]
    ob = gate[:, G_B:G_B + bw] * conv
    y = jnp.concatenate([oa * _silu(gate[:, G_AG:G_AG + A_WIDTH]),
                         ob * _silu(gate[:, G_B + 3 * bw:G_B + 4 * bw]),
                         oc * _silu(gate[:, G_CG:G_CG + C_WIDTH])], axis=-1)
    return y.astype(BF16)


def _final_norm(x, fg):
    ms = jnp.mean(x * x, axis=-1, keepdims=True)
    return x * lax.rsqrt(ms + NORM_EPS) * fg


def _merge_kernel(*refs, final):
    x_ref, *y_refs, w_ref, fg_ref, xo_ref = refs
    y = jnp.concatenate([r[...] for r in y_refs], axis=-1) if len(y_refs) > 1 else y_refs[0][...]
    out = x_ref[...] + jnp.dot(y, w_ref[0], preferred_element_type=F32)
    xo_ref[...] = _final_norm(out, fg_ref[...]) if final else out


def _merge(x, ys, w_all, layer, fg, ts, final):
    n, d = x.shape
    row = lambda width: pl.BlockSpec((ts, width), lambda i: (i, 0))
    return pl.pallas_call(
        functools.partial(_merge_kernel, final=final),
        grid=(n // ts,),
        in_specs=[row(d)] + [row(y.shape[1]) for y in ys] + [_layer_weights(w_all, layer), _resident((1, d))],
        out_specs=row(d),
        out_shape=jax.ShapeDtypeStruct((n, d), F32),
        compiler_params=_cparams(),
        name="merge",
    )(x, *ys, w_all, fg)


def _stack_heads(q, n_groups, width):
    grp = lax.broadcasted_iota(jnp.int32, q.shape, 1) // width
    return jnp.concatenate([jnp.where(grp == g, q, jnp.zeros_like(q)) for g in range(n_groups)], axis=0)


def _cached_attention(qs, kt_c, vt_c, k_n, v_n, bias_c, bias_n):
    s_c = jnp.dot(qs, kt_c, preferred_element_type=F32)
    s_n = _nt_dot(qs, k_n)
    if bias_c is not None:
        s_c, s_n = s_c + bias_c, s_n + bias_n
    m = jnp.maximum(jnp.max(s_c, axis=-1, keepdims=True), jnp.max(s_n, axis=-1, keepdims=True))
    e_c = jnp.exp2(s_c - m)
    e_n = jnp.exp2(s_n - m)
    l = jnp.sum(e_c, axis=-1, keepdims=True) + jnp.sum(e_n, axis=-1, keepdims=True)
    o = _nt_dot(e_c.astype(BF16), vt_c) + jnp.dot(e_n.astype(BF16), v_n, preferred_element_type=F32)
    return o / l


def _sample_kernel(lam_ref, g_ref, bc_ref, bn_ref,
                   qa_ref, ka_ref, va_ref, qc_ref, kc_ref, vc_ref, gate_ref,
                   cak_ref, cav_ref, cck_ref, ccv_ref, conv_ref, cw_ref,
                   y_ref, ul_ref, *, lam_init):
    t = qa_ref.shape[0]
    qs = _stack_heads(qa_ref[...], A_HEADS, HEAD_DIM)
    of = _cached_attention(qs, cak_ref[0, 0].astype(BF16), cav_ref[0, 0].astype(BF16),
                           ka_ref[...], va_ref[...], bc_ref[...], bn_ref[...])
    grp = lax.broadcasted_iota(jnp.int32, (t, A_WIDTH), 1) // HEAD_DIM
    oa = jnp.zeros((t, A_WIDTH), F32)
    for h in range(A_HEADS):
        oa = jnp.where(grp == h, of[h * t:(h + 1) * t], oa)
    qs = _stack_heads(qc_ref[...], C_MAPS, C_QK_DIM)
    of = _cached_attention(qs, cck_ref[0, 0].astype(BF16), ccv_ref[0, 0].astype(BF16),
                           kc_ref[...], vc_ref[...], None, None)
    lam = _diff_lambda(lam_ref, lam_init)
    grp = lax.broadcasted_iota(jnp.int32, (t, C_WIDTH), 1) // C_V_DIM
    d = jnp.zeros((t, C_WIDTH), F32)
    for h in range(C_HEADS):
        dh = of[2 * h * t:(2 * h + 1) * t] - lam * of[(2 * h + 1) * t:(2 * h + 2) * t]
        d = jnp.where(grp == h, dh, d)
    oc = jnp.concatenate([_subln(d[:, p * LANES:(p + 1) * LANES], g_ref[...], lam_init)
                          for p in range(C_WIDTH // LANES)], axis=-1)
    gate = gate_ref[...]
    u = _conv_input(gate)
    prev = conv_ref[0]
    u2, u1 = _conv_taps(u, prev[0:1], prev[1:2])
    y_ref[...] = _gated_mix(gate, oa, oc, u2, u1, u, cw_ref[...])
    ul_ref[0] = u[t - 8:]


def _sample_step(layer, lam4, g2, bias_c, bias_n, gate, qa, ka, va, qc, kc, vc,
                 cak_t, cav_t, cck_t, ccv_t, conv, cw, t, lam_init):
    n = gate.shape[0]
    d_mix = A_WIDTH + B_WIDTH + C_WIDTH
    nb = n // t
    row = lambda width: pl.BlockSpec((t, width), lambda b: (b, 0))
    cache = lambda a: pl.BlockSpec((1, 1) + a.shape[2:], lambda b: (layer, b, 0, 0))
    return pl.pallas_call(
        functools.partial(_sample_kernel, lam_init=lam_init),
        grid=(nb,),
        in_specs=[_resident((4, C_QK_DIM)), _resident((1, LANES)),
                  _resident(bias_c.shape), _resident(bias_n.shape),
                  row(A_WIDTH), row(A_WIDTH), row(A_WIDTH),
                  row(C_WIDTH), row(C_WIDTH), row(C_WIDTH), row(GATE_W),
                  cache(cak_t), cache(cav_t), cache(cck_t), cache(ccv_t),
                  pl.BlockSpec((1,) + conv.shape[1:], lambda b: (b, 0, 0)),
                  _resident(cw.shape)],
        out_specs=(row(d_mix), pl.BlockSpec((1, 8, B_WIDTH), lambda b: (b, 0, 0))),
        out_shape=(jax.ShapeDtypeStruct((n, d_mix), BF16),
                   jax.ShapeDtypeStruct((nb, 8, B_WIDTH), F32)),
        compiler_params=_cparams(),
        name="sample_step",
    )(lam4, g2, bias_c, bias_n, qa, ka, va, qc, kc, vc, gate,
      cak_t, cav_t, cck_t, ccv_t, conv, cw)


def _feature_major(cache):
    nd = cache.ndim
    c = jnp.transpose(cache, (0, 1) + tuple(range(3, nd)) + (2,))
    return c.reshape(c.shape[0], c.shape[1], -1, c.shape[-1])


def kernel(x_prompt, x_sample, cache_a_k, cache_a_v, state_conv, cache_c_k, cache_c_v, norm_g, w_in, w_out, rel_bias, conv_w, lam_q1, lam_k1, lam_q2, lam_k2, subln_g, final_g):
    depth = norm_g.shape[0]
    batch, seq, d_model = x_prompt.shape
    nb, t, _ = x_sample.shape
    past = cache_c_k.shape[2]
    win = cache_a_k.shape[2]
    ts = TS_PROMPT
    assert batch == 1 and win == A_WIN and seq % T_C == 0 and seq % ts == 0 and seq >= WIN_A
    assert t % 16 == 0 and t <= CHUNK and past % LANES == 0
    keep = min(A_WIN, seq)
    assert keep % ts == 0

    xp = x_prompt.reshape(seq, d_model)
    xs = x_sample.reshape(nb * t, d_model)
    rope_p = _rope_angle_tables(jnp.arange(0, seq, ts), jnp.arange(ts))
    rope_s = _rope_angle_tables(jnp.full((1,), past), jnp.tile(jnp.arange(t), nb))
    w_in_b = w_in.astype(BF16)
    w_out_b = w_out.astype(BF16)
    fg = final_g.reshape(1, d_model).astype(F32)
    cak_t, cav_t = _feature_major(cache_a_k), _feature_major(cache_a_v)
    cck_t, ccv_t = _feature_major(cache_c_k), _feature_major(cache_c_v)

    outs = {k: [] for k in ("pak", "pav", "pcv", "sak", "sav", "scv", "sck", "scc")}
    kv_acc = None
    pending = None
    for l in range(depth):
        final = l == depth - 1
        lam_init = 0.8 - 0.6 * math.exp(-0.3 * l)
        g = norm_g[l].reshape(1, d_model).astype(F32)
        cw = conv_w[l].astype(F32)
        lam4 = jnp.stack([lam_q1[l], lam_k1[l], lam_q2[l], lam_k2[l]]).astype(F32)
        g2 = jnp.tile(subln_g[l].astype(F32), LANES // C_V_DIM).reshape(1, LANES)
        bias = _bias_tables(rel_bias[l])

        res = _inproj(xp, g, w_in_b, l, *rope_p, ts, keep, cw, kv_acc, pending)
        acg, akv, qa, ka, va, qc, kc, vc, ck_all, cv_all, yb, tail = res[:12]
        if pending is not None:
            xp = res[12]
        kv_acc = (ck_all, cv_all)
        ya = _band_attention(qa, ka, va, bias, acg)
        yc = _diff_attention(lam4, g2, qc, kc, vc, acg, lam_init)
        pending = ((ya, yb, yc), w_out_b, l)
        outs["pak"].append(akv[:, :A_WIDTH].reshape(1, keep, A_HEADS, HEAD_DIM))
        outs["pav"].append(akv[:, A_WIDTH:].reshape(1, keep, A_HEADS, HEAD_DIM))
        outs["pcv"].append(tail[8 - (CONV_WIDTH - 1):][None])

        gate, akv, qa, ka, va, qc, kc, vc, ckr, cvr = _inproj(
            xs, g, w_in_b, l, *rope_s, nb * t, nb * t)
        unmasked = bias[N_BIAS - 1, :, :t, :A_WIN + t].reshape(A_HEADS * t, A_WIN + t)
        ys, ul = _sample_step(
            l, lam4, g2, unmasked[:, :A_WIN], unmasked[:, A_WIN:], gate, qa, ka, va, qc, kc, vc,
            cak_t, cav_t, cck_t, ccv_t, state_conv[l], cw, t, lam_init)
        xs = _merge(xs, (ys,), w_out_b, l, fg, nb * t, final)
        outs["sak"].append(akv[:, :A_WIDTH].reshape(nb, t, A_HEADS, HEAD_DIM))
        outs["sav"].append(akv[:, A_WIDTH:].reshape(nb, t, A_HEADS, HEAD_DIM))
        outs["scv"].append(ul[:, 8 - (CONV_WIDTH - 1):])
        outs["sck"].append(ckr.reshape(nb, t, C_HEADS, 2, C_QK_DIM))
        outs["scc"].append(cvr.reshape(nb, t, C_HEADS, C_V_DIM))

    xp = _merge(xp, pending[0], w_out_b, depth - 1, fg, TS_MERGE, True)
    st = lambda k: jnp.stack(outs[k])
    ck_all, cv_all = kv_acc
    pck = jnp.transpose(ck_all.reshape(depth, C_HEADS, 2, C_QK_DIM, seq), (0, 4, 1, 2, 3))[:, None]
    pcc = jnp.transpose(cv_all.reshape(depth, C_HEADS, C_V_DIM, seq), (0, 3, 1, 2))[:, None]
    return (xp.reshape(batch, seq, d_model), xs.reshape(nb, t, d_model),
            st("pak"), st("pav"), st("pcv"), pck, pcc,
            st("sak"), st("sav"), st("scv"), st("sck"), st("scc"))
```

```python
import functools
import math

import numpy as np
import jax
import jax.numpy as jnp
from jax import lax
from jax.experimental import pallas as pl
from jax.experimental.pallas import tpu as pltpu

F32 = jnp.float32
BF16 = jnp.bfloat16

CHUNK = 64
HEAD_DIM = 64
A_HEADS = 8
A_WIDTH = A_HEADS * HEAD_DIM
A_PREV_CHUNKS = 8
A_WIN = A_PREV_CHUNKS * CHUNK
REL_CLIP = 128
B_WIDTH = 256
CONV_WIDTH = 3
C_HEADS = 4
C_QK_DIM = 32
C_V_DIM = 64
C_WIDTH = C_HEADS * C_V_DIM
C_MAPS = 2 * C_HEADS
ROPE_DIMS = 8
ROPE_THETA = 500000.0
NORM_EPS = 1e-6
SUBLN_EPS = 1e-5
D_IN_PROJ = 4096
COL_AQ, COL_AK, COL_AV, COL_AG = 0, 512, 1024, 1536
COL_B = 2048
COL_CQ, COL_CK, COL_CV, COL_CG = 3072, 3328, 3584, 3840
GATE_W = A_WIDTH + 4 * B_WIDTH + C_WIDTH
G_AG, G_B, G_CG = 0, A_WIDTH, A_WIDTH + 4 * B_WIDTH

LANES = 128
HALF = LANES // 2
NEG = -1e30
LOG2E = math.log2(math.e)
A_QSCALE = HEAD_DIM ** -0.5 * LOG2E
C_QSCALE = C_QK_DIM ** -0.5 * LOG2E
VMEM_LIMIT = 48 * 1024 * 1024

TS_PROMPT = 512
TS_MERGE = 1024
TQ_A = 256
WIN_A = A_WIN + TQ_A
ROLL_A = 1024
N_BIAS = 3
T_C = 512
CACHE_RING = 3


def _nt_dot(a, b):
    return lax.dot_general(a, b, (((1,), (1,)), ((), ())), preferred_element_type=F32)


def _normalize_pair(acc_even, acc_odd, lo):
    o = jnp.where(lo, acc_even, acc_odd)
    l = jnp.where(lo, acc_odd, acc_even)
    return o / pltpu.roll(l, HALF, 1)


def _cparams(n_axes=1):
    return pltpu.CompilerParams(dimension_semantics=("parallel",) * n_axes,
                                vmem_limit_bytes=VMEM_LIMIT)


def _resident(shape):
    nd = len(shape)
    return pl.BlockSpec(shape, lambda *_: (0,) * nd, pipeline_mode=pl.Buffered(1))


def _layer_weights(w_all, layer):
    return pl.BlockSpec((1,) + w_all.shape[1:], lambda *_: (layer, 0, 0), pipeline_mode=pl.Buffered(1))


def _rope(x, cos, sa, sb):
    return (x * cos + pltpu.roll(x, LANES - ROPE_DIMS // 2, 1) * sa
            + pltpu.roll(x, ROPE_DIMS // 2, 1) * sb)


def _rope_coefficients(base, rt_ref):
    cb, sb_ = base[0:1], base[1:2]
    ct, st = rt_ref[0], rt_ref[1]
    cos_p = cb * ct - sb_ * st
    sin_p = sb_ * ct + cb * st
    d = lax.broadcasted_iota(jnp.int32, cos_p.shape, 1) % C_QK_DIM
    half = ROPE_DIMS // 2
    cos = jnp.where(d < ROPE_DIMS, cos_p, 1.0)
    sa = jnp.where(d < half, -sin_p, 0.0)
    sb = jnp.where((d >= half) & (d < ROPE_DIMS), sin_p, 0.0)
    return cos, sa, sb


def _inproj_kernel(*refs, prompt, n_aliased, zero_rest, merged):
    x_ref, g_ref, w_ref, rb_ref, rt_ref = refs[:5]
    if prompt:
        cw_ref = refs[5]
        n_merge_in = 4 if merged else 0
        y_refs, wo_ref = refs[6:6 + n_merge_in - 1], refs[6 + n_merge_in - 1]
        outs = refs[6 + n_merge_in + n_aliased:]
        (acg_ref, akv_ref, qa_ref, ka_ref, va_ref, qc_ref, kc_ref, vc_ref, ckr_ref, cvr_ref,
         yb_ref, tail_ref) = outs[:12]
        carry_ref = outs[-1]
    else:
        gate_ref, akv_ref, qa_ref, ka_ref, va_ref, qc_ref, kc_ref, vc_ref, ckr_ref, cvr_ref = refs[5:]

    if prompt:
        @pl.when(pl.program_id(0) == 0)
        def _():
            carry_ref[...] = jnp.zeros(carry_ref.shape, F32)

    x = x_ref[...]
    if merged:
        y = jnp.concatenate([r[...] for r in y_refs], axis=-1)
        x = x + jnp.dot(y, wo_ref[0], preferred_element_type=F32)
        outs[12][...] = x
    ms = jnp.mean(x * x, axis=-1, keepdims=True)
    h = (x * lax.rsqrt(ms + NORM_EPS) * g_ref[...]).astype(BF16)
    cos, sa, sb = _rope_coefficients(rb_ref[0], rt_ref)

    def proj(lo, width):
        return jnp.dot(h, w_ref[0, :, lo:lo + width], preferred_element_type=F32)

    def put_f32(ref, t, val):
        if prompt:
            ref[0, t * LANES:(t + 1) * LANES, :] = val.T
            if zero_rest:
                ref[1:, t * LANES:(t + 1) * LANES, :] = jnp.zeros(
                    (ref.shape[0] - 1, LANES, val.shape[0]), F32)
        else:
            ref[:, t * LANES:(t + 1) * LANES] = val

    zc = proj(COL_CQ, 2 * C_WIDTH)
    for t in range(C_WIDTH // LANES):
        cq = _rope(zc[:, t * LANES:(t + 1) * LANES], cos, sa, sb)
        qc_ref[:, t * LANES:(t + 1) * LANES] = (cq * C_QSCALE).astype(BF16)
        ck = _rope(zc[:, C_WIDTH + t * LANES:C_WIDTH + (t + 1) * LANES], cos, sa, sb)
        kc_ref[:, t * LANES:(t + 1) * LANES] = ck.astype(BF16)
        put_f32(ckr_ref, t, ck)
    zc = proj(COL_CV, 2 * C_WIDTH)
    vc_ref[...] = zc[:, :C_WIDTH].astype(BF16)
    for t in range(C_WIDTH // LANES):
        put_f32(cvr_ref, t, zc[:, t * LANES:(t + 1) * LANES])
    if prompt:
        acg_ref[:, A_WIDTH:] = zc[:, C_WIDTH:]
    else:
        gate_ref[:, G_CG:G_CG + C_WIDTH] = zc[:, C_WIDTH:]

    zb = proj(COL_B, 4 * B_WIDTH)
    if prompt:
        bw = B_WIDTH
        u = zb[:, bw:2 * bw] * zb[:, 2 * bw:3 * bw]
        prev = carry_ref[...]
        u2, u1 = _conv_taps(u, prev[6:7], prev[7:8])
        cw = cw_ref[...]
        conv = u2 * cw[0:1] + u1 * cw[1:2] + u * cw[2:3]
        yb_ref[...] = (zb[:, :bw] * conv * _silu(zb[:, 3 * bw:])).astype(BF16)
        carry_ref[...] = u[u.shape[0] - 8:]
        tail_ref[...] = u[u.shape[0] - 8:]
    else:
        gate_ref[:, G_B:G_B + 4 * B_WIDTH] = zb

    zc = proj(COL_AK, A_WIDTH)
    ka_ref[...] = zc.astype(BF16)
    akv_ref[:, :A_WIDTH] = zc
    zc = proj(COL_AV, A_WIDTH)
    va_ref[...] = zc.astype(BF16)
    akv_ref[:, A_WIDTH:] = zc
    zc = proj(COL_AG, A_WIDTH)
    if prompt:
        acg_ref[:, :A_WIDTH] = zc
    else:
        gate_ref[:, G_AG:G_AG + A_WIDTH] = zc
    qa_ref[...] = (proj(COL_AQ, A_WIDTH) * A_QSCALE).astype(BF16)


def _inproj(x, g, w_all, layer, rope_base, rope_off, ts, keep, cw=None, kv_acc=None, merge_in=None):
    prompt = cw is not None
    depth = w_all.shape[0]
    n, d = x.shape
    nt = n // ts
    first_keep = nt - keep // ts
    row = lambda width: pl.BlockSpec((ts, width), lambda i: (i, 0))
    sds = jax.ShapeDtypeStruct
    if prompt:
        kv_shape = (depth, C_WIDTH, n)
        slabs = depth if kv_acc is None else 1
        kv_spec = pl.BlockSpec((slabs, C_WIDTH, ts), lambda i: (layer, 0, i))
        gate_w = A_WIDTH + C_WIDTH
    else:
        kv_shape, kv_spec = (n, C_WIDTH), row(C_WIDTH)
        gate_w = GATE_W
    akv_spec = pl.BlockSpec((ts, 2 * A_WIDTH), lambda i: (jnp.maximum(i - first_keep, 0), 0))
    out_shape = [sds((n, gate_w), F32), sds((keep, 2 * A_WIDTH), F32),
                 sds((n, A_WIDTH), BF16), sds((n, A_WIDTH), BF16), sds((n, A_WIDTH), BF16),
                 sds((n, C_WIDTH), BF16), sds((n, C_WIDTH), BF16), sds((n, C_WIDTH), BF16),
                 sds(kv_shape, F32), sds(kv_shape, F32)]
    out_specs = [row(gate_w), akv_spec, row(A_WIDTH), row(A_WIDTH), row(A_WIDTH),
                 row(C_WIDTH), row(C_WIDTH), row(C_WIDTH), kv_spec, kv_spec]
    in_specs = [row(d), _resident((1, d)),
                _layer_weights(w_all, layer),
                pl.BlockSpec((1, 2, LANES), lambda i: (i, 0, 0)), _resident(rope_off.shape)]
    args = [x, g, w_all, rope_base, rope_off]
    scratch = []
    aliases = {}
    if prompt:
        in_specs.append(_resident(cw.shape))
        args.append(cw)
        if merge_in is not None:
            ys, wo_all, prev_layer = merge_in
            in_specs += [row(y.shape[1]) for y in ys] + [_layer_weights(wo_all, prev_layer)]
            args += list(ys) + [wo_all]
        if kv_acc is not None:
            aliases = {len(args) + j: 8 + j for j in range(len(kv_acc))}
            in_specs += [pl.BlockSpec(memory_space=pl.ANY)] * len(kv_acc)
            args += list(kv_acc)
        out_shape += [sds((n, B_WIDTH), BF16), sds((8, B_WIDTH), F32)]
        out_specs += [row(B_WIDTH), pl.BlockSpec((8, B_WIDTH), lambda i: (0, 0))]
        if merge_in is not None:
            out_shape.append(sds((n, d), F32))
            out_specs.append(row(d))
        scratch = [pltpu.VMEM((8, B_WIDTH), F32)]
    return pl.pallas_call(
        functools.partial(_inproj_kernel, prompt=prompt, n_aliased=len(aliases),
                          zero_rest=prompt and kv_acc is None and depth > 1,
                          merged=merge_in is not None),
        grid=(nt,),
        in_specs=in_specs,
        out_specs=tuple(out_specs),
        out_shape=tuple(out_shape),
        scratch_shapes=scratch,
        input_output_aliases=aliases,
        compiler_params=pltpu.CompilerParams(dimension_semantics=("arbitrary",),
                                             vmem_limit_bytes=VMEM_LIMIT),
        name="inproj",
    )(*args)


def _rope_angle_tables(tile_base, row_off):
    half = ROPE_DIMS // 2
    inv_freq = ROPE_THETA ** (-jnp.arange(half, dtype=F32) * (2.0 / ROPE_DIMS))
    lane_freq = inv_freq[(np.arange(LANES) % C_QK_DIM) % half][None, :]
    a = tile_base.astype(F32)[:, None] * lane_freq
    b = row_off.astype(F32)[:, None] * lane_freq
    return jnp.stack([jnp.cos(a), jnp.sin(a)], axis=1), jnp.stack([jnp.cos(b), jnp.sin(b)], axis=0)


def _bias_kernel(u_ref, b_ref):
    v = pl.program_id(0)
    qc = lax.broadcasted_iota(jnp.int32, (TQ_A, WIN_A), 0) // CHUNK
    kc = lax.broadcasted_iota(jnp.int32, (TQ_A, WIN_A), 1) // CHUNK
    top = qc + v * (TQ_A // CHUNK)
    valid = (kc <= top) & (kc >= top - A_PREV_CHUNKS)
    for h in range(A_HEADS):
        u = jnp.broadcast_to(u_ref[0, h], (TQ_A, ROLL_A))
        t = pltpu.roll(u, 0, 1, stride=1, stride_axis=0)[:, ROLL_A - WIN_A:]
        b_ref[0, h] = jnp.where(valid, t * LOG2E, NEG)


def _bias_tables(rel_bias_l):
    tab = rel_bias_l.astype(F32)
    rows = []
    for off in (0, TQ_A, A_WIN):
        n_hi = off + (ROLL_A - WIN_A) - REL_CLIP
        n_lo = ROLL_A - n_hi - (2 * REL_CLIP + 1)
        rows.append(jnp.concatenate([jnp.broadcast_to(tab[:, -1:], (A_HEADS, n_hi)), tab[:, ::-1],
                                     jnp.broadcast_to(tab[:, :1], (A_HEADS, n_lo))], axis=1))
    u = jnp.stack(rows)[:, :, None, :]
    return pl.pallas_call(
        _bias_kernel,
        grid=(N_BIAS,),
        in_specs=[pl.BlockSpec((1, A_HEADS, 1, ROLL_A), lambda v: (v, 0, 0, 0))],
        out_specs=pl.BlockSpec((1, A_HEADS, TQ_A, WIN_A), lambda v: (v, 0, 0, 0)),
        out_shape=jax.ShapeDtypeStruct((N_BIAS, A_HEADS, TQ_A, WIN_A), F32),
        compiler_params=_cparams(),
        name="bias",
    )(u)


def _band_kernel(q_ref, k0_ref, k1_ref, k2_ref, v0_ref, v1_ref, v2_ref, b_ref, ag_ref, o_ref, s_ref):
    i = pl.program_id(0)
    n_pair = A_WIDTH // LANES
    lo = lax.broadcasted_iota(jnp.int32, (TQ_A, LANES), 1) < HALF
    lo_w = lax.broadcasted_iota(jnp.int32, (WIN_A, LANES), 1) < HALF

    @pl.when(i == 0)
    def _():
        s_ref[1] = jnp.zeros(s_ref.shape[1:], F32)

    def probabilities(slot, h, steady):
        if not steady:
            s = s_ref[slot, h * TQ_A:(h + 1) * TQ_A, :] + b_ref[0, h]
            return jnp.exp2(s - jnp.max(s, axis=-1, keepdims=True)).astype(BF16)
        parts = []
        for a in range(TQ_A // CHUNK):
            first = (a * CHUNK) // LANES
            last = ((a + A_PREV_CHUNKS) * CHUNK) // LANES
            cols = slice(first * LANES, (last + 1) * LANES)
            sa = (s_ref[slot, h * TQ_A + a * CHUNK:h * TQ_A + (a + 1) * CHUNK, cols]
                  + b_ref[0, h, a * CHUNK:(a + 1) * CHUNK, cols])
            ea = jnp.exp2(sa - jnp.max(sa, axis=-1, keepdims=True)).astype(BF16)
            pads = [jnp.zeros((CHUNK, first * LANES), BF16), ea,
                    jnp.zeros((CHUNK, WIN_A - (last + 1) * LANES), BF16)]
            parts.append(jnp.concatenate([x for x in pads if x.shape[1]], axis=1))
        return jnp.concatenate(parts, axis=0)

    def step(cur, steady):
        for p in range(n_pair):
            cols = slice(p * LANES, (p + 1) * LANES)
            q2 = q_ref[:, cols]
            qs = jnp.concatenate([jnp.where(lo, q2, jnp.zeros_like(q2)),
                                  jnp.where(lo, jnp.zeros_like(q2), q2)], axis=0)
            kw = jnp.concatenate([r[:, cols] for r in (k0_ref, k1_ref, k2_ref)], axis=0)
            s_ref[cur, 2 * p * TQ_A:(2 * p + 2) * TQ_A, :] = _nt_dot(qs, kw)
            vw = jnp.concatenate([r[:, cols] for r in (v0_ref, v1_ref, v2_ref)], axis=0)
            outs = []
            for half in range(2):
                h = 2 * p + half
                sel_w = lo_w if half == 0 else jnp.logical_not(lo_w)
                v_aug = jnp.where(sel_w, vw, jnp.ones_like(vw))
                e = probabilities(1 - cur, h, steady)
                outs.append(jnp.dot(e, v_aug, preferred_element_type=F32))
            o_ref[:, cols] = (_normalize_pair(outs[0], outs[1], lo) * _silu(ag_ref[:, cols])).astype(BF16)

    parity = jnp.bitwise_and(i, 1)
    steady = i > A_WIN // TQ_A
    for cur in range(2):
        pl.when(jnp.logical_and(parity == cur, steady))(functools.partial(step, cur, True))
        pl.when(jnp.logical_and(parity == cur, jnp.logical_not(steady)))(functools.partial(step, cur, False))


def _band_attention(qa, ka, va, bias, acg):
    s = qa.shape[0]
    nq = s // TQ_A
    q_tile = lambda i: jnp.minimum(i, nq - 1)
    o_tile = lambda i: jnp.maximum(i - 1, 0)
    first = lambda tile: jnp.maximum(tile - A_WIN // TQ_A, 0)
    k_spec = lambda j: pl.BlockSpec((TQ_A, A_WIDTH), lambda i: (first(q_tile(i)) + j, 0))
    v_spec = lambda j: pl.BlockSpec((TQ_A, A_WIDTH), lambda i: (first(o_tile(i)) + j, 0))
    return pl.pallas_call(
        _band_kernel,
        grid=(nq + 1,),
        in_specs=[pl.BlockSpec((TQ_A, A_WIDTH), lambda i: (q_tile(i), 0)),
                  k_spec(0), k_spec(1), k_spec(2), v_spec(0), v_spec(1), v_spec(2),
                  pl.BlockSpec((1, A_HEADS, TQ_A, WIN_A),
                               lambda i: (jnp.minimum(o_tile(i), N_BIAS - 1), 0, 0, 0)),
                  pl.BlockSpec((TQ_A, A_WIDTH), lambda i: (o_tile(i), 0))],
        out_specs=pl.BlockSpec((TQ_A, A_WIDTH), lambda i: (o_tile(i), 0)),
        out_shape=jax.ShapeDtypeStruct((s, A_WIDTH), BF16),
        scratch_shapes=[pltpu.VMEM((2, A_HEADS * TQ_A, WIN_A), F32)],
        compiler_params=pltpu.CompilerParams(dimension_semantics=("arbitrary",),
                                             vmem_limit_bytes=VMEM_LIMIT),
        name="band_attn",
    )(qa, ka, ka, ka, va, va, va, bias, acg)


def _diff_lambda(lam_ref, lam_init):
    a = jnp.sum(lam_ref[0:1, :] * lam_ref[1:2, :], axis=-1, keepdims=True)
    b = jnp.sum(lam_ref[2:3, :] * lam_ref[3:4, :], axis=-1, keepdims=True)
    return jnp.exp(a) - jnp.exp(b) + lam_init


def _subln(d, g2, lam_init):
    lo = lax.broadcasted_iota(jnp.int32, d.shape, 1) < HALF
    sq = d * d
    s_lo = jnp.sum(jnp.where(lo, sq, 0.0), axis=-1, keepdims=True)
    s_hi = jnp.sum(jnp.where(lo, 0.0, sq), axis=-1, keepdims=True)
    ms = jnp.where(lo, s_lo, s_hi) * (1.0 / C_V_DIM)
    return d * lax.rsqrt(ms + SUBLN_EPS) * g2 * (1.0 - lam_init)


def _diff_kernel(lam_ref, g_ref, q_ref, k_ref, v_ref, cg_ref, o_ref, qm_ref, s_ref, m_ref, acc_ref,
                 *, lam_init):
    i = pl.program_id(0)
    t = T_C
    n_grp = C_HEADS // 2
    per_grp = C_MAPS // n_grp
    q = q_ref[...]
    lane_grp = lax.broadcasted_iota(jnp.int32, (t, C_WIDTH), 1) // C_QK_DIM
    lo = lax.broadcasted_iota(jnp.int32, (t, LANES), 1) < HALF

    def mask_queries(g):
        for loc in range(per_grp):
            hm = g * per_grp + loc
            qm_ref[g, loc * t:(loc + 1) * t, :] = jnp.where(lane_grp == hm, q, jnp.zeros_like(q))

    def k_tile(j):
        return k_ref[pl.ds(pl.multiple_of(j * t, t), t), :]

    def v_tile(j):
        return v_ref[pl.ds(pl.multiple_of(j * t, t), t), :]

    def scores(g, k_t):
        s_ref[g] = _nt_dot(qm_ref[g], k_t)

    def diagonal_probs(s, m_old):
        per_tile = LANES // CHUNK
        part_lane = lax.broadcasted_iota(jnp.int32, (CHUNK, LANES), 1)
        m_parts, p_parts = [], []
        for a in range(t // CHUNK):
            rows = slice(a * CHUNK, (a + 1) * CHUNK)
            n_full, part = divmod(a + 1, per_tile)
            pieces = [s[rows, :n_full * LANES]] if n_full else []
            if part:
                edge = s[rows, n_full * LANES:(n_full + 1) * LANES]
                pieces.append(jnp.where(part_lane < part * CHUNK, edge, -jnp.inf))
            sa = jnp.concatenate(pieces, axis=1) if len(pieces) > 1 else pieces[0]
            width = sa.shape[1]
            mn = jnp.maximum(m_old[rows], jnp.max(sa, axis=-1, keepdims=True))
            pa = jnp.exp2(sa - jnp.tile(mn, (1, width // LANES))).astype(BF16)
            if width < t:
                pa = jnp.concatenate([pa, jnp.zeros((CHUNK, t - width), BF16)], axis=1)
            m_parts.append(mn)
            p_parts.append(pa)
        return jnp.concatenate(m_parts, axis=0), jnp.concatenate(p_parts, axis=0)

    def softmax_pv(g, v_t, diagonal):
        v2 = v_t[:, g * LANES:(g + 1) * LANES]
        for half in range(2):
            h = 2 * g + half
            sel = lo if half == 0 else jnp.logical_not(lo)
            v_aug = jnp.where(sel, v2, jnp.ones_like(v2))
            ps, alphas = [], []
            for mp in range(2):
                hm = 2 * h + mp
                loc = hm - g * per_grp
                s = s_ref[g, loc * t:(loc + 1) * t, :]
                m_old = m_ref[hm]
                if diagonal:
                    m_new, p = diagonal_probs(s, m_old)
                else:
                    m_new = jnp.maximum(m_old, jnp.max(s, axis=-1, keepdims=True))
                    p = jnp.exp2(s - jnp.tile(m_new, (1, t // LANES))).astype(BF16)
                alphas.append(jnp.exp2(m_old - m_new))
                ps.append(p)
                m_ref[hm] = m_new
            pv = jnp.dot(jnp.concatenate(ps, axis=0), v_aug, preferred_element_type=F32)
            for mp in range(2):
                hm = 2 * h + mp
                acc_ref[hm] = alphas[mp] * acc_ref[hm] + pv[mp * t:(mp + 1) * t]

    def finalize(p_, lam):
        r = [_normalize_pair(acc_ref[4 * p_ + mp], acc_ref[4 * p_ + 2 + mp], lo) for mp in range(2)]
        pair = r[0] - lam * r[1]
        tile = slice(p_ * LANES, (p_ + 1) * LANES)
        o_ref[:, tile] = (_subln(pair, g_ref[...], lam_init) * _silu(cg_ref[:, tile])).astype(BF16)

    mask_queries(0)
    scores(0, k_tile(0))
    mask_queries(1)
    m_ref[...] = jnp.full(m_ref.shape, -jnp.inf, F32)
    acc_ref[...] = jnp.zeros(acc_ref.shape, F32)

    def full_tile(j):
        v_t = v_tile(j)
        scores(1, k_tile(j))
        softmax_pv(0, v_t, False)
        scores(0, k_tile(j + 1))
        softmax_pv(1, v_t, False)

    odd = jnp.bitwise_and(i, 1)

    @pl.when(odd == 1)
    def _():
        full_tile(0)

    def body(jj, carry):
        j = odd + 2 * jj
        full_tile(j)
        full_tile(j + 1)
        return carry

    lax.fori_loop(0, lax.shift_right_logical(i, 1), body, 0)
    v_t = v_tile(i)
    lam = _diff_lambda(lam_ref, lam_init)
    scores(1, k_tile(i))
    softmax_pv(0, v_t, True)
    finalize(0, lam)
    softmax_pv(1, v_t, True)
    finalize(1, lam)


def _diff_attention(lam4, g2, qc, kc, vc, acg, lam_init):
    s = qc.shape[0]
    t = T_C
    n_grp = C_HEADS // 2
    return pl.pallas_call(
        functools.partial(_diff_kernel, lam_init=lam_init),
        grid=(s // t,),
        in_specs=[_resident((4, C_QK_DIM)), _resident((1, LANES)),
                  pl.BlockSpec((t, C_WIDTH), lambda i: (i, 0)),
                  _resident((s, C_WIDTH)), _resident((s, C_WIDTH)),
                  pl.BlockSpec((t, C_WIDTH), lambda i: (i, A_WIDTH // C_WIDTH))],
        out_specs=pl.BlockSpec((t, C_WIDTH), lambda i: (i, 0)),
        out_shape=jax.ShapeDtypeStruct((s, C_WIDTH), BF16),
        scratch_shapes=[pltpu.VMEM((n_grp, C_MAPS // n_grp * t, C_WIDTH), BF16),
                        pltpu.VMEM((n_grp, C_MAPS // n_grp * t, t), F32),
                        pltpu.VMEM((C_MAPS, t, LANES), F32),
                        pltpu.VMEM((C_MAPS, t, LANES), F32)],
        compiler_params=_cparams(),
        name="diff_attn",
    )(lam4, g2, qc, kc, vc, acg)


def _silu(t):
    return t * jax.nn.sigmoid(t)


def _conv_input(gate):
    b = gate[:, G_B:G_B + 4 * B_WIDTH]
    return b[:, B_WIDTH:2 * B_WIDTH] * b[:, 2 * B_WIDTH:3 * B_WIDTH]


def _conv_taps(u, prev0, prev1):
    row = lax.broadcasted_iota(jnp.int32, u.shape, 0)
    u1 = jnp.where(row == 0, prev1, pltpu.roll(u, 1, 0))
    u2 = jnp.where(row == 0, prev0, jnp.where(row == 1, prev1, pltpu.roll(u, 2, 0)))
    return u2, u1


def _gated_mix(gate, oa, oc, u2, u1, u, cw):
    bw = B_WIDTH
    conv = u2 * cw[0:1] + u1 * cw[1:2] + u * cw[2:3]
    ob = gate[:, G_B:G_B + bw] * conv
    y = jnp.concatenate([oa * _silu(gate[:, G_AG:G_AG + A_WIDTH]),
                         ob * _silu(gate[:, G_B + 3 * bw:G_B + 4 * bw]),
                         oc * _silu(gate[:, G_CG:G_CG + C_WIDTH])], axis=-1)
    return y.astype(BF16)


def _final_norm(x, fg):
    ms = jnp.mean(x * x, axis=-1, keepdims=True)
    return x * lax.rsqrt(ms + NORM_EPS) * fg


def _merge_kernel(*refs, final):
    x_ref, *y_refs, w_ref, fg_ref, xo_ref = refs
    y = jnp.concatenate([r[...] for r in y_refs], axis=-1) if len(y_refs) > 1 else y_refs[0][...]
    out = x_ref[...] + jnp.dot(y, w_ref[0], preferred_element_type=F32)
    xo_ref[...] = _final_norm(out, fg_ref[...]) if final else out


def _merge(x, ys, w_all, layer, fg, ts, final):
    n, d = x.shape
    row = lambda width: pl.BlockSpec((ts, width), lambda i: (i, 0))
    return pl.pallas_call(
        functools.partial(_merge_kernel, final=final),
        grid=(n // ts,),
        in_specs=[row(d)] + [row(y.shape[1]) for y in ys] + [_layer_weights(w_all, layer), _resident((1, d))],
        out_specs=row(d),
        out_shape=jax.ShapeDtypeStruct((n, d), F32),
        compiler_params=_cparams(),
        name="merge",
    )(x, *ys, w_all, fg)


def _stack_heads(q, n_groups, width):
    grp = lax.broadcasted_iota(jnp.int32, q.shape, 1) // width
    return jnp.concatenate([jnp.where(grp == g, q, jnp.zeros_like(q)) for g in range(n_groups)], axis=0)


def _cached_attention(qs, kt_c, vt_c, k_n, v_n, bias_c, bias_n):
    s_c = jnp.dot(qs, kt_c, preferred_element_type=F32)
    s_n = _nt_dot(qs, k_n)
    if bias_c is not None:
        s_c, s_n = s_c + bias_c, s_n + bias_n
    m = jnp.maximum(jnp.max(s_c, axis=-1, keepdims=True), jnp.max(s_n, axis=-1, keepdims=True))
    e_c = jnp.exp2(s_c - m)
    e_n = jnp.exp2(s_n - m)
    l = jnp.sum(e_c, axis=-1, keepdims=True) + jnp.sum(e_n, axis=-1, keepdims=True)
    o = _nt_dot(e_c.astype(BF16), vt_c) + jnp.dot(e_n.astype(BF16), v_n, preferred_element_type=F32)
    return o / l


def _sample_kernel(lam_ref, g_ref, bc_ref, bn_ref,
                   qa_ref, ka_ref, va_ref, qc_ref, kc_ref, vc_ref, gate_ref,
                   cak_hbm, cav_hbm, cck_hbm, ccv_hbm, conv_ref, cw_ref,
                   y_ref, ul_ref, cak_buf, cav_buf, cck_buf, ccv_buf, sem, *, lam_init, layer, nb):
    b = pl.program_id(0)
    streams = ((cak_hbm, cak_buf), (cav_hbm, cav_buf), (cck_hbm, cck_buf), (ccv_hbm, ccv_buf))

    def fetch(step, slot):
        return [pltpu.make_async_copy(hbm.at[layer, step], buf.at[slot], sem.at[slot, j])
                for j, (hbm, buf) in enumerate(streams)]

    @pl.when(b == 0)
    def _():
        for s in range(min(CACHE_RING - 1, nb)):
            for c in fetch(s, s):
                c.start()

    ahead = b + (CACHE_RING - 1)

    @pl.when(ahead < nb)
    def _():
        for c in fetch(ahead, ahead % CACHE_RING):
            c.start()

    slot = b % CACHE_RING
    for c in fetch(b, slot):
        c.wait()
    t = qa_ref.shape[0]
    qs = _stack_heads(qa_ref[...], A_HEADS, HEAD_DIM)
    of = _cached_attention(qs, cak_buf[slot].astype(BF16), cav_buf[slot].astype(BF16),
                           ka_ref[...], va_ref[...], bc_ref[...], bn_ref[...])
    grp = lax.broadcasted_iota(jnp.int32, (t, A_WIDTH), 1) // HEAD_DIM
    oa = jnp.zeros((t, A_WIDTH), F32)
    for h in range(A_HEADS):
        oa = jnp.where(grp == h, of[h * t:(h + 1) * t], oa)
    qs = _stack_heads(qc_ref[...], C_MAPS, C_QK_DIM)
    of = _cached_attention(qs, cck_buf[slot].astype(BF16), ccv_buf[slot].astype(BF16),
                           kc_ref[...], vc_ref[...], None, None)
    lam = _diff_lambda(lam_ref, lam_init)
    grp = lax.broadcasted_iota(jnp.int32, (t, C_WIDTH), 1) // C_V_DIM
    d = jnp.zeros((t, C_WIDTH), F32)
    for h in range(C_HEADS):
        dh = of[2 * h * t:(2 * h + 1) * t] - lam * of[(2 * h + 1) * t:(2 * h + 2) * t]
        d = jnp.where(grp == h, dh, d)
    oc = jnp.concatenate([_subln(d[:, p * LANES:(p + 1) * LANES], g_ref[...], lam_init)
                          for p in range(C_WIDTH // LANES)], axis=-1)
    gate = gate_ref[...]
    u = _conv_input(gate)
    prev = conv_ref[0]
    u2, u1 = _conv_taps(u, prev[0:1], prev[1:2])
    y_ref[...] = _gated_mix(gate, oa, oc, u2, u1, u, cw_ref[...])
    ul_ref[0] = u[t - 8:]


def _sample_step(layer, lam4, g2, bias_c, bias_n, gate, qa, ka, va, qc, kc, vc,
                 cak_t, cav_t, cck_t, ccv_t, conv, cw, t, lam_init):
    n = gate.shape[0]
    d_mix = A_WIDTH + B_WIDTH + C_WIDTH
    nb = n // t
    row = lambda width: pl.BlockSpec((t, width), lambda b: (b, 0))
    in_hbm = pl.BlockSpec(memory_space=pl.ANY)
    ring = lambda a: pltpu.VMEM((CACHE_RING,) + a.shape[2:], a.dtype)
    return pl.pallas_call(
        functools.partial(_sample_kernel, lam_init=lam_init, layer=layer, nb=nb),
        grid=(nb,),
        in_specs=[_resident((4, C_QK_DIM)), _resident((1, LANES)),
                  _resident(bias_c.shape), _resident(bias_n.shape),
                  row(A_WIDTH), row(A_WIDTH), row(A_WIDTH),
                  row(C_WIDTH), row(C_WIDTH), row(C_WIDTH), row(GATE_W),
                  in_hbm, in_hbm, in_hbm, in_hbm,
                  pl.BlockSpec((1,) + conv.shape[1:], lambda b: (b, 0, 0)),
                  _resident(cw.shape)],
        out_specs=(row(d_mix), pl.BlockSpec((1, 8, B_WIDTH), lambda b: (b, 0, 0))),
        out_shape=(jax.ShapeDtypeStruct((n, d_mix), BF16),
                   jax.ShapeDtypeStruct((nb, 8, B_WIDTH), F32)),
        scratch_shapes=[ring(cak_t), ring(cav_t), ring(cck_t), ring(ccv_t),
                        pltpu.SemaphoreType.DMA((CACHE_RING, 4))],
        compiler_params=pltpu.CompilerParams(dimension_semantics=("arbitrary",),
                                             vmem_limit_bytes=VMEM_LIMIT),
        name="sample_step",
    )(lam4, g2, bias_c, bias_n, qa, ka, va, qc, kc, vc, gate,
      cak_t, cav_t, cck_t, ccv_t, conv, cw)


def _feature_major(cache):
    nd = cache.ndim
    c = jnp.transpose(cache, (0, 1) + tuple(range(3, nd)) + (2,))
    return c.reshape(c.shape[0], c.shape[1], -1, c.shape[-1])


def kernel(x_prompt, x_sample, cache_a_k, cache_a_v, state_conv, cache_c_k, cache_c_v, norm_g, w_in, w_out, rel_bias, conv_w, lam_q1, lam_k1, lam_q2, lam_k2, subln_g, final_g):
    depth = norm_g.shape[0]
    batch, seq, d_model = x_prompt.shape
    nb, t, _ = x_sample.shape
    past = cache_c_k.shape[2]
    win = cache_a_k.shape[2]
    ts = TS_PROMPT
    assert batch == 1 and win == A_WIN and seq % T_C == 0 and seq % ts == 0 and seq >= WIN_A
    assert t % 16 == 0 and t <= CHUNK and past % LANES == 0
    keep = min(A_WIN, seq)
    assert keep % ts == 0

    xp = x_prompt.reshape(seq, d_model)
    xs = x_sample.reshape(nb * t, d_model)
    rope_p = _rope_angle_tables(jnp.arange(0, seq, ts), jnp.arange(ts))
    rope_s = _rope_angle_tables(jnp.full((1,), past), jnp.tile(jnp.arange(t), nb))
    w_in_b = w_in.astype(BF16)
    w_out_b = w_out.astype(BF16)
    fg = final_g.reshape(1, d_model).astype(F32)
    cak_t, cav_t = _feature_major(cache_a_k), _feature_major(cache_a_v)
    cck_t, ccv_t = _feature_major(cache_c_k), _feature_major(cache_c_v)

    outs = {k: [] for k in ("pak", "pav", "pcv", "sak", "sav", "scv", "sck", "scc")}
    kv_acc = None
    pending = None
    for l in range(depth):
        final = l == depth - 1
        lam_init = 0.8 - 0.6 * math.exp(-0.3 * l)
        g = norm_g[l].reshape(1, d_model).astype(F32)
        cw = conv_w[l].astype(F32)
        lam4 = jnp.stack([lam_q1[l], lam_k1[l], lam_q2[l], lam_k2[l]]).astype(F32)
        g2 = jnp.tile(subln_g[l].astype(F32), LANES // C_V_DIM).reshape(1, LANES)
        bias = _bias_tables(rel_bias[l])

        res = _inproj(xp, g, w_in_b, l, *rope_p, ts, keep, cw, kv_acc, pending)
        acg, akv, qa, ka, va, qc, kc, vc, ck_all, cv_all, yb, tail = res[:12]
        if pending is not None:
            xp = res[12]
        kv_acc = (ck_all, cv_all)
        ya = _band_attention(qa, ka, va, bias, acg)
        yc = _diff_attention(lam4, g2, qc, kc, vc, acg, lam_init)
        pending = ((ya, yb, yc), w_out_b, l)
        outs["pak"].append(akv[:, :A_WIDTH].reshape(1, keep, A_HEADS, HEAD_DIM))
        outs["pav"].append(akv[:, A_WIDTH:].reshape(1, keep, A_HEADS, HEAD_DIM))
        outs["pcv"].append(tail[8 - (CONV_WIDTH - 1):][None])

        gate, akv, qa, ka, va, qc, kc, vc, ckr, cvr = _inproj(
            xs, g, w_in_b, l, *rope_s, nb * t, nb * t)
        unmasked = bias[N_BIAS - 1, :, :t, :A_WIN + t].reshape(A_HEADS * t, A_WIN + t)
        ys, ul = _sample_step(
            l, lam4, g2, unmasked[:, :A_WIN], unmasked[:, A_WIN:], gate, qa, ka, va, qc, kc, vc,
            cak_t, cav_t, cck_t, ccv_t, state_conv[l], cw, t, lam_init)
        xs = _merge(xs, (ys,), w_out_b, l, fg, nb * t, final)
        outs["sak"].append(akv[:, :A_WIDTH].reshape(nb, t, A_HEADS, HEAD_DIM))
        outs["sav"].append(akv[:, A_WIDTH:].reshape(nb, t, A_HEADS, HEAD_DIM))
        outs["scv"].append(ul[:, 8 - (CONV_WIDTH - 1):])
        outs["sck"].append(ckr.reshape(nb, t, C_HEADS, 2, C_QK_DIM))
        outs["scc"].append(cvr.reshape(nb, t, C_HEADS, C_V_DIM))

    xp = _merge(xp, pending[0], w_out_b, depth - 1, fg, TS_MERGE, True)
    st = lambda k: jnp.stack(outs[k])
    ck_all, cv_all = kv_acc
    pck = jnp.transpose(ck_all.reshape(depth, C_HEADS, 2, C_QK_DIM, seq), (0, 4, 1, 2, 3))[:, None]
    pcc = jnp.transpose(cv_all.reshape(depth, C_HEADS, C_V_DIM, seq), (0, 3, 1, 2))[:, None]
    return (xp.reshape(batch, seq, d_model), xs.reshape(nb, t, d_model),
            st("pak"), st("pav"), st("pcv"), pck, pcc,
            st("sak"), st("sav"), st("scv"), st("sck"), st("scc"))
```

```python
import functools
import math

import numpy as np
import jax
import jax.numpy as jnp
from jax import lax
from jax.experimental import pallas as pl
from jax.experimental.pallas import tpu as pltpu

F32 = jnp.float32
BF16 = jnp.bfloat16

CHUNK = 64
HEAD_DIM = 64
A_HEADS = 8
A_WIDTH = A_HEADS * HEAD_DIM
A_PREV_CHUNKS = 8
A_WIN = A_PREV_CHUNKS * CHUNK
REL_CLIP = 128
B_WIDTH = 256
CONV_WIDTH = 3
C_HEADS = 4
C_QK_DIM = 32
C_V_DIM = 64
C_WIDTH = C_HEADS * C_V_DIM
C_MAPS = 2 * C_HEADS
ROPE_DIMS = 8
ROPE_THETA = 500000.0
NORM_EPS = 1e-6
SUBLN_EPS = 1e-5
D_IN_PROJ = 4096
COL_AQ, COL_AK, COL_AV, COL_AG = 0, 512, 1024, 1536
COL_B = 2048
COL_CQ, COL_CK, COL_CV, COL_CG = 3072, 3328, 3584, 3840
GATE_W = A_WIDTH + 4 * B_WIDTH + C_WIDTH
G_AG, G_B, G_CG = 0, A_WIDTH, A_WIDTH + 4 * B_WIDTH

LANES = 128
HALF = LANES // 2
NEG = -1e30
LOG2E = math.log2(math.e)
A_QSCALE = HEAD_DIM ** -0.5 * LOG2E
C_QSCALE = C_QK_DIM ** -0.5 * LOG2E
VMEM_LIMIT = 48 * 1024 * 1024

TS_PROMPT = 512
TS_MERGE = 1024
TQ_A = 256
WIN_A = A_WIN + TQ_A
ROLL_A = 1024
N_BIAS = 3
T_C = 512
CACHE_RING = 3


def _nt_dot(a, b):
    return lax.dot_general(a, b, (((1,), (1,)), ((), ())), preferred_element_type=F32)


def _normalize_pair(acc_even, acc_odd, lo):
    o = jnp.where(lo, acc_even, acc_odd)
    l = jnp.where(lo, acc_odd, acc_even)
    return o / pltpu.roll(l, HALF, 1)


def _cparams(n_axes=1):
    return pltpu.CompilerParams(dimension_semantics=("parallel",) * n_axes,
                                vmem_limit_bytes=VMEM_LIMIT)


def _resident(shape):
    nd = len(shape)
    return pl.BlockSpec(shape, lambda *_: (0,) * nd, pipeline_mode=pl.Buffered(1))


def _layer_weights(w_all, layer):
    return pl.BlockSpec((1,) + w_all.shape[1:], lambda *_: (layer, 0, 0), pipeline_mode=pl.Buffered(1))


def _rope(x, cos, sa, sb):
    return (x * cos + pltpu.roll(x, LANES - ROPE_DIMS // 2, 1) * sa
            + pltpu.roll(x, ROPE_DIMS // 2, 1) * sb)


def _rope_coefficients(base, rt_ref):
    cb, sb_ = base[0:1], base[1:2]
    ct, st = rt_ref[0], rt_ref[1]
    cos_p = cb * ct - sb_ * st
    sin_p = sb_ * ct + cb * st
    d = lax.broadcasted_iota(jnp.int32, cos_p.shape, 1) % C_QK_DIM
    half = ROPE_DIMS // 2
    cos = jnp.where(d < ROPE_DIMS, cos_p, 1.0)
    sa = jnp.where(d < half, -sin_p, 0.0)
    sb = jnp.where((d >= half) & (d < ROPE_DIMS), sin_p, 0.0)
    return cos, sa, sb


def _inproj_kernel(*refs, prompt, n_aliased, zero_rest, merged):
    x_ref, g_ref, w_ref, rb_ref, rt_ref = refs[:5]
    if prompt:
        cw_ref = refs[5]
        n_merge_in = 4 if merged else 0
        y_refs, wo_ref = refs[6:6 + n_merge_in - 1], refs[6 + n_merge_in - 1]
        outs = refs[6 + n_merge_in + n_aliased:]
        (acg_ref, akv_ref, qa_ref, ka_ref, va_ref, qc_ref, kc_ref, vc_ref, ckr_ref, cvr_ref,
         yb_ref, tail_ref) = outs[:12]
        carry_ref = outs[-1]
    else:
        gate_ref, akv_ref, qa_ref, ka_ref, va_ref, qc_ref, kc_ref, vc_ref, ckr_ref, cvr_ref = refs[5:]

    if prompt:
        @pl.when(pl.program_id(0) == 0)
        def _():
            carry_ref[...] = jnp.zeros(carry_ref.shape, F32)

    x = x_ref[...]
    if merged:
        y = jnp.concatenate([r[...] for r in y_refs], axis=-1)
        x = x + jnp.dot(y, wo_ref[0], preferred_element_type=F32)
        outs[12][...] = x
    ms = jnp.mean(x * x, axis=-1, keepdims=True)
    h = (x * lax.rsqrt(ms + NORM_EPS) * g_ref[...]).astype(BF16)
    cos, sa, sb = _rope_coefficients(rb_ref[0], rt_ref)

    def proj(lo, width):
        return jnp.dot(h, w_ref[0, :, lo:lo + width], preferred_element_type=F32)

    def put_f32(ref, t, val):
        if prompt:
            ref[0, t * LANES:(t + 1) * LANES, :] = val.T
            if zero_rest:
                ref[1:, t * LANES:(t + 1) * LANES, :] = jnp.zeros(
                    (ref.shape[0] - 1, LANES, val.shape[0]), F32)
        else:
            ref[:, t * LANES:(t + 1) * LANES] = val

    zc = proj(COL_CQ, 2 * C_WIDTH)
    for t in range(C_WIDTH // LANES):
        cq = _rope(zc[:, t * LANES:(t + 1) * LANES], cos, sa, sb)
        qc_ref[:, t * LANES:(t + 1) * LANES] = (cq * C_QSCALE).astype(BF16)
        ck = _rope(zc[:, C_WIDTH + t * LANES:C_WIDTH + (t + 1) * LANES], cos, sa, sb)
        kc_ref[:, t * LANES:(t + 1) * LANES] = ck.astype(BF16)
        put_f32(ckr_ref, t, ck)
    zc = proj(COL_CV, 2 * C_WIDTH)
    vc_ref[...] = zc[:, :C_WIDTH].astype(BF16)
    for t in range(C_WIDTH // LANES):
        put_f32(cvr_ref, t, zc[:, t * LANES:(t + 1) * LANES])
    if prompt:
        acg_ref[:, A_WIDTH:] = zc[:, C_WIDTH:]
    else:
        gate_ref[:, G_CG:G_CG + C_WIDTH] = zc[:, C_WIDTH:]

    zb = proj(COL_B, 4 * B_WIDTH)
    if prompt:
        bw = B_WIDTH
        u = zb[:, bw:2 * bw] * zb[:, 2 * bw:3 * bw]
        prev = carry_ref[...]
        u2, u1 = _conv_taps(u, prev[6:7], prev[7:8])
        cw = cw_ref[...]
        conv = u2 * cw[0:1] + u1 * cw[1:2] + u * cw[2:3]
        yb_ref[...] = (zb[:, :bw] * conv * _silu(zb[:, 3 * bw:])).astype(BF16)
        carry_ref[...] = u[u.shape[0] - 8:]
        tail_ref[...] = u[u.shape[0] - 8:]
    else:
        gate_ref[:, G_B:G_B + 4 * B_WIDTH] = zb

    zc = proj(COL_AK, A_WIDTH)
    ka_ref[...] = zc.astype(BF16)
    akv_ref[:, :A_WIDTH] = zc
    zc = proj(COL_AV, A_WIDTH)
    va_ref[...] = zc.astype(BF16)
    akv_ref[:, A_WIDTH:] = zc
    zc = proj(COL_AG, A_WIDTH)
    if prompt:
        acg_ref[:, :A_WIDTH] = zc
    else:
        gate_ref[:, G_AG:G_AG + A_WIDTH] = zc
    qa_ref[...] = (proj(COL_AQ, A_WIDTH) * A_QSCALE).astype(BF16)


def _inproj(x, g, w_all, layer, rope_base, rope_off, ts, keep, cw=None, kv_acc=None, merge_in=None):
    prompt = cw is not None
    depth = w_all.shape[0]
    n, d = x.shape
    nt = n // ts
    first_keep = nt - keep // ts
    row = lambda width: pl.BlockSpec((ts, width), lambda i: (i, 0))
    sds = jax.ShapeDtypeStruct
    if prompt:
        kv_shape = (depth, C_WIDTH, n)
        slabs = depth if kv_acc is None else 1
        kv_spec = pl.BlockSpec((slabs, C_WIDTH, ts), lambda i: (layer, 0, i))
        gate_w = A_WIDTH + C_WIDTH
    else:
        kv_shape, kv_spec = (n, C_WIDTH), row(C_WIDTH)
        gate_w = GATE_W
    akv_spec = pl.BlockSpec((ts, 2 * A_WIDTH), lambda i: (jnp.maximum(i - first_keep, 0), 0))
    out_shape = [sds((n, gate_w), F32), sds((keep, 2 * A_WIDTH), F32),
                 sds((n, A_WIDTH), BF16), sds((n, A_WIDTH), BF16), sds((n, A_WIDTH), BF16),
                 sds((n, C_WIDTH), BF16), sds((n, C_WIDTH), BF16), sds((n, C_WIDTH), BF16),
                 sds(kv_shape, F32), sds(kv_shape, F32)]
    out_specs = [row(gate_w), akv_spec, row(A_WIDTH), row(A_WIDTH), row(A_WIDTH),
                 row(C_WIDTH), row(C_WIDTH), row(C_WIDTH), kv_spec, kv_spec]
    in_specs = [row(d), _resident((1, d)),
                _layer_weights(w_all, layer),
                pl.BlockSpec((1, 2, LANES), lambda i: (i, 0, 0)), _resident(rope_off.shape)]
    args = [x, g, w_all, rope_base, rope_off]
    scratch = []
    aliases = {}
    if prompt:
        in_specs.append(_resident(cw.shape))
        args.append(cw)
        if merge_in is not None:
            ys, wo_all, prev_layer = merge_in
            in_specs += [row(y.shape[1]) for y in ys] + [_layer_weights(wo_all, prev_layer)]
            args += list(ys) + [wo_all]
        if kv_acc is not None:
            aliases = {len(args) + j: 8 + j for j in range(len(kv_acc))}
            in_specs += [pl.BlockSpec(memory_space=pl.ANY)] * len(kv_acc)
            args += list(kv_acc)
        out_shape += [sds((n, B_WIDTH), BF16), sds((8, B_WIDTH), F32)]
        out_specs += [row(B_WIDTH), pl.BlockSpec((8, B_WIDTH), lambda i: (0, 0))]
        if merge_in is not None:
            out_shape.append(sds((n, d), F32))
            out_specs.append(row(d))
        scratch = [pltpu.VMEM((8, B_WIDTH), F32)]
    return pl.pallas_call(
        functools.partial(_inproj_kernel, prompt=prompt, n_aliased=len(aliases),
                          zero_rest=prompt and kv_acc is None and depth > 1,
                          merged=merge_in is not None),
        grid=(nt,),
        in_specs=in_specs,
        out_specs=tuple(out_specs),
        out_shape=tuple(out_shape),
        scratch_shapes=scratch,
        input_output_aliases=aliases,
        compiler_params=pltpu.CompilerParams(dimension_semantics=("arbitrary",),
                                             vmem_limit_bytes=VMEM_LIMIT),
        name="inproj",
    )(*args)


def _rope_angle_tables(tile_base, row_off):
    half = ROPE_DIMS // 2
    inv_freq = ROPE_THETA ** (-jnp.arange(half, dtype=F32) * (2.0 / ROPE_DIMS))
    lane_freq = inv_freq[(np.arange(LANES) % C_QK_DIM) % half][None, :]
    a = tile_base.astype(F32)[:, None] * lane_freq
    b = row_off.astype(F32)[:, None] * lane_freq
    return jnp.stack([jnp.cos(a), jnp.sin(a)], axis=1), jnp.stack([jnp.cos(b), jnp.sin(b)], axis=0)


def _bias_kernel(u_ref, b_ref):
    v = pl.program_id(0)
    qc = lax.broadcasted_iota(jnp.int32, (TQ_A, WIN_A), 0) // CHUNK
    kc = lax.broadcasted_iota(jnp.int32, (TQ_A, WIN_A), 1) // CHUNK
    top = qc + v * (TQ_A // CHUNK)
    valid = (kc <= top) & (kc >= top - A_PREV_CHUNKS)
    for h in range(A_HEADS):
        u = jnp.broadcast_to(u_ref[0, h], (TQ_A, ROLL_A))
        t = pltpu.roll(u, 0, 1, stride=1, stride_axis=0)[:, ROLL_A - WIN_A:]
        b_ref[0, h] = jnp.where(valid, t * LOG2E, NEG)


def _bias_tables(rel_bias_l):
    tab = rel_bias_l.astype(F32)
    rows = []
    for off in (0, TQ_A, A_WIN):
        n_hi = off + (ROLL_A - WIN_A) - REL_CLIP
        n_lo = ROLL_A - n_hi - (2 * REL_CLIP + 1)
        rows.append(jnp.concatenate([jnp.broadcast_to(tab[:, -1:], (A_HEADS, n_hi)), tab[:, ::-1],
                                     jnp.broadcast_to(tab[:, :1], (A_HEADS, n_lo))], axis=1))
    u = jnp.stack(rows)[:, :, None, :]
    return pl.pallas_call(
        _bias_kernel,
        grid=(N_BIAS,),
        in_specs=[pl.BlockSpec((1, A_HEADS, 1, ROLL_A), lambda v: (v, 0, 0, 0))],
        out_specs=pl.BlockSpec((1, A_HEADS, TQ_A, WIN_A), lambda v: (v, 0, 0, 0)),
        out_shape=jax.ShapeDtypeStruct((N_BIAS, A_HEADS, TQ_A, WIN_A), F32),
        compiler_params=_cparams(),
        name="bias",
    )(u)


def _band_kernel(q_ref, k0_ref, k1_ref, k2_ref, v0_ref, v1_ref, v2_ref, b_ref, ag_ref, o_ref, s_ref):
    i = pl.program_id(0)
    n_pair = A_WIDTH // LANES
    lo = lax.broadcasted_iota(jnp.int32, (TQ_A, LANES), 1) < HALF
    lo_w = lax.broadcasted_iota(jnp.int32, (WIN_A, LANES), 1) < HALF

    @pl.when(i == 0)
    def _():
        s_ref[1] = jnp.zeros(s_ref.shape[1:], F32)

    def probabilities(slot, h, steady):
        if not steady:
            s = s_ref[slot, h * TQ_A:(h + 1) * TQ_A, :] + b_ref[0, h]
            return jnp.exp2(s - jnp.max(s, axis=-1, keepdims=True)).astype(BF16)
        parts = []
        for a in range(TQ_A // CHUNK):
            first = (a * CHUNK) // LANES
            last = ((a + A_PREV_CHUNKS) * CHUNK) // LANES
            cols = slice(first * LANES, (last + 1) * LANES)
            sa = (s_ref[slot, h * TQ_A + a * CHUNK:h * TQ_A + (a + 1) * CHUNK, cols]
                  + b_ref[0, h, a * CHUNK:(a + 1) * CHUNK, cols])
            ea = jnp.exp2(sa - jnp.max(sa, axis=-1, keepdims=True)).astype(BF16)
            pads = [jnp.zeros((CHUNK, first * LANES), BF16), ea,
                    jnp.zeros((CHUNK, WIN_A - (last + 1) * LANES), BF16)]
            parts.append(jnp.concatenate([x for x in pads if x.shape[1]], axis=1))
        return jnp.concatenate(parts, axis=0)

    def step(cur, steady):
        for p in range(n_pair):
            cols = slice(p * LANES, (p + 1) * LANES)
            q2 = q_ref[:, cols]
            qs = jnp.concatenate([jnp.where(lo, q2, jnp.zeros_like(q2)),
                                  jnp.where(lo, jnp.zeros_like(q2), q2)], axis=0)
            kw = jnp.concatenate([r[:, cols] for r in (k0_ref, k1_ref, k2_ref)], axis=0)
            s_ref[cur, 2 * p * TQ_A:(2 * p + 2) * TQ_A, :] = _nt_dot(qs, kw)
            vw = jnp.concatenate([r[:, cols] for r in (v0_ref, v1_ref, v2_ref)], axis=0)
            outs = []
            for half in range(2):
                h = 2 * p + half
                sel_w = lo_w if half == 0 else jnp.logical_not(lo_w)
                v_aug = jnp.where(sel_w, vw, jnp.ones_like(vw))
                e = probabilities(1 - cur, h, steady)
                outs.append(jnp.dot(e, v_aug, preferred_element_type=F32))
            o_ref[:, cols] = (_normalize_pair(outs[0], outs[1], lo) * _silu(ag_ref[:, cols])).astype(BF16)

    parity = jnp.bitwise_and(i, 1)
    steady = i > A_WIN // TQ_A
    for cur in range(2):
        pl.when(jnp.logical_and(parity == cur, steady))(functools.partial(step, cur, True))
        pl.when(jnp.logical_and(parity == cur, jnp.logical_not(steady)))(functools.partial(step, cur, False))


def _band_attention(qa, ka, va, bias, acg):
    s = qa.shape[0]
    nq = s // TQ_A
    q_tile = lambda i: jnp.minimum(i, nq - 1)
    o_tile = lambda i: jnp.maximum(i - 1, 0)
    first = lambda tile: jnp.maximum(tile - A_WIN // TQ_A, 0)
    k_spec = lambda j: pl.BlockSpec((TQ_A, A_WIDTH), lambda i: (first(q_tile(i)) + j, 0))
    v_spec = lambda j: pl.BlockSpec((TQ_A, A_WIDTH), lambda i: (first(o_tile(i)) + j, 0))
    return pl.pallas_call(
        _band_kernel,
        grid=(nq + 1,),
        in_specs=[pl.BlockSpec((TQ_A, A_WIDTH), lambda i: (q_tile(i), 0)),
                  k_spec(0), k_spec(1), k_spec(2), v_spec(0), v_spec(1), v_spec(2),
                  pl.BlockSpec((1, A_HEADS, TQ_A, WIN_A),
                               lambda i: (jnp.minimum(o_tile(i), N_BIAS - 1), 0, 0, 0)),
                  pl.BlockSpec((TQ_A, A_WIDTH), lambda i: (o_tile(i), 0))],
        out_specs=pl.BlockSpec((TQ_A, A_WIDTH), lambda i: (o_tile(i), 0)),
        out_shape=jax.ShapeDtypeStruct((s, A_WIDTH), BF16),
        scratch_shapes=[pltpu.VMEM((2, A_HEADS * TQ_A, WIN_A), F32)],
        compiler_params=pltpu.CompilerParams(dimension_semantics=("arbitrary",),
                                             vmem_limit_bytes=VMEM_LIMIT),
        name="band_attn",
    )(qa, ka, ka, ka, va, va, va, bias, acg)


def _diff_lambda(lam_ref, lam_init):
    a = jnp.sum(lam_ref[0:1, :] * lam_ref[1:2, :], axis=-1, keepdims=True)
    b = jnp.sum(lam_ref[2:3, :] * lam_ref[3:4, :], axis=-1, keepdims=True)
    return jnp.exp(a) - jnp.exp(b) + lam_init


def _subln(d, g2, lam_init):
    lo = lax.broadcasted_iota(jnp.int32, d.shape, 1) < HALF
    sq = d * d
    s_lo = jnp.sum(jnp.where(lo, sq, 0.0), axis=-1, keepdims=True)
    s_hi = jnp.sum(jnp.where(lo, 0.0, sq), axis=-1, keepdims=True)
    ms = jnp.where(lo, s_lo, s_hi) * (1.0 / C_V_DIM)
    return d * lax.rsqrt(ms + SUBLN_EPS) * g2 * (1.0 - lam_init)


def _diff_kernel(lam_ref, g_ref, q_ref, k_ref, v_ref, cg_ref, o_ref, qm_ref, s_ref, m_ref, acc_ref,
                 *, lam_init):
    i = pl.program_id(0)
    t = T_C
    n_grp = C_HEADS // 2
    per_grp = C_MAPS // n_grp
    q = q_ref[...]
    lane_grp = lax.broadcasted_iota(jnp.int32, (t, C_WIDTH), 1) // C_QK_DIM
    lo = lax.broadcasted_iota(jnp.int32, (t, LANES), 1) < HALF

    def mask_queries(g):
        for loc in range(per_grp):
            hm = g * per_grp + loc
            qm_ref[g, loc * t:(loc + 1) * t, :] = jnp.where(lane_grp == hm, q, jnp.zeros_like(q))

    def k_tile(j):
        return k_ref[pl.ds(pl.multiple_of(j * t, t), t), :]

    def v_tile(j):
        return v_ref[pl.ds(pl.multiple_of(j * t, t), t), :]

    def scores(g, k_t):
        s_ref[g] = _nt_dot(qm_ref[g], k_t)

    def diagonal_probs(s, m_old):
        per_tile = LANES // CHUNK
        part_lane = lax.broadcasted_iota(jnp.int32, (CHUNK, LANES), 1)
        m_parts, p_parts = [], []
        for a in range(t // CHUNK):
            rows = slice(a * CHUNK, (a + 1) * CHUNK)
            n_full, part = divmod(a + 1, per_tile)
            pieces = [s[rows, :n_full * LANES]] if n_full else []
            if part:
                edge = s[rows, n_full * LANES:(n_full + 1) * LANES]
                pieces.append(jnp.where(part_lane < part * CHUNK, edge, -jnp.inf))
            sa = jnp.concatenate(pieces, axis=1) if len(pieces) > 1 else pieces[0]
            width = sa.shape[1]
            mn = jnp.maximum(m_old[rows], jnp.max(sa, axis=-1, keepdims=True))
            pa = jnp.exp2(sa - jnp.tile(mn, (1, width // LANES))).astype(BF16)
            if width < t:
                pa = jnp.concatenate([pa, jnp.zeros((CHUNK, t - width), BF16)], axis=1)
            m_parts.append(mn)
            p_parts.append(pa)
        return jnp.concatenate(m_parts, axis=0), jnp.concatenate(p_parts, axis=0)

    def softmax_pv(g, v_t, diagonal):
        v2 = v_t[:, g * LANES:(g + 1) * LANES]
        for half in range(2):
            h = 2 * g + half
            sel = lo if half == 0 else jnp.logical_not(lo)
            v_aug = jnp.where(sel, v2, jnp.ones_like(v2))
            ps, alphas = [], []
            for mp in range(2):
                hm = 2 * h + mp
                loc = hm - g * per_grp
                s = s_ref[g, loc * t:(loc + 1) * t, :]
                m_old = m_ref[hm]
                if diagonal:
                    m_new, p = diagonal_probs(s, m_old)
                else:
                    m_new = jnp.maximum(m_old, jnp.max(s, axis=-1, keepdims=True))
                    p = jnp.exp2(s - jnp.tile(m_new, (1, t // LANES))).astype(BF16)
                alphas.append(jnp.exp2(m_old - m_new))
                ps.append(p)
                m_ref[hm] = m_new
            pv = jnp.dot(jnp.concatenate(ps, axis=0), v_aug, preferred_element_type=F32)
            for mp in range(2):
                hm = 2 * h + mp
                acc_ref[hm] = alphas[mp] * acc_ref[hm] + pv[mp * t:(mp + 1) * t]

    def finalize(p_, lam):
        r = [_normalize_pair(acc_ref[4 * p_ + mp], acc_ref[4 * p_ + 2 + mp], lo) for mp in range(2)]
        pair = r[0] - lam * r[1]
        tile = slice(p_ * LANES, (p_ + 1) * LANES)
        o_ref[:, tile] = (_subln(pair, g_ref[...], lam_init) * _silu(cg_ref[:, tile])).astype(BF16)

    mask_queries(0)
    scores(0, k_tile(0))
    mask_queries(1)
    m_ref[...] = jnp.full(m_ref.shape, -jnp.inf, F32)
    acc_ref[...] = jnp.zeros(acc_ref.shape, F32)

    def full_tile(j):
        v_t = v_tile(j)
        scores(1, k_tile(j))
        softmax_pv(0, v_t, False)
        scores(0, k_tile(j + 1))
        softmax_pv(1, v_t, False)

    odd = jnp.bitwise_and(i, 1)

    @pl.when(odd == 1)
    def _():
        full_tile(0)

    def body(jj, carry):
        j = odd + 2 * jj
        full_tile(j)
        full_tile(j + 1)
        return carry

    lax.fori_loop(0, lax.shift_right_logical(i, 1), body, 0)
    v_t = v_tile(i)
    lam = _diff_lambda(lam_ref, lam_init)
    scores(1, k_tile(i))
    softmax_pv(0, v_t, True)
    finalize(0, lam)
    softmax_pv(1, v_t, True)
    finalize(1, lam)


def _diff_attention(lam4, g2, qc, kc, vc, acg, lam_init):
    s = qc.shape[0]
    t = T_C
    n_grp = C_HEADS // 2
    return pl.pallas_call(
        functools.partial(_diff_kernel, lam_init=lam_init),
        grid=(s // t,),
        in_specs=[_resident((4, C_QK_DIM)), _resident((1, LANES)),
                  pl.BlockSpec((t, C_WIDTH), lambda i: (i, 0)),
                  _resident((s, C_WIDTH)), _resident((s, C_WIDTH)),
                  pl.BlockSpec((t, C_WIDTH), lambda i: (i, A_WIDTH // C_WIDTH))],
        out_specs=pl.BlockSpec((t, C_WIDTH), lambda i: (i, 0)),
        out_shape=jax.ShapeDtypeStruct((s, C_WIDTH), BF16),
        scratch_shapes=[pltpu.VMEM((n_grp, C_MAPS // n_grp * t, C_WIDTH), BF16),
                        pltpu.VMEM((n_grp, C_MAPS // n_grp * t, t), F32),
                        pltpu.VMEM((C_MAPS, t, LANES), F32),
                        pltpu.VMEM((C_MAPS, t, LANES), F32)],
        compiler_params=_cparams(),
        name="diff_attn",
    )(lam4, g2, qc, kc, vc, acg)


def _silu(t):
    return t * jax.nn.sigmoid(t)


def _conv_input(gate):
    b = gate[:, G_B:G_B + 4 * B_WIDTH]
    return b[:, B_WIDTH:2 * B_WIDTH] * b[:, 2 * B_WIDTH:3 * B_WIDTH]


def _conv_taps(u, prev0, prev1):
    row = lax.broadcasted_iota(jnp.int32, u.shape, 0)
    u1 = jnp.where(row == 0, prev1, pltpu.roll(u, 1, 0))
    u2 = jnp.where(row == 0, prev0, jnp.where(row == 1, prev1, pltpu.roll(u, 2, 0)))
    return u2, u1


def _gated_mix(gate, oa, oc, u2, u1, u, cw):
    bw = B_WIDTH
    conv = u2 * cw[0:1] + u1 * cw[1:2] + u * cw[2:3]
    ob = gate[:, G_B:G_B + bw] * conv
    y = jnp.concatenate([oa * _silu(gate[:, G_AG:G_AG + A_WIDTH]),
                         ob * _silu(gate[:, G_B + 3 * bw:G_B + 4 * bw]),
                         oc * _silu(gate[:, G_CG:G_CG + C_WIDTH])], axis=-1)
    return y.astype(BF16)


def _final_norm(x, fg):
    ms = jnp.mean(x * x, axis=-1, keepdims=True)
    return x * lax.rsqrt(ms + NORM_EPS) * fg


def _ring_schedule(step, n_steps, fetch):
    @pl.when(step == 0)
    def _():
        for s in range(min(CACHE_RING - 1, n_steps)):
            for c in fetch(s, s):
                c.start()

    ahead = step + (CACHE_RING - 1)

    @pl.when(ahead < n_steps)
    def _():
        for c in fetch(ahead, ahead % CACHE_RING):
            c.start()

    slot = step % CACHE_RING
    for c in fetch(step, slot):
        c.wait()
    return slot


def _merge_kernel(*refs, final, n_src, ring_steps):
    srcs = refs[:n_src]
    w_ref, fg_ref, xo_ref = refs[n_src:n_src + 3]
    if ring_steps:
        bufs, sem = refs[n_src + 3:2 * n_src + 3], refs[2 * n_src + 3]
        ts = xo_ref.shape[0]

        def fetch(step, slot):
            rows = pl.ds(pl.multiple_of(step * ts, ts), ts)
            return [pltpu.make_async_copy(src.at[rows], buf.at[slot], sem.at[slot, j])
                    for j, (src, buf) in enumerate(zip(srcs, bufs))]

        slot = _ring_schedule(pl.program_id(0), ring_steps, fetch)
        x, ys = bufs[0][slot], [buf[slot] for buf in bufs[1:]]
    else:
        x, ys = srcs[0][...], [r[...] for r in srcs[1:]]
    y = jnp.concatenate(ys, axis=-1) if len(ys) > 1 else ys[0]
    out = x + jnp.dot(y, w_ref[0], preferred_element_type=F32)
    xo_ref[...] = _final_norm(out, fg_ref[...]) if final else out


def _merge(x, ys, w_all, layer, fg, ts, final):
    n, d = x.shape
    steps = n // ts
    srcs = (x,) + tuple(ys)
    row = lambda width: pl.BlockSpec((ts, width), lambda i: (i, 0))
    ring = steps >= CACHE_RING
    if ring:
        src_specs = [pl.BlockSpec(memory_space=pl.ANY)] * len(srcs)
        scratch = [pltpu.VMEM((CACHE_RING, ts, a.shape[1]), a.dtype) for a in srcs]
        scratch.append(pltpu.SemaphoreType.DMA((CACHE_RING, len(srcs))))
    else:
        src_specs, scratch = [row(a.shape[1]) for a in srcs], []
    return pl.pallas_call(
        functools.partial(_merge_kernel, final=final, n_src=len(srcs), ring_steps=steps if ring else 0),
        grid=(steps,),
        in_specs=src_specs + [_layer_weights(w_all, layer), _resident((1, d))],
        out_specs=row(d),
        out_shape=jax.ShapeDtypeStruct((n, d), F32),
        scratch_shapes=scratch,
        compiler_params=pltpu.CompilerParams(dimension_semantics=("arbitrary",),
                                             vmem_limit_bytes=VMEM_LIMIT),
        name="merge",
    )(*srcs, w_all, fg)


def _stack_heads(q, n_groups, width):
    grp = lax.broadcasted_iota(jnp.int32, q.shape, 1) // width
    return jnp.concatenate([jnp.where(grp == g, q, jnp.zeros_like(q)) for g in range(n_groups)], axis=0)


def _cached_attention(qs, kt_c, vt_c, k_n, v_n, bias_c, bias_n):
    s_c = jnp.dot(qs, kt_c, preferred_element_type=F32)
    s_n = _nt_dot(qs, k_n)
    if bias_c is not None:
        s_c, s_n = s_c + bias_c, s_n + bias_n
    m = jnp.maximum(jnp.max(s_c, axis=-1, keepdims=True), jnp.max(s_n, axis=-1, keepdims=True))
    e_c = jnp.exp2(s_c - m)
    e_n = jnp.exp2(s_n - m)
    l = jnp.sum(e_c, axis=-1, keepdims=True) + jnp.sum(e_n, axis=-1, keepdims=True)
    o = _nt_dot(e_c.astype(BF16), vt_c) + jnp.dot(e_n.astype(BF16), v_n, preferred_element_type=F32)
    return o / l


def _sample_kernel(lam_ref, g_ref, bc_ref, bn_ref,
                   qa_ref, ka_ref, va_ref, qc_ref, kc_ref, vc_ref, gate_ref,
                   cak_hbm, cav_hbm, cck_hbm, ccv_hbm, conv_ref, cw_ref,
                   y_ref, ul_ref, cak_buf, cav_buf, cck_buf, ccv_buf, sem, *, lam_init, layer, nb):
    b = pl.program_id(0)
    streams = ((cak_hbm, cak_buf), (cav_hbm, cav_buf), (cck_hbm, cck_buf), (ccv_hbm, ccv_buf))

    def fetch(step, slot):
        return [pltpu.make_async_copy(hbm.at[layer, step], buf.at[slot], sem.at[slot, j])
                for j, (hbm, buf) in enumerate(streams)]

    slot = _ring_schedule(b, nb, fetch)
    t = qa_ref.shape[0]
    qs = _stack_heads(qa_ref[...], A_HEADS, HEAD_DIM)
    of = _cached_attention(qs, cak_buf[slot].astype(BF16), cav_buf[slot].astype(BF16),
                           ka_ref[...], va_ref[...], bc_ref[...], bn_ref[...])
    grp = lax.broadcasted_iota(jnp.int32, (t, A_WIDTH), 1) // HEAD_DIM
    oa = jnp.zeros((t, A_WIDTH), F32)
    for h in range(A_HEADS):
        oa = jnp.where(grp == h, of[h * t:(h + 1) * t], oa)
    qs = _stack_heads(qc_ref[...], C_MAPS, C_QK_DIM)
    of = _cached_attention(qs, cck_buf[slot].astype(BF16), ccv_buf[slot].astype(BF16),
                           kc_ref[...], vc_ref[...], None, None)
    lam = _diff_lambda(lam_ref, lam_init)
    grp = lax.broadcasted_iota(jnp.int32, (t, C_WIDTH), 1) // C_V_DIM
    d = jnp.zeros((t, C_WIDTH), F32)
    for h in range(C_HEADS):
        dh = of[2 * h * t:(2 * h + 1) * t] - lam * of[(2 * h + 1) * t:(2 * h + 2) * t]
        d = jnp.where(grp == h, dh, d)
    oc = jnp.concatenate([_subln(d[:, p * LANES:(p + 1) * LANES], g_ref[...], lam_init)
                          for p in range(C_WIDTH // LANES)], axis=-1)
    gate = gate_ref[...]
    u = _conv_input(gate)
    prev = conv_ref[0]
    u2, u1 = _conv_taps(u, prev[0:1], prev[1:2])
    y_ref[...] = _gated_mix(gate, oa, oc, u2, u1, u, cw_ref[...])
    ul_ref[0] = u[t - 8:]


def _sample_step(layer, lam4, g2, bias_c, bias_n, gate, qa, ka, va, qc, kc, vc,
                 cak_t, cav_t, cck_t, ccv_t, conv, cw, t, lam_init):
    n = gate.shape[0]
    d_mix = A_WIDTH + B_WIDTH + C_WIDTH
    nb = n // t
    row = lambda width: pl.BlockSpec((t, width), lambda b: (b, 0))
    in_hbm = pl.BlockSpec(memory_space=pl.ANY)
    ring = lambda a: pltpu.VMEM((CACHE_RING,) + a.shape[2:], a.dtype)
    return pl.pallas_call(
        functools.partial(_sample_kernel, lam_init=lam_init, layer=layer, nb=nb),
        grid=(nb,),
        in_specs=[_resident((4, C_QK_DIM)), _resident((1, LANES)),
                  _resident(bias_c.shape), _resident(bias_n.shape),
                  row(A_WIDTH), row(A_WIDTH), row(A_WIDTH),
                  row(C_WIDTH), row(C_WIDTH), row(C_WIDTH), row(GATE_W),
                  in_hbm, in_hbm, in_hbm, in_hbm,
                  pl.BlockSpec((1,) + conv.shape[1:], lambda b: (b, 0, 0)),
                  _resident(cw.shape)],
        out_specs=(row(d_mix), pl.BlockSpec((1, 8, B_WIDTH), lambda b: (b, 0, 0))),
        out_shape=(jax.ShapeDtypeStruct((n, d_mix), BF16),
                   jax.ShapeDtypeStruct((nb, 8, B_WIDTH), F32)),
        scratch_shapes=[ring(cak_t), ring(cav_t), ring(cck_t), ring(ccv_t),
                        pltpu.SemaphoreType.DMA((CACHE_RING, 4))],
        compiler_params=pltpu.CompilerParams(dimension_semantics=("arbitrary",),
                                             vmem_limit_bytes=VMEM_LIMIT),
        name="sample_step",
    )(lam4, g2, bias_c, bias_n, qa, ka, va, qc, kc, vc, gate,
      cak_t, cav_t, cck_t, ccv_t, conv, cw)


def _feature_major(cache):
    nd = cache.ndim
    c = jnp.transpose(cache, (0, 1) + tuple(range(3, nd)) + (2,))
    return c.reshape(c.shape[0], c.shape[1], -1, c.shape[-1])


def kernel(x_prompt, x_sample, cache_a_k, cache_a_v, state_conv, cache_c_k, cache_c_v, norm_g, w_in, w_out, rel_bias, conv_w, lam_q1, lam_k1, lam_q2, lam_k2, subln_g, final_g):
    depth = norm_g.shape[0]
    batch, seq, d_model = x_prompt.shape
    nb, t, _ = x_sample.shape
    past = cache_c_k.shape[2]
    win = cache_a_k.shape[2]
    ts = TS_PROMPT
    assert batch == 1 and win == A_WIN and seq % T_C == 0 and seq % ts == 0 and seq >= WIN_A
    assert t % 16 == 0 and t <= CHUNK and past % LANES == 0
    keep = min(A_WIN, seq)
    assert keep % ts == 0

    xp = x_prompt.reshape(seq, d_model)
    xs = x_sample.reshape(nb * t, d_model)
    rope_p = _rope_angle_tables(jnp.arange(0, seq, ts), jnp.arange(ts))
    rope_s = _rope_angle_tables(jnp.full((1,), past), jnp.tile(jnp.arange(t), nb))
    w_in_b = w_in.astype(BF16)
    w_out_b = w_out.astype(BF16)
    fg = final_g.reshape(1, d_model).astype(F32)
    cak_t, cav_t = _feature_major(cache_a_k), _feature_major(cache_a_v)
    cck_t, ccv_t = _feature_major(cache_c_k), _feature_major(cache_c_v)

    outs = {k: [] for k in ("pak", "pav", "pcv", "sak", "sav", "scv", "sck", "scc")}
    kv_acc = None
    pending = None
    for l in range(depth):
        final = l == depth - 1
        lam_init = 0.8 - 0.6 * math.exp(-0.3 * l)
        g = norm_g[l].reshape(1, d_model).astype(F32)
        cw = conv_w[l].astype(F32)
        lam4 = jnp.stack([lam_q1[l], lam_k1[l], lam_q2[l], lam_k2[l]]).astype(F32)
        g2 = jnp.tile(subln_g[l].astype(F32), LANES // C_V_DIM).reshape(1, LANES)
        bias = _bias_tables(rel_bias[l])

        res = _inproj(xp, g, w_in_b, l, *rope_p, ts, keep, cw, kv_acc, pending)
        acg, akv, qa, ka, va, qc, kc, vc, ck_all, cv_all, yb, tail = res[:12]
        if pending is not None:
            xp = res[12]
        kv_acc = (ck_all, cv_all)
        ya = _band_attention(qa, ka, va, bias, acg)
        yc = _diff_attention(lam4, g2, qc, kc, vc, acg, lam_init)
        pending = ((ya, yb, yc), w_out_b, l)
        outs["pak"].append(akv[:, :A_WIDTH].reshape(1, keep, A_HEADS, HEAD_DIM))
        outs["pav"].append(akv[:, A_WIDTH:].reshape(1, keep, A_HEADS, HEAD_DIM))
        outs["pcv"].append(tail[8 - (CONV_WIDTH - 1):][None])

        gate, akv, qa, ka, va, qc, kc, vc, ckr, cvr = _inproj(
            xs, g, w_in_b, l, *rope_s, nb * t, nb * t)
        unmasked = bias[N_BIAS - 1, :, :t, :A_WIN + t].reshape(A_HEADS * t, A_WIN + t)
        ys, ul = _sample_step(
            l, lam4, g2, unmasked[:, :A_WIN], unmasked[:, A_WIN:], gate, qa, ka, va, qc, kc, vc,
            cak_t, cav_t, cck_t, ccv_t, state_conv[l], cw, t, lam_init)
        xs = _merge(xs, (ys,), w_out_b, l, fg, nb * t, final)
        outs["sak"].append(akv[:, :A_WIDTH].reshape(nb, t, A_HEADS, HEAD_DIM))
        outs["sav"].append(akv[:, A_WIDTH:].reshape(nb, t, A_HEADS, HEAD_DIM))
        outs["scv"].append(ul[:, 8 - (CONV_WIDTH - 1):])
        outs["sck"].append(ckr.reshape(nb, t, C_HEADS, 2, C_QK_DIM))
        outs["scc"].append(cvr.reshape(nb, t, C_HEADS, C_V_DIM))

    xp = _merge(xp, pending[0], w_out_b, depth - 1, fg, TS_MERGE, True)
    st = lambda k: jnp.stack(outs[k])
    ck_all, cv_all = kv_acc
    pck = jnp.transpose(ck_all.reshape(depth, C_HEADS, 2, C_QK_DIM, seq), (0, 4, 1, 2, 3))[:, None]
    pcc = jnp.transpose(cv_all.reshape(depth, C_HEADS, C_V_DIM, seq), (0, 3, 1, 2))[:, None]
    return (xp.reshape(batch, seq, d_model), xs.reshape(nb, t, d_model),
            st("pak"), st("pav"), st("pcv"), pck, pcc,
            st("sak"), st("sav"), st("scv"), st("sck"), st("scc"))
```

```python
import functools
import math

import numpy as np
import jax
import jax.numpy as jnp
from jax import lax
from jax.experimental import pallas as pl
from jax.experimental.pallas import tpu as pltpu

F32 = jnp.float32
BF16 = jnp.bfloat16

CHUNK = 64
HEAD_DIM = 64
A_HEADS = 8
A_WIDTH = A_HEADS * HEAD_DIM
A_PREV_CHUNKS = 8
A_WIN = A_PREV_CHUNKS * CHUNK
REL_CLIP = 128
B_WIDTH = 256
CONV_WIDTH = 3
C_HEADS = 4
C_QK_DIM = 32
C_V_DIM = 64
C_WIDTH = C_HEADS * C_V_DIM
C_MAPS = 2 * C_HEADS
ROPE_DIMS = 8
ROPE_THETA = 500000.0
NORM_EPS = 1e-6
SUBLN_EPS = 1e-5
D_IN_PROJ = 4096
COL_AQ, COL_AK, COL_AV, COL_AG = 0, 512, 1024, 1536
COL_B = 2048
COL_CQ, COL_CK, COL_CV, COL_CG = 3072, 3328, 3584, 3840
GATE_W = A_WIDTH + 4 * B_WIDTH + C_WIDTH
G_AG, G_B, G_CG = 0, A_WIDTH, A_WIDTH + 4 * B_WIDTH

LANES = 128
HALF = LANES // 2
NEG = -1e30
LOG2E = math.log2(math.e)
A_QSCALE = HEAD_DIM ** -0.5 * LOG2E
C_QSCALE = C_QK_DIM ** -0.5 * LOG2E
VMEM_LIMIT = 48 * 1024 * 1024

TS_PROMPT = 512
TS_MERGE = 1024
TQ_A = 256
WIN_A = A_WIN + TQ_A
ROLL_A = 1024
N_BIAS = 3
T_C = 512
CACHE_RING = 3


def _nt_dot(a, b):
    return lax.dot_general(a, b, (((1,), (1,)), ((), ())), preferred_element_type=F32)


def _normalize_pair(acc_even, acc_odd, lo):
    o = jnp.where(lo, acc_even, acc_odd)
    l = jnp.where(lo, acc_odd, acc_even)
    return o / pltpu.roll(l, HALF, 1)


def _cparams(n_axes=1):
    return pltpu.CompilerParams(dimension_semantics=("parallel",) * n_axes,
                                vmem_limit_bytes=VMEM_LIMIT)


def _resident(shape):
    nd = len(shape)
    return pl.BlockSpec(shape, lambda *_: (0,) * nd, pipeline_mode=pl.Buffered(1))


def _layer_weights(w_all, layer):
    return pl.BlockSpec((1,) + w_all.shape[1:], lambda *_: (layer, 0, 0), pipeline_mode=pl.Buffered(1))


def _rope(x, cos, sa, sb):
    return (x * cos + pltpu.roll(x, LANES - ROPE_DIMS // 2, 1) * sa
            + pltpu.roll(x, ROPE_DIMS // 2, 1) * sb)


def _rope_coefficients(base, rt_ref):
    cb, sb_ = base[0:1], base[1:2]
    ct, st = rt_ref[0], rt_ref[1]
    cos_p = cb * ct - sb_ * st
    sin_p = sb_ * ct + cb * st
    d = lax.broadcasted_iota(jnp.int32, cos_p.shape, 1) % C_QK_DIM
    half = ROPE_DIMS // 2
    cos = jnp.where(d < ROPE_DIMS, cos_p, 1.0)
    sa = jnp.where(d < half, -sin_p, 0.0)
    sb = jnp.where((d >= half) & (d < ROPE_DIMS), sin_p, 0.0)
    return cos, sa, sb


def _inproj_kernel(*refs, prompt, n_aliased, zero_rest, merged):
    x_ref, g_ref, w_ref, rb_ref, rt_ref = refs[:5]
    if prompt:
        cw_ref = refs[5]
        n_merge_in = 4 if merged else 0
        y_refs, wo_ref = refs[6:6 + n_merge_in - 1], refs[6 + n_merge_in - 1]
        outs = refs[6 + n_merge_in + n_aliased:]
        (acg_ref, akv_ref, qa_ref, ka_ref, va_ref, qc_ref, kc_ref, vc_ref, ckr_ref, cvr_ref,
         yb_ref, tail_ref) = outs[:12]
        carry_ref = outs[-1]
    else:
        gate_ref, akv_ref, qa_ref, ka_ref, va_ref, qc_ref, kc_ref, vc_ref, ckr_ref, cvr_ref = refs[5:]

    if prompt:
        @pl.when(pl.program_id(0) == 0)
        def _():
            carry_ref[...] = jnp.zeros(carry_ref.shape, F32)

    x = x_ref[...]
    if merged:
        y = jnp.concatenate([r[...] for r in y_refs], axis=-1)
        x = x + jnp.dot(y, wo_ref[0], preferred_element_type=F32)
        outs[12][...] = x
    ms = jnp.mean(x * x, axis=-1, keepdims=True)
    h = (x * lax.rsqrt(ms + NORM_EPS) * g_ref[...]).astype(BF16)
    cos, sa, sb = _rope_coefficients(rb_ref[0], rt_ref)

    def proj(lo, width):
        return jnp.dot(h, w_ref[0, :, lo:lo + width], preferred_element_type=F32)

    def put_f32(ref, t, val):
        if prompt:
            ref[0, t * LANES:(t + 1) * LANES, :] = val.T
            if zero_rest:
                ref[1:, t * LANES:(t + 1) * LANES, :] = jnp.zeros(
                    (ref.shape[0] - 1, LANES, val.shape[0]), F32)
        else:
            ref[:, t * LANES:(t + 1) * LANES] = val

    zc = proj(COL_CQ, 2 * C_WIDTH)
    for t in range(C_WIDTH // LANES):
        cq = _rope(zc[:, t * LANES:(t + 1) * LANES], cos, sa, sb)
        qc_ref[:, t * LANES:(t + 1) * LANES] = (cq * C_QSCALE).astype(BF16)
        ck = _rope(zc[:, C_WIDTH + t * LANES:C_WIDTH + (t + 1) * LANES], cos, sa, sb)
        kc_ref[:, t * LANES:(t + 1) * LANES] = ck.astype(BF16)
        put_f32(ckr_ref, t, ck)
    zc = proj(COL_CV, 2 * C_WIDTH)
    vc_ref[...] = zc[:, :C_WIDTH].astype(BF16)
    for t in range(C_WIDTH // LANES):
        put_f32(cvr_ref, t, zc[:, t * LANES:(t + 1) * LANES])
    if prompt:
        acg_ref[:, A_WIDTH:] = zc[:, C_WIDTH:]
    else:
        gate_ref[:, G_CG:G_CG + C_WIDTH] = zc[:, C_WIDTH:]

    zb = proj(COL_B, 4 * B_WIDTH)
    if prompt:
        bw = B_WIDTH
        u = zb[:, bw:2 * bw] * zb[:, 2 * bw:3 * bw]
        prev = carry_ref[...]
        u2, u1 = _conv_taps(u, prev[6:7], prev[7:8])
        cw = cw_ref[...]
        conv = u2 * cw[0:1] + u1 * cw[1:2] + u * cw[2:3]
        yb_ref[...] = (zb[:, :bw] * conv * _silu(zb[:, 3 * bw:])).astype(BF16)
        carry_ref[...] = u[u.shape[0] - 8:]
        tail_ref[...] = u[u.shape[0] - 8:]
    else:
        gate_ref[:, G_B:G_B + 4 * B_WIDTH] = zb

    zc = proj(COL_AK, A_WIDTH)
    ka_ref[...] = zc.astype(BF16)
    akv_ref[:, :A_WIDTH] = zc
    zc = proj(COL_AV, A_WIDTH)
    va_ref[...] = zc.astype(BF16)
    akv_ref[:, A_WIDTH:] = zc
    zc = proj(COL_AG, A_WIDTH)
    if prompt:
        acg_ref[:, :A_WIDTH] = zc
    else:
        gate_ref[:, G_AG:G_AG + A_WIDTH] = zc
    qa_ref[...] = (proj(COL_AQ, A_WIDTH) * A_QSCALE).astype(BF16)


def _inproj(x, g, w_all, layer, rope_base, rope_off, ts, keep, cw=None, kv_acc=None, merge_in=None):
    prompt = cw is not None
    depth = w_all.shape[0]
    n, d = x.shape
    nt = n // ts
    first_keep = nt - keep // ts
    row = lambda width: pl.BlockSpec((ts, width), lambda i: (i, 0))
    sds = jax.ShapeDtypeStruct
    if prompt:
        kv_shape = (depth, C_WIDTH, n)
        slabs = depth if kv_acc is None else 1
        kv_spec = pl.BlockSpec((slabs, C_WIDTH, ts), lambda i: (layer, 0, i))
        gate_w = A_WIDTH + C_WIDTH
    else:
        kv_shape, kv_spec = (n, C_WIDTH), row(C_WIDTH)
        gate_w = GATE_W
    akv_spec = pl.BlockSpec((ts, 2 * A_WIDTH), lambda i: (jnp.maximum(i - first_keep, 0), 0))
    out_shape = [sds((n, gate_w), F32), sds((keep, 2 * A_WIDTH), F32),
                 sds((n, A_WIDTH), BF16), sds((n, A_WIDTH), BF16), sds((n, A_WIDTH), BF16),
                 sds((n, C_WIDTH), BF16), sds((n, C_WIDTH), BF16), sds((n, C_WIDTH), BF16),
                 sds(kv_shape, F32), sds(kv_shape, F32)]
    out_specs = [row(gate_w), akv_spec, row(A_WIDTH), row(A_WIDTH), row(A_WIDTH),
                 row(C_WIDTH), row(C_WIDTH), row(C_WIDTH), kv_spec, kv_spec]
    in_specs = [row(d), _resident((1, d)),
                _layer_weights(w_all, layer),
                pl.BlockSpec((1, 2, LANES), lambda i: (i, 0, 0)), _resident(rope_off.shape)]
    args = [x, g, w_all, rope_base, rope_off]
    scratch = []
    aliases = {}
    if prompt:
        in_specs.append(_resident(cw.shape))
        args.append(cw)
        if merge_in is not None:
            ys, wo_all, prev_layer = merge_in
            in_specs += [row(y.shape[1]) for y in ys] + [_layer_weights(wo_all, prev_layer)]
            args += list(ys) + [wo_all]
        if kv_acc is not None:
            aliases = {len(args) + j: 8 + j for j in range(len(kv_acc))}
            in_specs += [pl.BlockSpec(memory_space=pl.ANY)] * len(kv_acc)
            args += list(kv_acc)
        out_shape += [sds((n, B_WIDTH), BF16), sds((8, B_WIDTH), F32)]
        out_specs += [row(B_WIDTH), pl.BlockSpec((8, B_WIDTH), lambda i: (0, 0))]
        if merge_in is not None:
            out_shape.append(sds((n, d), F32))
            out_specs.append(row(d))
        scratch = [pltpu.VMEM((8, B_WIDTH), F32)]
    return pl.pallas_call(
        functools.partial(_inproj_kernel, prompt=prompt, n_aliased=len(aliases),
                          zero_rest=prompt and kv_acc is None and depth > 1,
                          merged=merge_in is not None),
        grid=(nt,),
        in_specs=in_specs,
        out_specs=tuple(out_specs),
        out_shape=tuple(out_shape),
        scratch_shapes=scratch,
        input_output_aliases=aliases,
        compiler_params=pltpu.CompilerParams(dimension_semantics=("arbitrary",),
                                             vmem_limit_bytes=VMEM_LIMIT),
        name="inproj",
    )(*args)


def _rope_angle_tables(tile_base, row_off):
    half = ROPE_DIMS // 2
    inv_freq = ROPE_THETA ** (-jnp.arange(half, dtype=F32) * (2.0 / ROPE_DIMS))
    lane_freq = inv_freq[(np.arange(LANES) % C_QK_DIM) % half][None, :]
    a = tile_base.astype(F32)[:, None] * lane_freq
    b = row_off.astype(F32)[:, None] * lane_freq
    return jnp.stack([jnp.cos(a), jnp.sin(a)], axis=1), jnp.stack([jnp.cos(b), jnp.sin(b)], axis=0)


def _bias_kernel(u_ref, b_ref):
    v = pl.program_id(0)
    qc = lax.broadcasted_iota(jnp.int32, (TQ_A, WIN_A), 0) // CHUNK
    kc = lax.broadcasted_iota(jnp.int32, (TQ_A, WIN_A), 1) // CHUNK
    top = qc + v * (TQ_A // CHUNK)
    valid = (kc <= top) & (kc >= top - A_PREV_CHUNKS)
    for h in range(A_HEADS):
        u = jnp.broadcast_to(u_ref[0, h], (TQ_A, ROLL_A))
        t = pltpu.roll(u, 0, 1, stride=1, stride_axis=0)[:, ROLL_A - WIN_A:]
        b_ref[0, h] = jnp.where(valid, t * LOG2E, NEG)


def _bias_tables(rel_bias_l):
    tab = rel_bias_l.astype(F32)
    rows = []
    for off in (0, TQ_A, A_WIN):
        n_hi = off + (ROLL_A - WIN_A) - REL_CLIP
        n_lo = ROLL_A - n_hi - (2 * REL_CLIP + 1)
        rows.append(jnp.concatenate([jnp.broadcast_to(tab[:, -1:], (A_HEADS, n_hi)), tab[:, ::-1],
                                     jnp.broadcast_to(tab[:, :1], (A_HEADS, n_lo))], axis=1))
    u = jnp.stack(rows)[:, :, None, :]
    return pl.pallas_call(
        _bias_kernel,
        grid=(N_BIAS,),
        in_specs=[pl.BlockSpec((1, A_HEADS, 1, ROLL_A), lambda v: (v, 0, 0, 0))],
        out_specs=pl.BlockSpec((1, A_HEADS, TQ_A, WIN_A), lambda v: (v, 0, 0, 0)),
        out_shape=jax.ShapeDtypeStruct((N_BIAS, A_HEADS, TQ_A, WIN_A), F32),
        compiler_params=_cparams(),
        name="bias",
    )(u)


def _band_kernel(q_ref, k0_ref, k1_ref, k2_ref, v0_ref, v1_ref, v2_ref, b_ref, ag_ref, o_ref, s_ref):
    i = pl.program_id(0)
    n_pair = A_WIDTH // LANES
    lo = lax.broadcasted_iota(jnp.int32, (TQ_A, LANES), 1) < HALF
    lo_w = lax.broadcasted_iota(jnp.int32, (WIN_A, LANES), 1) < HALF

    @pl.when(i == 0)
    def _():
        s_ref[1] = jnp.zeros(s_ref.shape[1:], F32)

    def probabilities(slot, h, steady):
        if not steady:
            s = s_ref[slot, h * TQ_A:(h + 1) * TQ_A, :] + b_ref[0, h]
            return jnp.exp2(s - jnp.max(s, axis=-1, keepdims=True)).astype(BF16)
        parts = []
        for a in range(TQ_A // CHUNK):
            first = (a * CHUNK) // LANES
            last = ((a + A_PREV_CHUNKS) * CHUNK) // LANES
            cols = slice(first * LANES, (last + 1) * LANES)
            sa = (s_ref[slot, h * TQ_A + a * CHUNK:h * TQ_A + (a + 1) * CHUNK, cols]
                  + b_ref[0, h, a * CHUNK:(a + 1) * CHUNK, cols])
            ea = jnp.exp2(sa - jnp.max(sa, axis=-1, keepdims=True)).astype(BF16)
            pads = [jnp.zeros((CHUNK, first * LANES), BF16), ea,
                    jnp.zeros((CHUNK, WIN_A - (last + 1) * LANES), BF16)]
            parts.append(jnp.concatenate([x for x in pads if x.shape[1]], axis=1))
        return jnp.concatenate(parts, axis=0)

    def step(cur, steady):
        for p in range(n_pair):
            cols = slice(p * LANES, (p + 1) * LANES)
            q2 = q_ref[:, cols]
            qs = jnp.concatenate([jnp.where(lo, q2, jnp.zeros_like(q2)),
                                  jnp.where(lo, jnp.zeros_like(q2), q2)], axis=0)
            kw = jnp.concatenate([r[:, cols] for r in (k0_ref, k1_ref, k2_ref)], axis=0)
            s_ref[cur, 2 * p * TQ_A:(2 * p + 2) * TQ_A, :] = _nt_dot(qs, kw)
            vw = jnp.concatenate([r[:, cols] for r in (v0_ref, v1_ref, v2_ref)], axis=0)
            outs = []
            for half in range(2):
                h = 2 * p + half
                sel_w = lo_w if half == 0 else jnp.logical_not(lo_w)
                v_aug = jnp.where(sel_w, vw, jnp.ones_like(vw))
                e = probabilities(1 - cur, h, steady)
                outs.append(jnp.dot(e, v_aug, preferred_element_type=F32))
            o_ref[:, cols] = (_normalize_pair(outs[0], outs[1], lo) * _silu(ag_ref[:, cols])).astype(BF16)

    parity = jnp.bitwise_and(i, 1)
    steady = i > A_WIN // TQ_A
    for cur in range(2):
        pl.when(jnp.logical_and(parity == cur, steady))(functools.partial(step, cur, True))
        pl.when(jnp.logical_and(parity == cur, jnp.logical_not(steady)))(functools.partial(step, cur, False))


def _band_attention(qa, ka, va, bias, acg):
    s = qa.shape[0]
    nq = s // TQ_A
    q_tile = lambda i: jnp.minimum(i, nq - 1)
    o_tile = lambda i: jnp.maximum(i - 1, 0)
    first = lambda tile: jnp.maximum(tile - A_WIN // TQ_A, 0)
    k_spec = lambda j: pl.BlockSpec((TQ_A, A_WIDTH), lambda i: (first(q_tile(i)) + j, 0))
    v_spec = lambda j: pl.BlockSpec((TQ_A, A_WIDTH), lambda i: (first(o_tile(i)) + j, 0))
    return pl.pallas_call(
        _band_kernel,
        grid=(nq + 1,),
        in_specs=[pl.BlockSpec((TQ_A, A_WIDTH), lambda i: (q_tile(i), 0)),
                  k_spec(0), k_spec(1), k_spec(2), v_spec(0), v_spec(1), v_spec(2),
                  pl.BlockSpec((1, A_HEADS, TQ_A, WIN_A),
                               lambda i: (jnp.minimum(o_tile(i), N_BIAS - 1), 0, 0, 0)),
                  pl.BlockSpec((TQ_A, A_WIDTH), lambda i: (o_tile(i), 0))],
        out_specs=pl.BlockSpec((TQ_A, A_WIDTH), lambda i: (o_tile(i), 0)),
        out_shape=jax.ShapeDtypeStruct((s, A_WIDTH), BF16),
        scratch_shapes=[pltpu.VMEM((2, A_HEADS * TQ_A, WIN_A), F32)],
        compiler_params=pltpu.CompilerParams(dimension_semantics=("arbitrary",),
                                             vmem_limit_bytes=VMEM_LIMIT),
        name="band_attn",
    )(qa, ka, ka, ka, va, va, va, bias, acg)


def _diff_lambda(lam_ref, lam_init):
    a = jnp.sum(lam_ref[0:1, :] * lam_ref[1:2, :], axis=-1, keepdims=True)
    b = jnp.sum(lam_ref[2:3, :] * lam_ref[3:4, :], axis=-1, keepdims=True)
    return jnp.exp(a) - jnp.exp(b) + lam_init


def _subln(d, g2, lam_init):
    lo = lax.broadcasted_iota(jnp.int32, d.shape, 1) < HALF
    sq = d * d
    s_lo = jnp.sum(jnp.where(lo, sq, 0.0), axis=-1, keepdims=True)
    s_hi = jnp.sum(jnp.where(lo, 0.0, sq), axis=-1, keepdims=True)
    ms = jnp.where(lo, s_lo, s_hi) * (1.0 / C_V_DIM)
    return d * lax.rsqrt(ms + SUBLN_EPS) * g2 * (1.0 - lam_init)


def _diff_kernel(lam_ref, g_ref, q_ref, k_hbm, v_hbm, cg_ref, o_ref, qm_ref, s_ref, m_ref, acc_ref,
                 k_ref, v_ref, kv_sem, *, lam_init, n_tiles):
    i = pl.program_id(0)
    t = T_C

    def kv_fetch(tile):
        rows = pl.ds(pl.multiple_of(tile * t, t), t)
        par = tile % 2
        return [pltpu.make_async_copy(k_hbm.at[rows], k_ref.at[rows], kv_sem.at[par, 0]),
                pltpu.make_async_copy(v_hbm.at[rows], v_ref.at[rows], kv_sem.at[par, 1])]

    @pl.when(i == 0)
    def _():
        for c in kv_fetch(0):
            c.start()

    @pl.when(i + 1 < n_tiles)
    def _():
        for c in kv_fetch(i + 1):
            c.start()

    for c in kv_fetch(i):
        c.wait()
    n_grp = C_HEADS // 2
    per_grp = C_MAPS // n_grp
    q = q_ref[...]
    lane_grp = lax.broadcasted_iota(jnp.int32, (t, C_WIDTH), 1) // C_QK_DIM
    lo = lax.broadcasted_iota(jnp.int32, (t, LANES), 1) < HALF

    def mask_queries(g):
        for loc in range(per_grp):
            hm = g * per_grp + loc
            qm_ref[g, loc * t:(loc + 1) * t, :] = jnp.where(lane_grp == hm, q, jnp.zeros_like(q))

    def k_tile(j):
        return k_ref[pl.ds(pl.multiple_of(j * t, t), t), :]

    def v_tile(j):
        return v_ref[pl.ds(pl.multiple_of(j * t, t), t), :]

    def scores(g, k_t):
        s_ref[g] = _nt_dot(qm_ref[g], k_t)

    def diagonal_probs(s, m_old):
        per_tile = LANES // CHUNK
        part_lane = lax.broadcasted_iota(jnp.int32, (CHUNK, LANES), 1)
        m_parts, p_parts = [], []
        for a in range(t // CHUNK):
            rows = slice(a * CHUNK, (a + 1) * CHUNK)
            n_full, part = divmod(a + 1, per_tile)
            pieces = [s[rows, :n_full * LANES]] if n_full else []
            if part:
                edge = s[rows, n_full * LANES:(n_full + 1) * LANES]
                pieces.append(jnp.where(part_lane < part * CHUNK, edge, -jnp.inf))
            sa = jnp.concatenate(pieces, axis=1) if len(pieces) > 1 else pieces[0]
            width = sa.shape[1]
            mn = jnp.maximum(m_old[rows], jnp.max(sa, axis=-1, keepdims=True))
            pa = jnp.exp2(sa - jnp.tile(mn, (1, width // LANES))).astype(BF16)
            if width < t:
                pa = jnp.concatenate([pa, jnp.zeros((CHUNK, t - width), BF16)], axis=1)
            m_parts.append(mn)
            p_parts.append(pa)
        return jnp.concatenate(m_parts, axis=0), jnp.concatenate(p_parts, axis=0)

    def softmax_pv(g, v_t, diagonal):
        v2 = v_t[:, g * LANES:(g + 1) * LANES]
        for half in range(2):
            h = 2 * g + half
            sel = lo if half == 0 else jnp.logical_not(lo)
            v_aug = jnp.where(sel, v2, jnp.ones_like(v2))
            ps, alphas = [], []
            for mp in range(2):
                hm = 2 * h + mp
                loc = hm - g * per_grp
                s = s_ref[g, loc * t:(loc + 1) * t, :]
                m_old = m_ref[hm]
                if diagonal:
                    m_new, p = diagonal_probs(s, m_old)
                else:
                    m_new = jnp.maximum(m_old, jnp.max(s, axis=-1, keepdims=True))
                    p = jnp.exp2(s - jnp.tile(m_new, (1, t // LANES))).astype(BF16)
                alphas.append(jnp.exp2(m_old - m_new))
                ps.append(p)
                m_ref[hm] = m_new
            pv = jnp.dot(jnp.concatenate(ps, axis=0), v_aug, preferred_element_type=F32)
            for mp in range(2):
                hm = 2 * h + mp
                acc_ref[hm] = alphas[mp] * acc_ref[hm] + pv[mp * t:(mp + 1) * t]

    def finalize(p_, lam):
        r = [_normalize_pair(acc_ref[4 * p_ + mp], acc_ref[4 * p_ + 2 + mp], lo) for mp in range(2)]
        pair = r[0] - lam * r[1]
        tile = slice(p_ * LANES, (p_ + 1) * LANES)
        o_ref[:, tile] = (_subln(pair, g_ref[...], lam_init) * _silu(cg_ref[:, tile])).astype(BF16)

    mask_queries(0)
    scores(0, k_tile(0))
    mask_queries(1)
    m_ref[...] = jnp.full(m_ref.shape, -jnp.inf, F32)
    acc_ref[...] = jnp.zeros(acc_ref.shape, F32)

    def full_tile(j):
        v_t = v_tile(j)
        scores(1, k_tile(j))
        softmax_pv(0, v_t, False)
        scores(0, k_tile(j + 1))
        softmax_pv(1, v_t, False)

    odd = jnp.bitwise_and(i, 1)

    @pl.when(odd == 1)
    def _():
        full_tile(0)

    def body(jj, carry):
        j = odd + 2 * jj
        full_tile(j)
        full_tile(j + 1)
        return carry

    lax.fori_loop(0, lax.shift_right_logical(i, 1), body, 0)
    v_t = v_tile(i)
    lam = _diff_lambda(lam_ref, lam_init)
    scores(1, k_tile(i))
    softmax_pv(0, v_t, True)
    finalize(0, lam)
    softmax_pv(1, v_t, True)
    finalize(1, lam)


def _diff_attention(lam4, g2, qc, kc, vc, acg, lam_init):
    s = qc.shape[0]
    t = T_C
    n_grp = C_HEADS // 2
    return pl.pallas_call(
        functools.partial(_diff_kernel, lam_init=lam_init, n_tiles=s // t),
        grid=(s // t,),
        in_specs=[_resident((4, C_QK_DIM)), _resident((1, LANES)),
                  pl.BlockSpec((t, C_WIDTH), lambda i: (i, 0)),
                  pl.BlockSpec(memory_space=pl.ANY), pl.BlockSpec(memory_space=pl.ANY),
                  pl.BlockSpec((t, C_WIDTH), lambda i: (i, A_WIDTH // C_WIDTH))],
        out_specs=pl.BlockSpec((t, C_WIDTH), lambda i: (i, 0)),
        out_shape=jax.ShapeDtypeStruct((s, C_WIDTH), BF16),
        scratch_shapes=[pltpu.VMEM((n_grp, C_MAPS // n_grp * t, C_WIDTH), BF16),
                        pltpu.VMEM((n_grp, C_MAPS // n_grp * t, t), F32),
                        pltpu.VMEM((C_MAPS, t, LANES), F32),
                        pltpu.VMEM((C_MAPS, t, LANES), F32),
                        pltpu.VMEM((s, C_WIDTH), BF16), pltpu.VMEM((s, C_WIDTH), BF16),
                        pltpu.SemaphoreType.DMA((2, 2))],
        compiler_params=pltpu.CompilerParams(dimension_semantics=("arbitrary",),
                                             vmem_limit_bytes=VMEM_LIMIT),
        name="diff_attn",
    )(lam4, g2, qc, kc, vc, acg)


def _silu(t):
    return t * jax.nn.sigmoid(t)


def _conv_input(gate):
    b = gate[:, G_B:G_B + 4 * B_WIDTH]
    return b[:, B_WIDTH:2 * B_WIDTH] * b[:, 2 * B_WIDTH:3 * B_WIDTH]


def _conv_taps(u, prev0, prev1):
    row = lax.broadcasted_iota(jnp.int32, u.shape, 0)
    u1 = jnp.where(row == 0, prev1, pltpu.roll(u, 1, 0))
    u2 = jnp.where(row == 0, prev0, jnp.where(row == 1, prev1, pltpu.roll(u, 2, 0)))
    return u2, u1


def _gated_mix(gate, oa, oc, u2, u1, u, cw):
    bw = B_WIDTH
    conv = u2 * cw[0:1] + u1 * cw[1:2] + u * cw[2:3]
    ob = gate[:, G_B:G_B + bw] * conv
    y = jnp.concatenate([oa * _silu(gate[:, G_AG:G_AG + A_WIDTH]),
                         ob * _silu(gate[:, G_B + 3 * bw:G_B + 4 * bw]),
                         oc * _silu(gate[:, G_CG:G_CG + C_WIDTH])], axis=-1)
    return y.astype(BF16)


def _final_norm(x, fg):
    ms = jnp.mean(x * x, axis=-1, keepdims=True)
    return x * lax.rsqrt(ms + NORM_EPS) * fg


def _ring_schedule(step, n_steps, fetch):
    @pl.when(step == 0)
    def _():
        for s in range(min(CACHE_RING - 1, n_steps)):
            for c in fetch(s, s):
                c.start()

    ahead = step + (CACHE_RING - 1)

    @pl.when(ahead < n_steps)
    def _():
        for c in fetch(ahead, ahead % CACHE_RING):
            c.start()

    slot = step % CACHE_RING
    for c in fetch(step, slot):
        c.wait()
    return slot


def _merge_kernel(*refs, final, n_src, ring_steps):
    srcs = refs[:n_src]
    w_ref, fg_ref, xo_ref = refs[n_src:n_src + 3]
    if ring_steps:
        bufs, sem = refs[n_src + 3:2 * n_src + 3], refs[2 * n_src + 3]
        ts = xo_ref.shape[0]

        def fetch(step, slot):
            rows = pl.ds(pl.multiple_of(step * ts, ts), ts)
            return [pltpu.make_async_copy(src.at[rows], buf.at[slot], sem.at[slot, j])
                    for j, (src, buf) in enumerate(zip(srcs, bufs))]

        slot = _ring_schedule(pl.program_id(0), ring_steps, fetch)
        x, ys = bufs[0][slot], [buf[slot] for buf in bufs[1:]]
    else:
        x, ys = srcs[0][...], [r[...] for r in srcs[1:]]
    y = jnp.concatenate(ys, axis=-1) if len(ys) > 1 else ys[0]
    out = x + jnp.dot(y, w_ref[0], preferred_element_type=F32)
    xo_ref[...] = _final_norm(out, fg_ref[...]) if final else out


def _merge(x, ys, w_all, layer, fg, ts, final):
    n, d = x.shape
    steps = n // ts
    srcs = (x,) + tuple(ys)
    row = lambda width: pl.BlockSpec((ts, width), lambda i: (i, 0))
    ring = steps >= CACHE_RING
    if ring:
        src_specs = [pl.BlockSpec(memory_space=pl.ANY)] * len(srcs)
        scratch = [pltpu.VMEM((CACHE_RING, ts, a.shape[1]), a.dtype) for a in srcs]
        scratch.append(pltpu.SemaphoreType.DMA((CACHE_RING, len(srcs))))
    else:
        src_specs, scratch = [row(a.shape[1]) for a in srcs], []
    return pl.pallas_call(
        functools.partial(_merge_kernel, final=final, n_src=len(srcs), ring_steps=steps if ring else 0),
        grid=(steps,),
        in_specs=src_specs + [_layer_weights(w_all, layer), _resident((1, d))],
        out_specs=row(d),
        out_shape=jax.ShapeDtypeStruct((n, d), F32),
        scratch_shapes=scratch,
        compiler_params=pltpu.CompilerParams(dimension_semantics=("arbitrary",),
                                             vmem_limit_bytes=VMEM_LIMIT),
        name="merge",
    )(*srcs, w_all, fg)


def _stack_heads(q, n_groups, width):
    grp = lax.broadcasted_iota(jnp.int32, q.shape, 1) // width
    return jnp.concatenate([jnp.where(grp == g, q, jnp.zeros_like(q)) for g in range(n_groups)], axis=0)


def _cached_attention(qs, kt_c, vt_c, k_n, v_n, bias_c, bias_n):
    s_c = jnp.dot(qs, kt_c, preferred_element_type=F32)
    s_n = _nt_dot(qs, k_n)
    if bias_c is not None:
        s_c, s_n = s_c + bias_c, s_n + bias_n
    m = jnp.maximum(jnp.max(s_c, axis=-1, keepdims=True), jnp.max(s_n, axis=-1, keepdims=True))
    e_c = jnp.exp2(s_c - m)
    e_n = jnp.exp2(s_n - m)
    l = jnp.sum(e_c, axis=-1, keepdims=True) + jnp.sum(e_n, axis=-1, keepdims=True)
    o = _nt_dot(e_c.astype(BF16), vt_c) + jnp.dot(e_n.astype(BF16), v_n, preferred_element_type=F32)
    return o / l


def _sample_kernel(lam_ref, g_ref, bc_ref, bn_ref,
                   qa_ref, ka_ref, va_ref, qc_ref, kc_ref, vc_ref, gate_ref,
                   cak_hbm, cav_hbm, cck_hbm, ccv_hbm, conv_ref, cw_ref,
                   y_ref, ul_ref, cak_buf, cav_buf, cck_buf, ccv_buf, sem, *, lam_init, layer, nb):
    b = pl.program_id(0)
    streams = ((cak_hbm, cak_buf), (cav_hbm, cav_buf), (cck_hbm, cck_buf), (ccv_hbm, ccv_buf))

    def fetch(step, slot):
        return [pltpu.make_async_copy(hbm.at[layer, step], buf.at[slot], sem.at[slot, j])
                for j, (hbm, buf) in enumerate(streams)]

    slot = _ring_schedule(b, nb, fetch)
    t = qa_ref.shape[0]
    qs = _stack_heads(qa_ref[...], A_HEADS, HEAD_DIM)
    of = _cached_attention(qs, cak_buf[slot].astype(BF16), cav_buf[slot].astype(BF16),
                           ka_ref[...], va_ref[...], bc_ref[...], bn_ref[...])
    grp = lax.broadcasted_iota(jnp.int32, (t, A_WIDTH), 1) // HEAD_DIM
    oa = jnp.zeros((t, A_WIDTH), F32)
    for h in range(A_HEADS):
        oa = jnp.where(grp == h, of[h * t:(h + 1) * t], oa)
    qs = _stack_heads(qc_ref[...], C_MAPS, C_QK_DIM)
    of = _cached_attention(qs, cck_buf[slot].astype(BF16), ccv_buf[slot].astype(BF16),
                           kc_ref[...], vc_ref[...], None, None)
    lam = _diff_lambda(lam_ref, lam_init)
    grp = lax.broadcasted_iota(jnp.int32, (t, C_WIDTH), 1) // C_V_DIM
    d = jnp.zeros((t, C_WIDTH), F32)
    for h in range(C_HEADS):
        dh = of[2 * h * t:(2 * h + 1) * t] - lam * of[(2 * h + 1) * t:(2 * h + 2) * t]
        d = jnp.where(grp == h, dh, d)
    oc = jnp.concatenate([_subln(d[:, p * LANES:(p + 1) * LANES], g_ref[...], lam_init)
                          for p in range(C_WIDTH // LANES)], axis=-1)
    gate = gate_ref[...]
    u = _conv_input(gate)
    prev = conv_ref[0]
    u2, u1 = _conv_taps(u, prev[0:1], prev[1:2])
    y_ref[...] = _gated_mix(gate, oa, oc, u2, u1, u, cw_ref[...])
    ul_ref[0] = u[t - 8:]


def _sample_step(layer, lam4, g2, bias_c, bias_n, gate, qa, ka, va, qc, kc, vc,
                 cak_t, cav_t, cck_t, ccv_t, conv, cw, t, lam_init):
    n = gate.shape[0]
    d_mix = A_WIDTH + B_WIDTH + C_WIDTH
    nb = n // t
    row = lambda width: pl.BlockSpec((t, width), lambda b: (b, 0))
    in_hbm = pl.BlockSpec(memory_space=pl.ANY)
    ring = lambda a: pltpu.VMEM((CACHE_RING,) + a.shape[2:], a.dtype)
    return pl.pallas_call(
        functools.partial(_sample_kernel, lam_init=lam_init, layer=layer, nb=nb),
        grid=(nb,),
        in_specs=[_resident((4, C_QK_DIM)), _resident((1, LANES)),
                  _resident(bias_c.shape), _resident(bias_n.shape),
                  row(A_WIDTH), row(A_WIDTH), row(A_WIDTH),
                  row(C_WIDTH), row(C_WIDTH), row(C_WIDTH), row(GATE_W),
                  in_hbm, in_hbm, in_hbm, in_hbm,
                  pl.BlockSpec((1,) + conv.shape[1:], lambda b: (b, 0, 0)),
                  _resident(cw.shape)],
        out_specs=(row(d_mix), pl.BlockSpec((1, 8, B_WIDTH), lambda b: (b, 0, 0))),
        out_shape=(jax.ShapeDtypeStruct((n, d_mix), BF16),
                   jax.ShapeDtypeStruct((nb, 8, B_WIDTH), F32)),
        scratch_shapes=[ring(cak_t), ring(cav_t), ring(cck_t), ring(ccv_t),
                        pltpu.SemaphoreType.DMA((CACHE_RING, 4))],
        compiler_params=pltpu.CompilerParams(dimension_semantics=("arbitrary",),
                                             vmem_limit_bytes=VMEM_LIMIT),
        name="sample_step",
    )(lam4, g2, bias_c, bias_n, qa, ka, va, qc, kc, vc, gate,
      cak_t, cav_t, cck_t, ccv_t, conv, cw)


def _feature_major(cache):
    nd = cache.ndim
    c = jnp.transpose(cache, (0, 1) + tuple(range(3, nd)) + (2,))
    return c.reshape(c.shape[0], c.shape[1], -1, c.shape[-1])


def kernel(x_prompt, x_sample, cache_a_k, cache_a_v, state_conv, cache_c_k, cache_c_v, norm_g, w_in, w_out, rel_bias, conv_w, lam_q1, lam_k1, lam_q2, lam_k2, subln_g, final_g):
    depth = norm_g.shape[0]
    batch, seq, d_model = x_prompt.shape
    nb, t, _ = x_sample.shape
    past = cache_c_k.shape[2]
    win = cache_a_k.shape[2]
    ts = TS_PROMPT
    assert batch == 1 and win == A_WIN and seq % T_C == 0 and seq % ts == 0 and seq >= WIN_A
    assert t % 16 == 0 and t <= CHUNK and past % LANES == 0
    keep = min(A_WIN, seq)
    assert keep % ts == 0

    xp = x_prompt.reshape(seq, d_model)
    xs = x_sample.reshape(nb * t, d_model)
    rope_p = _rope_angle_tables(jnp.arange(0, seq, ts), jnp.arange(ts))
    rope_s = _rope_angle_tables(jnp.full((1,), past), jnp.tile(jnp.arange(t), nb))
    w_in_b = w_in.astype(BF16)
    w_out_b = w_out.astype(BF16)
    fg = final_g.reshape(1, d_model).astype(F32)
    cak_t, cav_t = _feature_major(cache_a_k), _feature_major(cache_a_v)
    cck_t, ccv_t = _feature_major(cache_c_k), _feature_major(cache_c_v)

    outs = {k: [] for k in ("pak", "pav", "pcv", "sak", "sav", "scv", "sck", "scc")}
    kv_acc = None
    pending = None
    for l in range(depth):
        final = l == depth - 1
        lam_init = 0.8 - 0.6 * math.exp(-0.3 * l)
        g = norm_g[l].reshape(1, d_model).astype(F32)
        cw = conv_w[l].astype(F32)
        lam4 = jnp.stack([lam_q1[l], lam_k1[l], lam_q2[l], lam_k2[l]]).astype(F32)
        g2 = jnp.tile(subln_g[l].astype(F32), LANES // C_V_DIM).reshape(1, LANES)
        bias = _bias_tables(rel_bias[l])

        res = _inproj(xp, g, w_in_b, l, *rope_p, ts, keep, cw, kv_acc, pending)
        acg, akv, qa, ka, va, qc, kc, vc, ck_all, cv_all, yb, tail = res[:12]
        if pending is not None:
            xp = res[12]
        kv_acc = (ck_all, cv_all)
        ya = _band_attention(qa, ka, va, bias, acg)
        yc = _diff_attention(lam4, g2, qc, kc, vc, acg, lam_init)
        pending = ((ya, yb, yc), w_out_b, l)
        outs["pak"].append(akv[:, :A_WIDTH].reshape(1, keep, A_HEADS, HEAD_DIM))
        outs["pav"].append(akv[:, A_WIDTH:].reshape(1, keep, A_HEADS, HEAD_DIM))
        outs["pcv"].append(tail[8 - (CONV_WIDTH - 1):][None])

        gate, akv, qa, ka, va, qc, kc, vc, ckr, cvr = _inproj(
            xs, g, w_in_b, l, *rope_s, nb * t, nb * t)
        unmasked = bias[N_BIAS - 1, :, :t, :A_WIN + t].reshape(A_HEADS * t, A_WIN + t)
        ys, ul = _sample_step(
            l, lam4, g2, unmasked[:, :A_WIN], unmasked[:, A_WIN:], gate, qa, ka, va, qc, kc, vc,
            cak_t, cav_t, cck_t, ccv_t, state_conv[l], cw, t, lam_init)
        xs = _merge(xs, (ys,), w_out_b, l, fg, nb * t, final)
        outs["sak"].append(akv[:, :A_WIDTH].reshape(nb, t, A_HEADS, HEAD_DIM))
        outs["sav"].append(akv[:, A_WIDTH:].reshape(nb, t, A_HEADS, HEAD_DIM))
        outs["scv"].append(ul[:, 8 - (CONV_WIDTH - 1):])
        outs["sck"].append(ckr.reshape(nb, t, C_HEADS, 2, C_QK_DIM))
        outs["scc"].append(cvr.reshape(nb, t, C_HEADS, C_V_DIM))

    xp = _merge(xp, pending[0], w_out_b, depth - 1, fg, TS_MERGE, True)
    st = lambda k: jnp.stack(outs[k])
    ck_all, cv_all = kv_acc
    pck = jnp.transpose(ck_all.reshape(depth, C_HEADS, 2, C_QK_DIM, seq), (0, 4, 1, 2, 3))[:, None]
    pcc = jnp.transpose(cv_all.reshape(depth, C_HEADS, C_V_DIM, seq), (0, 3, 1, 2))[:, None]
    return (xp.reshape(batch, seq, d_model), xs.reshape(nb, t, d_model),
            st("pak"), st("pav"), st("pcv"), pck, pcc,
            st("sak"), st("sav"), st("scv"), st("sck"), st("scc"))
```
